```python
import jax, jax.numpy as jnp
from jax import lax
import numpy as np

D_MODEL = 1024
BATCH = 4
SEQ = 8192
DEPTH = 1
DEC_BATCH = 16
DEC_SEQ = 32
PAST_LEN = 1024

CHUNK = 64
N_HEADS = 16
N_KV_HEADS = 4
HEAD_DIM = 64
GROUP = N_HEADS // N_KV_HEADS
ROT_DIM = HEAD_DIM // 4
ROPE_THETA = 500000.0
WINDOW = 128
WIN_CHUNKS = WINDOW // CHUNK
ATTN_SCALE = HEAD_DIM ** -0.5
NEG_INF = -1e30
D_RNN = 1280
RNN_HEADS = 16
RNN_HEAD_DIM = D_RNN // RNN_HEADS
CONV_W = 4
LRU_C = 8.0
N_EXPERTS = 32
TOP_K = 4
D_FF = 1024
SWIGLU_ALPHA = 1.702
SWIGLU_LIMIT = 7.0
LN_EPS = 1e-5
DN_ALPHA = (2.0 * DEPTH) ** 0.25
DN_BETA = (8.0 * DEPTH) ** -0.25

Q_DIM = N_HEADS * HEAD_DIM
KV_DIM = N_KV_HEADS * HEAD_DIM
SPLIT_SIZES = (Q_DIM, KV_DIM, KV_DIM, D_RNN, D_RNN, D_MODEL, D_MODEL)
SPLIT_OFFSETS = tuple(int(o) for o in np.cumsum(SPLIT_SIZES)[:-1])
IN_DIM = int(sum(SPLIT_SIZES))

kernel_name = "hybrid_swa_sink_rglru_moe_streaming_step"


def _layer_norm(x, g, b):
    xf = x.astype(jnp.float32)
    mu = jnp.mean(xf, axis=-1, keepdims=True)
    var = jnp.mean(jnp.square(xf - mu), axis=-1, keepdims=True)
    y = (xf - mu) * lax.rsqrt(var + LN_EPS)
    return (y * g.astype(jnp.float32) + b.astype(jnp.float32)).astype(x.dtype)


def _partial_rope(x, pos):
    half = ROT_DIM // 2
    inv_freq = ROPE_THETA ** (-jnp.arange(half, dtype=jnp.float32) / half)
    ang = pos.astype(jnp.float32)[:, None] * inv_freq[None, :]
    cos = jnp.cos(ang)[:, None, :]
    sin = jnp.sin(ang)[:, None, :]
    xr = x[..., :ROT_DIM].astype(jnp.float32)
    x1, x2 = xr[..., :half], xr[..., half:]
    rot = jnp.concatenate([x1 * cos - x2 * sin, x2 * cos + x1 * sin], axis=-1)
    return jnp.concatenate([rot.astype(x.dtype), x[..., ROT_DIM:]], axis=-1)


def _sink_attention(q, k, v, sinks, valid=None):
    s = jnp.einsum("bnqhgd,bnkhd->bnhgqk", q, k).astype(jnp.float32) * ATTN_SCALE
    if valid is not None:
        s = jnp.where(valid, s, NEG_INF)
    sink = sinks.astype(jnp.float32).reshape(1, 1, N_KV_HEADS, GROUP, 1, 1)
    m = jnp.maximum(jnp.max(s, axis=-1, keepdims=True), sink)
    p = jnp.exp(s - m)
    denom = jnp.sum(p, axis=-1, keepdims=True) + jnp.exp(sink - m)
    w = (p / denom).astype(v.dtype)
    return jnp.einsum("bnhgqk,bnkhd->bnqhgd", w, v)


def _window_attention_prompt(q, k, v, sinks):
    B, T = q.shape[:2]
    n_c = T // CHUNK
    qb = q.reshape(B, n_c, CHUNK, N_KV_HEADS, GROUP, HEAD_DIM)
    pad = ((0, 0), (WIN_CHUNKS, 0), (0, 0), (0, 0), (0, 0))
    kp = jnp.pad(k.reshape(B, n_c, CHUNK, N_KV_HEADS, HEAD_DIM), pad)
    vp = jnp.pad(v.reshape(B, n_c, CHUNK, N_KV_HEADS, HEAD_DIM), pad)
    kwin = jnp.concatenate([kp[:, j:j + n_c] for j in range(WIN_CHUNKS + 1)], axis=2)
    vwin = jnp.concatenate([vp[:, j:j + n_c] for j in range(WIN_CHUNKS + 1)], axis=2)
    key_chunk = jnp.arange(n_c)[:, None] - WIN_CHUNKS + jnp.arange(WIN_CHUNKS + 1)[None, :]
    valid = jnp.repeat(key_chunk >= 0, CHUNK, axis=1)[None, :, None, None, None, :]
    o = _sink_attention(qb, kwin, vwin, sinks, valid)
    return o.reshape(B, T, Q_DIM)


def _window_attention_sample(q, k_all, v_all, sinks):
    B, T = q.shape[:2]
    qb = q.reshape(B, 1, T, N_KV_HEADS, GROUP, HEAD_DIM)
    o = _sink_attention(qb, k_all[:, None], v_all[:, None], sinks)
    return o.reshape(B, T, Q_DIM)


def _causal_conv(xp, t_len, conv_w, conv_b):
    y = xp[:, 0:t_len] * conv_w[0]
    for j in range(1, CONV_W):
        y = y + xp[:, j:j + t_len] * conv_w[j]
    return y + conv_b


def _rglru(xc, h0, gate_a_w, gate_a_b, gate_x_w, gate_x_b, lru_lambda):
    B, T, _ = xc.shape
    xh = xc.reshape(B, T, RNN_HEADS, RNN_HEAD_DIM)
    r = jax.nn.sigmoid((jnp.einsum("bthi,hij->bthj", xh, gate_a_w).reshape(B, T, D_RNN) + gate_a_b).astype(jnp.float32))
    i = jax.nn.sigmoid((jnp.einsum("bthi,hij->bthj", xh, gate_x_w).reshape(B, T, D_RNN) + gate_x_b).astype(jnp.float32))
    log_a = -LRU_C * r * jax.nn.softplus(-lru_lambda.astype(jnp.float32))
    a = jnp.exp(log_a)
    mult = jnp.sqrt(-jnp.expm1(2.0 * log_a))
    b = mult * i * xc.astype(jnp.float32)
    b = b.at[:, 0].add(a[:, 0] * h0.astype(jnp.float32))

    def combine(left, right):
        a1, b1 = left
        a2, b2 = right
        return a1 * a2, a2 * b1 + b2

    _, h = lax.associative_scan(combine, (a, b), axis=1)
    return h


def _moe(x, router_w, router_b, w_gate_up, b_gate_up, w_down, b_down):
    shp = x.shape
    t = x.reshape(-1, D_MODEL)
    logits = (t @ router_w + router_b).astype(jnp.float32)
    top_v, top_i = lax.top_k(logits, TOP_K)
    gates = jax.nn.softmax(top_v, axis=-1)
    comb = jnp.sum(jax.nn.one_hot(top_i, N_EXPERTS, dtype=jnp.float32) * gates[..., None], axis=1)
    out = jnp.zeros((t.shape[0], D_MODEL), jnp.float32)
    for e in range(N_EXPERTS):
        hgu = t @ w_gate_up[e] + b_gate_up[e]
        glu = jnp.minimum(hgu[:, :D_FF], SWIGLU_LIMIT)
        lin = jnp.clip(hgu[:, D_FF:], -SWIGLU_LIMIT, SWIGLU_LIMIT)
        hh = glu * jax.nn.sigmoid(SWIGLU_ALPHA * glu) * (lin + 1.0)
        out = out + comb[:, e:e + 1] * (hh @ w_down[e] + b_down[e])
    return out.astype(x.dtype).reshape(shp)


def _trunk_layer(x, pos, past, p):
    B, T, _ = x.shape
    z = x @ p["w_in"]
    q, k, v, xr, yr, ga, gr = jnp.split(z, SPLIT_OFFSETS, axis=-1)
    q = _partial_rope(q.reshape(B, T, N_HEADS, HEAD_DIM), pos)
    k = _partial_rope(k.reshape(B, T, N_KV_HEADS, HEAD_DIM), pos)
    v = v.reshape(B, T, N_KV_HEADS, HEAD_DIM)
    if past is None:
        o_att = _window_attention_prompt(q, k, v, p["sinks"])
        keep = min(WINDOW, T)
        new_k, new_v = k[:, T - keep:], v[:, T - keep:]
        conv_buf = jnp.zeros((B, CONV_W - 1, D_RNN), xr.dtype)
        h0 = jnp.zeros((B, D_RNN), jnp.float32)
    else:
        past_k, past_v, conv_buf, h0 = past
        keep = past_k.shape[1]
        k_all = jnp.concatenate([past_k.astype(k.dtype), k], axis=1)
        v_all = jnp.concatenate([past_v.astype(v.dtype), v], axis=1)
        o_att = _window_attention_sample(q, k_all, v_all, p["sinks"])
        new_k = k_all[:, k_all.shape[1] - keep:]
        new_v = v_all[:, v_all.shape[1] - keep:]
    xp = jnp.concatenate([conv_buf.astype(xr.dtype), xr], axis=1)
    new_conv = xp[:, xp.shape[1] - (CONV_W - 1):]
    xc = _causal_conv(xp, T, p["conv_w"], p["conv_b"])
    h = _rglru(xc, h0, p["gate_a_w"], p["gate_a_b"], p["gate_x_w"], p["gate_x_b"], p["lru_lambda"])
    new_h = h[:, -1]
    rnn = h.astype(x.dtype) * jax.nn.gelu(yr)
    merged = jax.nn.sigmoid(ga) * (o_att @ p["w_attn_out"]) + jax.nn.sigmoid(gr) * (rnn @ p["w_rnn_out"])
    mix = merged @ p["w_out"]
    x1 = _layer_norm(DN_ALPHA * x + mix, p["ln1_g"], p["ln1_b"])
    ffn = _moe(x1, p["router_w"], p["router_b"], p["w_gate_up"], p["b_gate_up"], p["w_down"], p["b_down"])
    x2 = _layer_norm(DN_ALPHA * x1 + ffn, p["ln2_g"], p["ln2_b"])
    return x2, (new_k, new_v, new_conv, new_h)


def setup_inputs(seed: int = 0) -> dict:
    key = jax.random.key(seed)
    ks = jax.random.split(key, 32)
    f32 = jnp.float32
    win_len = min(WINDOW, PAST_LEN)
    nrm = lambda k, shape, s: jax.random.normal(k, shape, f32) * s
    col_scale = np.ones((IN_DIM,), np.float32)
    col_scale[SPLIT_OFFSETS[1]:SPLIT_OFFSETS[2]] = DN_BETA
    u = jax.random.uniform(ks[10], (DEPTH, D_RNN), f32, 0.9, 0.999)
    base = u ** (1.0 / LRU_C)
    return {
        "x_prompt": nrm(ks[0], (BATCH, SEQ, D_MODEL), 1.0),
        "x_sample": nrm(ks[1], (DEC_BATCH, DEC_SEQ, D_MODEL), 1.0),
        "cache_k": nrm(ks[2], (DEPTH, DEC_BATCH, win_len, N_KV_HEADS, HEAD_DIM), 1.0),
        "cache_v": nrm(ks[3], (DEPTH, DEC_BATCH, win_len, N_KV_HEADS, HEAD_DIM), DN_BETA),
        "state_conv": nrm(ks[4], (DEPTH, DEC_BATCH, CONV_W - 1, D_RNN), 1.0),
        "state_h": nrm(ks[5], (DEPTH, DEC_BATCH, D_RNN), 0.5),
        "w_in": nrm(ks[6], (DEPTH, D_MODEL, IN_DIM), D_MODEL ** -0.5) * jnp.asarray(col_scale),
        "attn_sinks": nrm(ks[7], (DEPTH, N_HEADS), 0.5),
        "w_attn_out": nrm(ks[8], (DEPTH, Q_DIM, D_MODEL), Q_DIM ** -0.5 * DN_BETA),
        "conv_w": nrm(ks[9], (DEPTH, CONV_W, D_RNN), CONV_W ** -0.5),
        "conv_b": nrm(ks[11], (DEPTH, D_RNN), 0.01),
        "gate_a_w": nrm(ks[12], (DEPTH, RNN_HEADS, RNN_HEAD_DIM, RNN_HEAD_DIM), RNN_HEAD_DIM ** -0.5),
        "gate_a_b": nrm(ks[13], (DEPTH, D_RNN), 0.01),
        "gate_x_w": nrm(ks[14], (DEPTH, RNN_HEADS, RNN_HEAD_DIM, RNN_HEAD_DIM), RNN_HEAD_DIM ** -0.5),
        "gate_x_b": nrm(ks[15], (DEPTH, D_RNN), 0.01),
        "lru_lambda": jnp.log(base / (1.0 - base)),
        "w_rnn_out": nrm(ks[16], (DEPTH, D_RNN, D_MODEL), D_RNN ** -0.5 * DN_BETA),
        "w_out": nrm(ks[17], (DEPTH, D_MODEL, D_MODEL), D_MODEL ** -0.5 * DN_BETA),
        "ln1_g": 1.0 + nrm(ks[18], (DEPTH, D_MODEL), 0.02),
        "ln1_b": nrm(ks[19], (DEPTH, D_MODEL), 0.02),
        "router_w": nrm(ks[20], (DEPTH, D_MODEL, N_EXPERTS), D_MODEL ** -0.5),
        "router_b": nrm(ks[21], (DEPTH, N_EXPERTS), 0.01),
        "w_gate_up": nrm(ks[22], (DEPTH, N_EXPERTS, D_MODEL, 2 * D_FF), D_MODEL ** -0.5),
        "b_gate_up": nrm(ks[23], (DEPTH, N_EXPERTS, 2 * D_FF), 0.01),
        "w_down": nrm(ks[24], (DEPTH, N_EXPERTS, D_FF, D_MODEL), D_FF ** -0.5 * DN_BETA),
        "b_down": nrm(ks[25], (DEPTH, N_EXPERTS, D_MODEL), 0.01),
        "ln2_g": 1.0 + nrm(ks[26], (DEPTH, D_MODEL), 0.02),
        "ln2_b": nrm(ks[27], (DEPTH, D_MODEL), 0.02),
    }


def reference(x_prompt, x_sample, cache_k, cache_v, state_conv, state_h, w_in, attn_sinks, w_attn_out,
              conv_w, conv_b, gate_a_w, gate_a_b, gate_x_w, gate_x_b, lru_lambda, w_rnn_out, w_out,
              ln1_g, ln1_b, router_w, router_b, w_gate_up, b_gate_up, w_down, b_down, ln2_g, ln2_b):
    pos_prompt = jnp.arange(x_prompt.shape[1], dtype=jnp.int32)
    pos_sample = PAST_LEN + jnp.arange(x_sample.shape[1], dtype=jnp.int32)
    yp, ys = x_prompt, x_sample
    pk, pv, pc, ph, sk, sv, sc, sh = [], [], [], [], [], [], [], []
    for l in range(DEPTH):
        p = {"w_in": w_in[l], "sinks": attn_sinks[l], "w_attn_out": w_attn_out[l],
             "conv_w": conv_w[l], "conv_b": conv_b[l], "gate_a_w": gate_a_w[l], "gate_a_b": gate_a_b[l],
             "gate_x_w": gate_x_w[l], "gate_x_b": gate_x_b[l], "lru_lambda": lru_lambda[l],
             "w_rnn_out": w_rnn_out[l], "w_out": w_out[l], "ln1_g": ln1_g[l], "ln1_b": ln1_b[l],
             "router_w": router_w[l], "router_b": router_b[l], "w_gate_up": w_gate_up[l],
             "b_gate_up": b_gate_up[l], "w_down": w_down[l], "b_down": b_down[l],
             "ln2_g": ln2_g[l], "ln2_b": ln2_b[l]}
        yp, (k1, v1, c1, h1) = _trunk_layer(yp, pos_prompt, None, p)
        ys, (k2, v2, c2, h2) = _trunk_layer(ys, pos_sample, (cache_k[l], cache_v[l], state_conv[l], state_h[l]), p)
        pk.append(k1); pv.append(v1); pc.append(c1); ph.append(h1)
        sk.append(k2); sv.append(v2); sc.append(c2); sh.append(h2)
    prompt_k = jnp.stack(pk)
    prompt_v = jnp.stack(pv)
    prompt_conv = jnp.stack(pc)
    prompt_h = jnp.stack(ph)
    sample_k = jnp.stack(sk)
    sample_v = jnp.stack(sv)
    sample_conv = jnp.stack(sc)
    sample_h = jnp.stack(sh)
    return (yp, ys, prompt_k, prompt_v, prompt_conv, prompt_h, sample_k, sample_v, sample_conv, sample_h)
```

```python
import functools

import jax
import jax.numpy as jnp
import numpy as np
from jax import lax
from jax.experimental import pallas as pl
from jax.experimental.pallas import tpu as pltpu

D_MODEL = 1024
PAST_LEN = 1024
CHUNK = 64
N_HEADS = 16
N_KV_HEADS = 4
HEAD_DIM = 64
GROUP = N_HEADS // N_KV_HEADS
ROT_DIM = HEAD_DIM // 4
ROPE_THETA = 500000.0
WINDOW = 128
ATTN_SCALE = HEAD_DIM ** -0.5
NEG_INF = -1e30
D_RNN = 1280
RNN_HEADS = 16
RNN_HEAD_DIM = D_RNN // RNN_HEADS
CONV_W = 4
LRU_C = 8.0
N_EXPERTS = 32
TOP_K = 4
D_FF = 1024
SWIGLU_ALPHA = 1.702
SWIGLU_LIMIT = 7.0
LN_EPS = 1e-5
DEPTH = 1
DN_ALPHA = (2.0 * DEPTH) ** 0.25

Q_DIM = N_HEADS * HEAD_DIM
KV_DIM = N_KV_HEADS * HEAD_DIM
OFF_K = Q_DIM
OFF_V = OFF_K + KV_DIM
OFF_XR = OFF_V + KV_DIM
OFF_YR = OFF_XR + D_RNN
OFF_GA = OFF_YR + D_RNN
OFF_GR = OFF_GA + D_MODEL
IN_DIM = OFF_GR + D_MODEL

LANES = 128
CONV_HDR = 8
VMEM_LIMIT_BYTES = 56 * 1024 * 1024

_BF16 = jnp.bfloat16
_F32 = jnp.float32


def _dot(a, b):
    return jnp.dot(a, b, preferred_element_type=_F32)


def _dot_nt(a, b):
    return lax.dot_general(a, b, (((1,), (1,)), ((), ())), preferred_element_type=_F32)


def _layer_norm(x, g, b):
    mu = jnp.mean(x, axis=-1, keepdims=True)
    xc = x - mu
    var = jnp.mean(xc * xc, axis=-1, keepdims=True)
    return xc * lax.rsqrt(var + LN_EPS) * g + b


def _rope(x, cos_t, sin_up, sin_dn):
    return x * cos_t + pltpu.roll(x, LANES - ROT_DIM // 2, 1) * sin_up + pltpu.roll(x, ROT_DIM // 2, 1) * sin_dn


def _lo_hi(slab, h):
    blk = slab[:, (h // 2) * LANES:(h // 2 + 1) * LANES]
    lane = lax.broadcasted_iota(jnp.int32, blk.shape, 1)
    if h % 2 == 0:
        lo = jnp.where(lane < HEAD_DIM, blk, 0.0)
        hi = pltpu.roll(lo, HEAD_DIM, 1)
    else:
        hi = jnp.where(lane >= HEAD_DIM, blk, 0.0)
        lo = pltpu.roll(hi, HEAD_DIM, 1)
    return lo.astype(_BF16), hi.astype(_BF16)


def _mixer_kernel(has_past, nb, lt, lq,
                  x_ref, rope_ref, w_in_ref, k0_ref, v0_ref, c0_ref, h0_ref, sinks_ref,
                  convw_ref, convb_ref, wg_ref, bg_ref, lam_ref, wao_ref, wro_ref, wout_ref,
                  g1_ref, b1_ref, rw_ref, rb_ref,
                  x1_ref, comb_ref, kn_ref, vn_ref, cn_ref, hn_ref,
                  kw_ref, vw_ref, xp_ref, xc_ref, a_ref, b_ref, hc_ref, oat_ref):
    s_idx = pl.program_id(1)
    n_s = pl.num_programs(1)
    rows = nb * lt
    n_chunks = lt // lq

    @pl.when(s_idx == 0)
    def _init():
        for bi in range(nb):
            for h in range(N_KV_HEADS):
                klo, khi = _lo_hi(k0_ref[bi], h)
                vlo, vhi = _lo_hi(v0_ref[bi], h)
                kw_ref[bi, h, 0, 0:WINDOW, :] = klo
                kw_ref[bi, h, 1, 0:WINDOW, :] = khi
                vw_ref[bi, h, 0, 0:WINDOW, :] = vlo
                vw_ref[bi, h, 1, 0:WINDOW, :] = vhi
            xp_ref[bi, CONV_HDR - (CONV_W - 1):CONV_HDR, :] = c0_ref[bi]
            hc_ref[bi] = h0_ref[bi]

    x = x_ref[...].reshape(rows, D_MODEL)
    xb = x.astype(_BF16)

    cos_t = jnp.concatenate([rope_ref[0]] * nb, axis=0) if nb > 1 else rope_ref[0]
    sin_up = jnp.concatenate([rope_ref[1]] * nb, axis=0) if nb > 1 else rope_ref[1]
    sin_dn = jnp.concatenate([rope_ref[2]] * nb, axis=0) if nb > 1 else rope_ref[2]

    zq = _dot(xb, w_in_ref[:, 0:Q_DIM])
    q_blocks = []
    for c in range(Q_DIM // LANES):
        qr = _rope(zq[:, c * LANES:(c + 1) * LANES], cos_t, sin_up, sin_dn)
        q_blocks.append((qr * ATTN_SCALE).astype(_BF16))
    zk = _dot(xb, w_in_ref[:, OFF_K:OFF_V])
    k_rot = jnp.concatenate(
        [_rope(zk[:, c * LANES:(c + 1) * LANES], cos_t, sin_up, sin_dn) for c in range(KV_DIM // LANES)], axis=1)
    v_new = _dot(xb, w_in_ref[:, OFF_V:OFF_XR])

    for bi in range(nb):
        r0 = bi * lt
        for h in range(N_KV_HEADS):
            klo, khi = _lo_hi(k_rot[r0:r0 + lt], h)
            vlo, vhi = _lo_hi(v_new[r0:r0 + lt], h)
            kw_ref[bi, h, 0, WINDOW:WINDOW + lt, :] = klo
            kw_ref[bi, h, 1, WINDOW:WINDOW + lt, :] = khi
            vw_ref[bi, h, 0, WINDOW:WINDOW + lt, :] = vlo
            vw_ref[bi, h, 1, WINDOW:WINDOW + lt, :] = vhi

    for bi in range(nb):
        r0 = bi * lt
        if lt >= WINDOW:
            kn_ref[bi] = k_rot[r0 + lt - WINDOW:r0 + lt]
            vn_ref[bi] = v_new[r0 + lt - WINDOW:r0 + lt]
        else:
            kn_ref[bi, 0:WINDOW - lt, :] = k0_ref[bi, lt:WINDOW, :]
            kn_ref[bi, WINDOW - lt:WINDOW, :] = k_rot[r0:r0 + lt]
            vn_ref[bi, 0:WINDOW - lt, :] = v0_ref[bi, lt:WINDOW, :]
            vn_ref[bi, WINDOW - lt:WINDOW, :] = v_new[r0:r0 + lt]

    w2 = 2 * lq
    lane2 = lax.broadcasted_iota(jnp.int32, (lq, w2), 1)
    is_a2 = lane2 < lq
    lane_p = lax.broadcasted_iota(jnp.int32, (lq, LANES), 1)
    first = s_idx == 0
    for bi in range(nb):
        for j in range(n_chunks):
            p0 = j * lq
            o0 = WINDOW + j * lq
            q0 = bi * lt + j * lq
            for h in range(N_KV_HEADS):
                kmat = jnp.concatenate([kw_ref[bi, h, 0, p0:p0 + WINDOW, :], kw_ref[bi, h, 1, p0:p0 + WINDOW, :],
                                        kw_ref[bi, h, 0, o0:o0 + lq, :], kw_ref[bi, h, 1, o0:o0 + lq, :]], axis=0)
                vmat = jnp.concatenate([vw_ref[bi, h, 0, p0:p0 + WINDOW, :], vw_ref[bi, h, 1, p0:p0 + WINDOW, :],
                                        vw_ref[bi, h, 0, o0:o0 + lq, :], vw_ref[bi, h, 1, o0:o0 + lq, :]], axis=0)
                qs = jnp.concatenate([q_blocks[2 * h][q0:q0 + lq], q_blocks[2 * h + 1][q0:q0 + lq]], axis=0)
                sc = _dot_nt(qs, kmat)
                probs = []
                scales = []
                for p in range(2):
                    sp = sc[p * lq:(p + 1) * lq]
                    c0 = sp[:, 0:LANES]
                    c1 = sp[:, LANES:2 * LANES]
                    c2 = sp[:, 2 * LANES:2 * LANES + w2]
                    if not has_past and j * lq < WINDOW:
                        n_bad = WINDOW - j * lq
                        bad = jnp.logical_and(first, lane_p < n_bad)
                        c0 = jnp.where(bad, NEG_INF, c0)
                        c1 = jnp.where(bad, NEG_INF, c1)
                    sink_a = sinks_ref[4 * h + 2 * p]
                    sink_b = sinks_ref[4 * h + 2 * p + 1]
                    ma = jnp.maximum(jnp.max(c0, axis=1, keepdims=True), sink_a)
                    ma = jnp.maximum(ma, jnp.max(jnp.where(is_a2, c2, NEG_INF), axis=1, keepdims=True))
                    mb = jnp.maximum(jnp.max(c1, axis=1, keepdims=True), sink_b)
                    mb = jnp.maximum(mb, jnp.max(jnp.where(is_a2, NEG_INF, c2), axis=1, keepdims=True))
                    e0 = jnp.exp(c0 - ma)
                    e1 = jnp.exp(c1 - mb)
                    e2 = jnp.exp(c2 - jnp.where(is_a2, ma, mb))
                    da = (jnp.sum(e0, axis=1, keepdims=True) + jnp.sum(jnp.where(is_a2, e2, 0.0), axis=1, keepdims=True)
                          + jnp.exp(sink_a - ma))
                    db = (jnp.sum(e1, axis=1, keepdims=True) + jnp.sum(jnp.where(is_a2, 0.0, e2), axis=1, keepdims=True)
                          + jnp.exp(sink_b - mb))
                    probs.append(jnp.concatenate([e0, e1, e2], axis=1).astype(_BF16))
                    scales.append(jnp.where(lane_p < HEAD_DIM, 1.0 / da, 1.0 / db))
                ov = _dot(jnp.concatenate(probs, axis=0), vmat)
                for p in range(2):
                    oat_ref[q0:q0 + lq, (2 * h + p) * LANES:(2 * h + p + 1) * LANES] = (
                        ov[p * lq:(p + 1) * lq] * scales[p]).astype(_BF16)

    if lt >= WINDOW:
        @pl.when(s_idx + 1 < n_s)
        def _carry_kv():
            for bi in range(nb):
                for h in range(N_KV_HEADS):
                    for v in range(2):
                        kw_ref[bi, h, v, 0:WINDOW, :] = kw_ref[bi, h, v, lt:lt + WINDOW, :]
                        vw_ref[bi, h, v, 0:WINDOW, :] = vw_ref[bi, h, v, lt:lt + WINDOW, :]

    xr = _dot(xb, w_in_ref[:, OFF_XR:OFF_YR])
    for bi in range(nb):
        r0 = bi * lt
        xp_ref[bi, CONV_HDR:CONV_HDR + lt, :] = xr[r0:r0 + lt]
        acc = xp_ref[bi, CONV_HDR - 3:CONV_HDR - 3 + lt, :] * convw_ref[0:1, :]
        for t in range(1, CONV_W):
            acc = acc + xp_ref[bi, CONV_HDR - 3 + t:CONV_HDR - 3 + t + lt, :] * convw_ref[t:t + 1, :]
        xc_ref[r0:r0 + lt, :] = acc + convb_ref[...]
        tail = xp_ref[bi, CONV_HDR + lt - (CONV_W - 1):CONV_HDR + lt, :]
        cn_ref[bi] = tail
        xp_ref[bi, CONV_HDR - (CONV_W - 1):CONV_HDR, :] = tail

    xc = xc_ref[...]
    gates = _dot(xc.astype(_BF16), wg_ref[...]) + bg_ref[...]
    r_gate = jax.nn.sigmoid(gates[:, 0:D_RNN])
    i_gate = jax.nn.sigmoid(gates[:, D_RNN:2 * D_RNN])
    lam = lam_ref[...]
    softplus_neg = jnp.maximum(-lam, 0.0) + jnp.log(1.0 + jnp.exp(-jnp.abs(lam)))
    log_a = (-LRU_C * r_gate) * softplus_neg
    a_ref[...] = jnp.exp(log_a)
    th = jnp.tanh(log_a)
    b_ref[...] = jnp.sqrt(-2.0 * th / (1.0 - th)) * i_gate * xc

    for bi in range(nb):
        r0 = bi * lt

        def scan_step(t, hprev, r0=r0):
            hcur = a_ref[pl.ds(r0 + t, 1), :] * hprev + b_ref[pl.ds(r0 + t, 1), :]
            b_ref[pl.ds(r0 + t, 1), :] = hcur
            return hcur

        h_last = lax.fori_loop(0, lt, scan_step, hc_ref[bi], unroll=8)
        hc_ref[bi] = h_last
        hn_ref[bi] = h_last

    yr = _dot(xb, w_in_ref[:, OFF_YR:OFF_GA])
    rnn = b_ref[...] * jax.nn.gelu(yr)

    ga = _dot(xb, w_in_ref[:, OFF_GA:OFF_GR])
    gr = _dot(xb, w_in_ref[:, OFF_GR:IN_DIM])
    merged = (jax.nn.sigmoid(ga) * _dot(oat_ref[...], wao_ref[...])
              + jax.nn.sigmoid(gr) * _dot(rnn.astype(_BF16), wro_ref[...]))
    mix = _dot(merged.astype(_BF16), wout_ref[...])
    x1 = _layer_norm(DN_ALPHA * x + mix, g1_ref[...], b1_ref[...])
    x1_ref[...] = x1.reshape(nb, lt, D_MODEL)

    logits = _dot(x1.astype(_BF16), rw_ref[...]) + rb_ref[...]
    e_iota = lax.broadcasted_iota(jnp.int32, logits.shape, 1)
    work = logits
    top_vals = []
    top_sel = []
    for _ in range(TOP_K):
        m = jnp.max(work, axis=1, keepdims=True)
        idx = jnp.min(jnp.where(work == m, e_iota, N_EXPERTS), axis=1, keepdims=True)
        sel = e_iota == idx
        top_vals.append(m)
        top_sel.append(sel)
        work = jnp.where(sel, -jnp.inf, work)
    exps = [jnp.exp(v - top_vals[0]) for v in top_vals]
    denom = exps[0]
    for e in exps[1:]:
        denom = denom + e
    comb = jnp.zeros_like(logits)
    for sel, e in zip(top_sel, exps):
        comb = comb + jnp.where(sel, e / denom, 0.0)
    comb_ref[...] = comb.reshape(nb, lt, N_EXPERTS)


def _const_spec(shape):
    nd = len(shape)
    return pl.BlockSpec(shape, lambda b, s, _nd=nd: (0,) * _nd, pipeline_mode=pl.Buffered(1))


def _mixer(x, rope, k0, v0, c0, h0, sinks, prm, *, has_past, nb, lt, lq):
    bsz, seq, _ = x.shape
    assert bsz % nb == 0 and seq % lt == 0 and lt % lq == 0 and lq % 32 == 0
    assert lt >= WINDOW or seq == lt
    n_s = seq // lt
    rows = nb * lt
    kernel = functools.partial(_mixer_kernel, has_past, nb, lt, lq)
    batch_spec = lambda shape: pl.BlockSpec((nb,) + shape, lambda b, s: (b,) + (0,) * len(shape))
    in_specs = [
        pl.BlockSpec((nb, lt, D_MODEL), lambda b, s: (b, s, 0)),
        pl.BlockSpec((3, lt, LANES), lambda b, s: (0, s, 0)),
        _const_spec((D_MODEL, IN_DIM)),
        batch_spec((WINDOW, KV_DIM)), batch_spec((WINDOW, KV_DIM)),
        batch_spec((CONV_W - 1, D_RNN)), batch_spec((1, D_RNN)),
        pl.BlockSpec(memory_space=pltpu.SMEM),
        _const_spec((CONV_W, D_RNN)), _const_spec((1, D_RNN)),
        _const_spec((D_RNN, 2 * D_RNN)), _const_spec((1, 2 * D_RNN)), _const_spec((1, D_RNN)),
        _const_spec((Q_DIM, D_MODEL)), _const_spec((D_RNN, D_MODEL)), _const_spec((D_MODEL, D_MODEL)),
        _const_spec((1, D_MODEL)), _const_spec((1, D_MODEL)),
        _const_spec((D_MODEL, N_EXPERTS)), _const_spec((1, N_EXPERTS)),
    ]
    out_shape = (
        jax.ShapeDtypeStruct((bsz, seq, D_MODEL), _F32),
        jax.ShapeDtypeStruct((bsz, seq, N_EXPERTS), _F32),
        jax.ShapeDtypeStruct((bsz, WINDOW, KV_DIM), _F32),
        jax.ShapeDtypeStruct((bsz, WINDOW, KV_DIM), _F32),
        jax.ShapeDtypeStruct((bsz, CONV_W - 1, D_RNN), _F32),
        jax.ShapeDtypeStruct((bsz, 1, D_RNN), _F32),
    )
    out_specs = (
        pl.BlockSpec((nb, lt, D_MODEL), lambda b, s: (b, s, 0)),
        pl.BlockSpec((nb, lt, N_EXPERTS), lambda b, s: (b, s, 0)),
        batch_spec((WINDOW, KV_DIM)), batch_spec((WINDOW, KV_DIM)),
        batch_spec((CONV_W - 1, D_RNN)), batch_spec((1, D_RNN)),
    )
    scratch = [
        pltpu.VMEM((nb, N_KV_HEADS, 2, WINDOW + lt, LANES), _BF16),
        pltpu.VMEM((nb, N_KV_HEADS, 2, WINDOW + lt, LANES), _BF16),
        pltpu.VMEM((nb, CONV_HDR + lt, D_RNN), _F32),
        pltpu.VMEM((rows, D_RNN), _F32),
        pltpu.VMEM((rows, D_RNN), _F32),
        pltpu.VMEM((rows, D_RNN), _F32),
        pltpu.VMEM((nb, 1, D_RNN), _F32),
        pltpu.VMEM((rows, Q_DIM), _BF16),
    ]
    return pl.pallas_call(
        kernel,
        grid=(bsz // nb, n_s),
        in_specs=in_specs,
        out_specs=out_specs,
        out_shape=out_shape,
        scratch_shapes=scratch,
        compiler_params=pltpu.CompilerParams(
            dimension_semantics=("arbitrary", "arbitrary"), vmem_limit_bytes=VMEM_LIMIT_BYTES),
        name="mixer_past" if has_past else "mixer_prompt",
    )(x, rope, prm["w_in"], k0, v0, c0, h0, sinks, prm["conv_w"], prm["conv_b"], prm["wg"], prm["bg"],
      prm["lam"], prm["wao"], prm["wro"], prm["wout"], prm["g1"], prm["b1"], prm["rw"], prm["rb"])


def _moe_kernel(x1_ref, comb_ref, wgu_ref, bgu_ref, wd_ref, bd_ref, g2_ref, b2_ref, y_ref, acc_ref, xb_ref):
    e = pl.program_id(1)

    @pl.when(e == 0)
    def _start():
        acc_ref[...] = jnp.zeros_like(acc_ref)
        xb_ref[...] = x1_ref[...].astype(_BF16)

    hgu = _dot(xb_ref[...], wgu_ref[0]) + bgu_ref[0]
    glu = jnp.minimum(hgu[:, 0:D_FF], SWIGLU_LIMIT)
    lin = jnp.clip(hgu[:, D_FF:2 * D_FF], -SWIGLU_LIMIT, SWIGLU_LIMIT)
    hh = glu * jax.nn.sigmoid(SWIGLU_ALPHA * glu) * (lin + 1.0)
    y = _dot(hh.astype(_BF16), wd_ref[0]) + bd_ref[0]
    comb = comb_ref[...]
    lane = lax.broadcasted_iota(jnp.int32, comb.shape, 1)
    w_e = jnp.sum(jnp.where(lane == e, comb, 0.0), axis=1, keepdims=True)
    acc_ref[...] += w_e * y

    @pl.when(e == pl.num_programs(1) - 1)
    def _finish():
        y_ref[...] = _layer_norm(DN_ALPHA * x1_ref[...] + acc_ref[...], g2_ref[...], b2_ref[...])


def _moe(x1, comb, prm, *, tb):
    n = x1.shape[0]
    assert n % tb == 0
    return pl.pallas_call(
        _moe_kernel,
        grid=(n // tb, N_EXPERTS),
        in_specs=[
            pl.BlockSpec((tb, D_MODEL), lambda i, e: (i, 0)),
            pl.BlockSpec((tb, N_EXPERTS), lambda i, e: (i, 0)),
            pl.BlockSpec((1, D_MODEL, 2 * D_FF), lambda i, e: (e, 0, 0)),
            pl.BlockSpec((1, 1, 2 * D_FF), lambda i, e: (e, 0, 0)),
            pl.BlockSpec((1, D_FF, D_MODEL), lambda i, e: (e, 0, 0)),
            pl.BlockSpec((1, 1, D_MODEL), lambda i, e: (e, 0, 0)),
            pl.BlockSpec((1, D_MODEL), lambda i, e: (0, 0)),
            pl.BlockSpec((1, D_MODEL), lambda i, e: (0, 0)),
        ],
        out_specs=pl.BlockSpec((tb, D_MODEL), lambda i, e: (i, 0)),
        out_shape=jax.ShapeDtypeStruct((n, D_MODEL), _F32),
        scratch_shapes=[pltpu.VMEM((tb, D_MODEL), _F32), pltpu.VMEM((tb, D_MODEL), _BF16)],
        compiler_params=pltpu.CompilerParams(
            dimension_semantics=("arbitrary", "arbitrary"), vmem_limit_bytes=VMEM_LIMIT_BYTES),
        name="moe",
    )(x1, comb, prm["wgu"], prm["bgu"], prm["wd"], prm["bd"], prm["g2"], prm["b2"])


def _rope_tables(pos):
    half = ROT_DIM // 2
    inv_freq = ROPE_THETA ** (-jnp.arange(half, dtype=_F32) / half)
    ang = pos.astype(_F32)[:, None] * inv_freq[None, :]
    cos, sin = jnp.cos(ang), jnp.sin(ang)
    n = pos.shape[0]
    ones = jnp.ones((n, HEAD_DIM - ROT_DIM), _F32)
    zeros = jnp.zeros((n, HEAD_DIM - ROT_DIM), _F32)
    zh = jnp.zeros((n, half), _F32)
    cos_t = jnp.concatenate([cos, cos, ones], axis=1)
    sin_up = jnp.concatenate([-sin, zh, zeros], axis=1)
    sin_dn = jnp.concatenate([zh, sin, zeros], axis=1)
    tab = jnp.stack([cos_t, sin_up, sin_dn])
    return jnp.concatenate([tab, tab], axis=2)


def _block_diag(w):
    hd, d, _ = w.shape
    eye = jnp.eye(hd, dtype=w.dtype)
    return (eye[:, None, :, None] * w[:, :, None, :]).reshape(hd * d, hd * d)


def kernel(x_prompt, x_sample, cache_k, cache_v, state_conv, state_h, w_in, attn_sinks, w_attn_out, conv_w, conv_b, gate_a_w, gate_a_b, gate_x_w, gate_x_b, lru_lambda, w_rnn_out, w_out, ln1_g, ln1_b, router_w, router_b, w_gate_up, b_gate_up, w_down, b_down, ln2_g, ln2_b):
    assert w_in.shape[0] == DEPTH == 1
    l = 0
    bsz, seq, _ = x_prompt.shape
    dbsz, dseq, _ = x_sample.shape
    row = lambda a: a.reshape(1, -1)
    prm = {
        "w_in": w_in[l].astype(_BF16),
        "conv_w": conv_w[l], "conv_b": row(conv_b[l]),
        "wg": jnp.concatenate([_block_diag(gate_a_w[l]), _block_diag(gate_x_w[l])], axis=1).astype(_BF16),
        "bg": row(jnp.concatenate([gate_a_b[l], gate_x_b[l]])),
        "lam": row(lru_lambda[l]),
        "wao": w_attn_out[l].astype(_BF16), "wro": w_rnn_out[l].astype(_BF16), "wout": w_out[l].astype(_BF16),
        "g1": row(ln1_g[l]), "b1": row(ln1_b[l]),
        "rw": router_w[l].astype(_BF16), "rb": row(router_b[l]),
        "wgu": w_gate_up[l].astype(_BF16), "bgu": b_gate_up[l][:, None, :],
        "wd": w_down[l].astype(_BF16), "bd": b_down[l][:, None, :],
        "g2": row(ln2_g[l]), "b2": row(ln2_b[l]),
    }
    sinks = attn_sinks[l]

    rope_p = _rope_tables(jnp.arange(seq, dtype=jnp.int32))
    rope_s = _rope_tables(PAST_LEN + jnp.arange(dseq, dtype=jnp.int32))

    zeros_kv = jnp.zeros((bsz, WINDOW, KV_DIM), _F32)
    x1p, combp, pk, pv, pc, ph = _mixer(
        x_prompt, rope_p, zeros_kv, zeros_kv, jnp.zeros((bsz, CONV_W - 1, D_RNN), _F32),
        jnp.zeros((bsz, 1, D_RNN), _F32), sinks, prm, has_past=False, nb=1, lt=min(256, seq), lq=CHUNK)
    x1s, combs, sk, sv, sc, sh = _mixer(
        x_sample, rope_s, cache_k[l].reshape(dbsz, WINDOW, KV_DIM), cache_v[l].reshape(dbsz, WINDOW, KV_DIM),
        state_conv[l], state_h[l][:, None, :], sinks, prm, has_past=True, nb=min(8, dbsz), lt=dseq, lq=dseq)

    x1 = jnp.concatenate([x1p.reshape(-1, D_MODEL), x1s.reshape(-1, D_MODEL)], axis=0)
    comb = jnp.concatenate([combp.reshape(-1, N_EXPERTS), combs.reshape(-1, N_EXPERTS)], axis=0)
    n = x1.shape[0]
    tb = 512 if n % 512 == 0 else dbsz * dseq
    y = _moe(x1, comb, prm, tb=tb)
    yp = y[:bsz * seq].reshape(bsz, seq, D_MODEL)
    ys = y[bsz * seq:].reshape(dbsz, dseq, D_MODEL)

    kv5 = lambda a: a.reshape(1, a.shape[0], WINDOW, N_KV_HEADS, HEAD_DIM)
    return (yp, ys, kv5(pk), kv5(pv), pc[None], ph.reshape(1, bsz, D_RNN),
            kv5(sk), kv5(sv), sc[None], sh.reshape(1, dbsz, D_RNN))
```

```python
import functools

import jax
import jax.numpy as jnp
import numpy as np
from jax import lax
from jax.experimental import pallas as pl
from jax.experimental.pallas import tpu as pltpu

D_MODEL = 1024
PAST_LEN = 1024
CHUNK = 64
N_HEADS = 16
N_KV_HEADS = 4
HEAD_DIM = 64
GROUP = N_HEADS // N_KV_HEADS
ROT_DIM = HEAD_DIM // 4
ROPE_THETA = 500000.0
WINDOW = 128
ATTN_SCALE = HEAD_DIM ** -0.5
NEG_INF = -1e30
D_RNN = 1280
RNN_HEADS = 16
RNN_HEAD_DIM = D_RNN // RNN_HEADS
CONV_W = 4
LRU_C = 8.0
N_EXPERTS = 32
TOP_K = 4
D_FF = 1024
SWIGLU_ALPHA = 1.702
SWIGLU_LIMIT = 7.0
LN_EPS = 1e-5
DEPTH = 1
DN_ALPHA = (2.0 * DEPTH) ** 0.25

Q_DIM = N_HEADS * HEAD_DIM
KV_DIM = N_KV_HEADS * HEAD_DIM
OFF_K = Q_DIM
OFF_V = OFF_K + KV_DIM
OFF_XR = OFF_V + KV_DIM
OFF_YR = OFF_XR + D_RNN
OFF_GA = OFF_YR + D_RNN
OFF_GR = OFF_GA + D_MODEL
IN_DIM = OFF_GR + D_MODEL

LANES = 128
SLAB = D_MODEL // LANES
MIXER_ROWS = 256
EXPERT_ROWS = 256
TOKEN_ROWS = 256
CONV_HDR = 8
VMEM_LIMIT_BYTES = 56 * 1024 * 1024

_BF16 = jnp.bfloat16
_F32 = jnp.float32


def _dot(a, b):
    return jnp.dot(a, b, preferred_element_type=_F32)


def _dot_nt(a, b):
    return lax.dot_general(a, b, (((1,), (1,)), ((), ())), preferred_element_type=_F32)


def _layer_norm(x, g, b):
    mu = jnp.mean(x, axis=-1, keepdims=True)
    xc = x - mu
    var = jnp.mean(xc * xc, axis=-1, keepdims=True)
    return xc * lax.rsqrt(var + LN_EPS) * g + b


def _rope(x, cos_t, sin_up, sin_dn):
    return x * cos_t + pltpu.roll(x, LANES - ROT_DIM // 2, 1) * sin_up + pltpu.roll(x, ROT_DIM // 2, 1) * sin_dn


def _lo_hi(slab, h):
    blk = slab[:, (h // 2) * LANES:(h // 2 + 1) * LANES]
    lane = lax.broadcasted_iota(jnp.int32, blk.shape, 1)
    if h % 2 == 0:
        lo = jnp.where(lane < HEAD_DIM, blk, 0.0)
        hi = pltpu.roll(lo, HEAD_DIM, 1)
    else:
        hi = jnp.where(lane >= HEAD_DIM, blk, 0.0)
        lo = pltpu.roll(hi, HEAD_DIM, 1)
    return lo.astype(_BF16), hi.astype(_BF16)


def _mixer_kernel(has_past, nb, lt, lq,
                  x_ref, rope_ref, w_in_ref, k0_ref, v0_ref, c0_ref, h0_ref, sinks_ref,
                  convw_ref, convb_ref, wg_ref, bg_ref, lam_ref, wao_ref, wro_ref, wout_ref,
                  g1_ref, b1_ref, rw_ref, rb_ref,
                  x1s_ref, meta_ref, gate_ref, cnt_ref, kn_ref, vn_ref, cn_ref, hn_ref,
                  kw_ref, vw_ref, xp_ref, xc_ref, a_ref, b_ref, hc_ref, oat_ref, run_ref):
    s_idx = pl.program_id(1)
    n_s = pl.num_programs(1)
    rows = nb * lt
    n_chunks = lt // lq

    @pl.when(jnp.logical_and(pl.program_id(0) == 0, s_idx == 0))
    def _init_counts():
        run_ref[...] = jnp.zeros_like(run_ref)

    @pl.when(s_idx == 0)
    def _init():
        for bi in range(nb):
            for h in range(N_KV_HEADS):
                klo, khi = _lo_hi(k0_ref[bi], h)
                vlo, vhi = _lo_hi(v0_ref[bi], h)
                kw_ref[bi, h, 0, 0:WINDOW, :] = klo
                kw_ref[bi, h, 1, 0:WINDOW, :] = khi
                vw_ref[bi, h, 0, 0:WINDOW, :] = vlo
                vw_ref[bi, h, 1, 0:WINDOW, :] = vhi
            xp_ref[bi, CONV_HDR - (CONV_W - 1):CONV_HDR, :] = c0_ref[bi]
            hc_ref[bi] = h0_ref[bi]

    x = x_ref[...].reshape(rows, D_MODEL)
    xb = x.astype(_BF16)

    cos_t = jnp.concatenate([rope_ref[0]] * nb, axis=0) if nb > 1 else rope_ref[0]
    sin_up = jnp.concatenate([rope_ref[1]] * nb, axis=0) if nb > 1 else rope_ref[1]
    sin_dn = jnp.concatenate([rope_ref[2]] * nb, axis=0) if nb > 1 else rope_ref[2]

    zq = _dot(xb, w_in_ref[:, 0:Q_DIM])
    q_blocks = []
    for c in range(Q_DIM // LANES):
        qr = _rope(zq[:, c * LANES:(c + 1) * LANES], cos_t, sin_up, sin_dn)
        q_blocks.append((qr * ATTN_SCALE).astype(_BF16))
    zk = _dot(xb, w_in_ref[:, OFF_K:OFF_V])
    k_rot = jnp.concatenate(
        [_rope(zk[:, c * LANES:(c + 1) * LANES], cos_t, sin_up, sin_dn) for c in range(KV_DIM // LANES)], axis=1)
    v_new = _dot(xb, w_in_ref[:, OFF_V:OFF_XR])

    for bi in range(nb):
        r0 = bi * lt
        for h in range(N_KV_HEADS):
            klo, khi = _lo_hi(k_rot[r0:r0 + lt], h)
            vlo, vhi = _lo_hi(v_new[r0:r0 + lt], h)
            kw_ref[bi, h, 0, WINDOW:WINDOW + lt, :] = klo
            kw_ref[bi, h, 1, WINDOW:WINDOW + lt, :] = khi
            vw_ref[bi, h, 0, WINDOW:WINDOW + lt, :] = vlo
            vw_ref[bi, h, 1, WINDOW:WINDOW + lt, :] = vhi

    for bi in range(nb):
        r0 = bi * lt
        if lt >= WINDOW:
            kn_ref[bi] = k_rot[r0 + lt - WINDOW:r0 + lt]
            vn_ref[bi] = v_new[r0 + lt - WINDOW:r0 + lt]
        else:
            kn_ref[bi, 0:WINDOW - lt, :] = k0_ref[bi, lt:WINDOW, :]
            kn_ref[bi, WINDOW - lt:WINDOW, :] = k_rot[r0:r0 + lt]
            vn_ref[bi, 0:WINDOW - lt, :] = v0_ref[bi, lt:WINDOW, :]
            vn_ref[bi, WINDOW - lt:WINDOW, :] = v_new[r0:r0 + lt]

    w2 = 2 * lq
    lane2 = lax.broadcasted_iota(jnp.int32, (lq, w2), 1)
    is_a2 = lane2 < lq
    lane_p = lax.broadcasted_iota(jnp.int32, (lq, LANES), 1)
    first = s_idx == 0
    for bi in range(nb):
        for j in range(n_chunks):
            p0 = j * lq
            o0 = WINDOW + j * lq
            q0 = bi * lt + j * lq
            for h in range(N_KV_HEADS):
                kmat = jnp.concatenate([kw_ref[bi, h, 0, p0:p0 + WINDOW, :], kw_ref[bi, h, 1, p0:p0 + WINDOW, :],
                                        kw_ref[bi, h, 0, o0:o0 + lq, :], kw_ref[bi, h, 1, o0:o0 + lq, :]], axis=0)
                vmat = jnp.concatenate([vw_ref[bi, h, 0, p0:p0 + WINDOW, :], vw_ref[bi, h, 1, p0:p0 + WINDOW, :],
                                        vw_ref[bi, h, 0, o0:o0 + lq, :], vw_ref[bi, h, 1, o0:o0 + lq, :]], axis=0)
                qs = jnp.concatenate([q_blocks[2 * h][q0:q0 + lq], q_blocks[2 * h + 1][q0:q0 + lq]], axis=0)
                sc = _dot_nt(qs, kmat)
                probs = []
                scales = []
                for p in range(2):
                    sp = sc[p * lq:(p + 1) * lq]
                    c0 = sp[:, 0:LANES]
                    c1 = sp[:, LANES:2 * LANES]
                    c2 = sp[:, 2 * LANES:2 * LANES + w2]
                    if not has_past and j * lq < WINDOW:
                        n_bad = WINDOW - j * lq
                        bad = jnp.logical_and(first, lane_p < n_bad)
                        c0 = jnp.where(bad, NEG_INF, c0)
                        c1 = jnp.where(bad, NEG_INF, c1)
                    sink_a = sinks_ref[4 * h + 2 * p]
                    sink_b = sinks_ref[4 * h + 2 * p + 1]
                    ma = jnp.maximum(jnp.max(c0, axis=1, keepdims=True), sink_a)
                    ma = jnp.maximum(ma, jnp.max(jnp.where(is_a2, c2, NEG_INF), axis=1, keepdims=True))
                    mb = jnp.maximum(jnp.max(c1, axis=1, keepdims=True), sink_b)
                    mb = jnp.maximum(mb, jnp.max(jnp.where(is_a2, NEG_INF, c2), axis=1, keepdims=True))
                    e0 = jnp.exp(c0 - ma)
                    e1 = jnp.exp(c1 - mb)
                    e2 = jnp.exp(c2 - jnp.where(is_a2, ma, mb))
                    da = (jnp.sum(e0, axis=1, keepdims=True) + jnp.sum(jnp.where(is_a2, e2, 0.0), axis=1, keepdims=True)
                          + jnp.exp(sink_a - ma))
                    db = (jnp.sum(e1, axis=1, keepdims=True) + jnp.sum(jnp.where(is_a2, 0.0, e2), axis=1, keepdims=True)
                          + jnp.exp(sink_b - mb))
                    probs.append(jnp.concatenate([e0, e1, e2], axis=1).astype(_BF16))
                    scales.append(jnp.where(lane_p < HEAD_DIM, 1.0 / da, 1.0 / db))
                ov = _dot(jnp.concatenate(probs, axis=0), vmat)
                for p in range(2):
                    oat_ref[q0:q0 + lq, (2 * h + p) * LANES:(2 * h + p + 1) * LANES] = (
                        ov[p * lq:(p + 1) * lq] * scales[p]).astype(_BF16)

    if lt >= WINDOW:
        @pl.when(s_idx + 1 < n_s)
        def _carry_kv():
            for bi in range(nb):
                for h in range(N_KV_HEADS):
                    for v in range(2):
                        kw_ref[bi, h, v, 0:WINDOW, :] = kw_ref[bi, h, v, lt:lt + WINDOW, :]
                        vw_ref[bi, h, v, 0:WINDOW, :] = vw_ref[bi, h, v, lt:lt + WINDOW, :]

    xr = _dot(xb, w_in_ref[:, OFF_XR:OFF_YR])
    for bi in range(nb):
        r0 = bi * lt
        xp_ref[bi, CONV_HDR:CONV_HDR + lt, :] = xr[r0:r0 + lt]
        acc = xp_ref[bi, CONV_HDR - 3:CONV_HDR - 3 + lt, :] * convw_ref[0:1, :]
        for t in range(1, CONV_W):
            acc = acc + xp_ref[bi, CONV_HDR - 3 + t:CONV_HDR - 3 + t + lt, :] * convw_ref[t:t + 1, :]
        xc_ref[r0:r0 + lt, :] = acc + convb_ref[...]
        tail = xp_ref[bi, CONV_HDR + lt - (CONV_W - 1):CONV_HDR + lt, :]
        cn_ref[bi] = tail
        xp_ref[bi, CONV_HDR - (CONV_W - 1):CONV_HDR, :] = tail

    xc = xc_ref[...]
    gates = _dot(xc.astype(_BF16), wg_ref[...]) + bg_ref[...]
    r_gate = jax.nn.sigmoid(gates[:, 0:D_RNN])
    i_gate = jax.nn.sigmoid(gates[:, D_RNN:2 * D_RNN])
    lam = lam_ref[...]
    softplus_neg = jnp.maximum(-lam, 0.0) + jnp.log(1.0 + jnp.exp(-jnp.abs(lam)))
    log_a = (-LRU_C * r_gate) * softplus_neg
    a_ref[...] = jnp.exp(log_a)
    th = jnp.tanh(log_a)
    b_ref[...] = jnp.sqrt(-2.0 * th / (1.0 - th)) * i_gate * xc

    for bi in range(nb):
        r0 = bi * lt

        def scan_step(t, hprev, r0=r0):
            hcur = a_ref[pl.ds(r0 + t, 1), :] * hprev + b_ref[pl.ds(r0 + t, 1), :]
            b_ref[pl.ds(r0 + t, 1), :] = hcur
            return hcur

        h_last = lax.fori_loop(0, lt, scan_step, hc_ref[bi], unroll=8)
        hc_ref[bi] = h_last
        hn_ref[bi] = h_last

    yr = _dot(xb, w_in_ref[:, OFF_YR:OFF_GA])
    rnn = b_ref[...] * jax.nn.gelu(yr)

    ga = _dot(xb, w_in_ref[:, OFF_GA:OFF_GR])
    gr = _dot(xb, w_in_ref[:, OFF_GR:IN_DIM])
    merged = (jax.nn.sigmoid(ga) * _dot(oat_ref[...], wao_ref[...])
              + jax.nn.sigmoid(gr) * _dot(rnn.astype(_BF16), wro_ref[...]))
    mix = _dot(merged.astype(_BF16), wout_ref[...])
    x1 = _layer_norm(DN_ALPHA * x + mix, g1_ref[...], b1_ref[...])
    for c in range(SLAB):
        x1s_ref[pl.ds(c, rows, stride=SLAB), :] = x1[:, c * LANES:(c + 1) * LANES]

    logits = _dot(x1.astype(_BF16), rw_ref[...]) + rb_ref[...]
    e_iota = lax.broadcasted_iota(jnp.int32, logits.shape, 1)
    work = logits
    top_vals = []
    top_idx = []
    top_sel = []
    for _ in range(TOP_K):
        m = jnp.max(work, axis=1, keepdims=True)
        idx = jnp.min(jnp.where(work == m, e_iota, N_EXPERTS), axis=1, keepdims=True)
        sel = e_iota == idx
        top_vals.append(m)
        top_idx.append(idx)
        top_sel.append(sel)
        work = jnp.where(sel, -jnp.inf, work)
    exps = [jnp.exp(v - top_vals[0]) for v in top_vals]
    denom = exps[0]
    for e in exps[1:]:
        denom = denom + e

    onehot = jnp.zeros_like(logits)
    for sel in top_sel:
        onehot = onehot + jnp.where(sel, 1.0, 0.0)
    r_i = lax.broadcasted_iota(jnp.int32, (rows, rows), 0)
    c_i = lax.broadcasted_iota(jnp.int32, (rows, rows), 1)
    earlier = jnp.where(c_i < r_i, 1.0, 0.0).astype(_BF16)
    before = _dot(earlier, onehot.astype(_BF16)) + run_ref[...]
    run_ref[...] = run_ref[...] + jnp.sum(onehot, axis=0, keepdims=True)
    cnt_ref[...] = run_ref[...]

    lane_m = lax.broadcasted_iota(jnp.int32, (rows, LANES), 1)
    meta_i = jnp.zeros((rows, LANES), jnp.int32)
    meta_g = jnp.zeros((rows, LANES), _F32)
    for k in range(TOP_K):
        rank_k = jnp.sum(jnp.where(top_sel[k], before, 0.0), axis=1, keepdims=True).astype(jnp.int32)
        meta_i = jnp.where(lane_m == k, top_idx[k], meta_i)
        meta_i = jnp.where(lane_m == TOP_K + k, rank_k, meta_i)
        meta_g = jnp.where(lane_m == k, exps[k] / denom, meta_g)
    meta_ref[...] = meta_i
    gate_ref[...] = meta_g


def _const_spec(shape):
    nd = len(shape)
    return pl.BlockSpec(shape, lambda b, s, _nd=nd: (0,) * _nd, pipeline_mode=pl.Buffered(1))


def _mixer(x, rope, k0, v0, c0, h0, sinks, prm, *, has_past, nb, lt, lq):
    bsz, seq, _ = x.shape
    assert bsz % nb == 0 and seq % lt == 0 and lt % lq == 0 and lq % 32 == 0
    assert lt >= WINDOW or seq == lt
    n_s = seq // lt
    rows = nb * lt
    n_tok = bsz * seq
    kernel = functools.partial(_mixer_kernel, has_past, nb, lt, lq)
    batch_spec = lambda shape: pl.BlockSpec((nb,) + shape, lambda b, s: (b,) + (0,) * len(shape))
    in_specs = [
        pl.BlockSpec((nb, lt, D_MODEL), lambda b, s: (b, s, 0)),
        pl.BlockSpec((3, lt, LANES), lambda b, s: (0, s, 0)),
        _const_spec((D_MODEL, IN_DIM)),
        batch_spec((WINDOW, KV_DIM)), batch_spec((WINDOW, KV_DIM)),
        batch_spec((CONV_W - 1, D_RNN)), batch_spec((1, D_RNN)),
        pl.BlockSpec(memory_space=pltpu.SMEM),
        _const_spec((CONV_W, D_RNN)), _const_spec((1, D_RNN)),
        _const_spec((D_RNN, 2 * D_RNN)), _const_spec((1, 2 * D_RNN)), _const_spec((1, D_RNN)),
        _const_spec((Q_DIM, D_MODEL)), _const_spec((D_RNN, D_MODEL)), _const_spec((D_MODEL, D_MODEL)),
        _const_spec((1, D_MODEL)), _const_spec((1, D_MODEL)),
        _const_spec((D_MODEL, N_EXPERTS)), _const_spec((1, N_EXPERTS)),
    ]
    tok_map = lambda b, s: (b * n_s + s, 0)
    out_shape = (
        jax.ShapeDtypeStruct((n_tok * SLAB, LANES), _F32),
        jax.ShapeDtypeStruct((n_tok, LANES), jnp.int32),
        jax.ShapeDtypeStruct((n_tok, LANES), _F32),
        jax.ShapeDtypeStruct((1, N_EXPERTS), _F32),
        jax.ShapeDtypeStruct((bsz, WINDOW, KV_DIM), _F32),
        jax.ShapeDtypeStruct((bsz, WINDOW, KV_DIM), _F32),
        jax.ShapeDtypeStruct((bsz, CONV_W - 1, D_RNN), _F32),
        jax.ShapeDtypeStruct((bsz, 1, D_RNN), _F32),
    )
    out_specs = (
        pl.BlockSpec((rows * SLAB, LANES), tok_map),
        pl.BlockSpec((rows, LANES), tok_map),
        pl.BlockSpec((rows, LANES), tok_map),
        pl.BlockSpec((1, N_EXPERTS), lambda b, s: (0, 0)),
        batch_spec((WINDOW, KV_DIM)), batch_spec((WINDOW, KV_DIM)),
        batch_spec((CONV_W - 1, D_RNN)), batch_spec((1, D_RNN)),
    )
    scratch = [
        pltpu.VMEM((nb, N_KV_HEADS, 2, WINDOW + lt, LANES), _BF16),
        pltpu.VMEM((nb, N_KV_HEADS, 2, WINDOW + lt, LANES), _BF16),
        pltpu.VMEM((nb, CONV_HDR + lt, D_RNN), _F32),
        pltpu.VMEM((rows, D_RNN), _F32),
        pltpu.VMEM((rows, D_RNN), _F32),
        pltpu.VMEM((rows, D_RNN), _F32),
        pltpu.VMEM((nb, 1, D_RNN), _F32),
        pltpu.VMEM((rows, Q_DIM), _BF16),
        pltpu.VMEM((1, N_EXPERTS), _F32),
    ]
    args = [x, rope, prm["w_in"], k0, v0, c0, h0, sinks, prm["conv_w"], prm["conv_b"], prm["wg"], prm["bg"],
            prm["lam"], prm["wao"], prm["wro"], prm["wout"], prm["g1"], prm["b1"], prm["rw"], prm["rb"]]
    return pl.pallas_call(
        kernel,
        grid=(bsz // nb, n_s),
        in_specs=in_specs,
        out_specs=out_specs,
        out_shape=out_shape,
        scratch_shapes=scratch,
        compiler_params=pltpu.CompilerParams(
            dimension_semantics=("arbitrary", "arbitrary"), vmem_limit_bytes=VMEM_LIMIT_BYTES),
        name="mixer_past" if has_past else "mixer_prompt",
    )(*args)


def _slab_rows(ref, first_tok, n):
    return jnp.concatenate([ref[pl.ds(first_tok * SLAB + c, n, stride=SLAB), :] for c in range(SLAB)], axis=1)


def _dispatch_kernel(tt, tm, n_first, n_tiles, zt_ref, nu_ref, pos_ref, xa_hbm, xb_hbm, xs_hbm, zero_ref, sem):
    i = pl.program_id(0)

    @pl.when(i == 0)
    def _zero_padding():
        zero_ref[...] = jnp.zeros_like(zero_ref)

        def clear(tile):
            dst = xs_hbm.at[pl.ds(pl.multiple_of(tile * (tm * SLAB), SLAB), tm * SLAB)]
            cp = pltpu.make_async_copy(zero_ref, dst, sem)
            cp.start()
            cp.wait()

        for e in range(N_EXPERTS):
            pl.when(zt_ref[e] >= 0)(functools.partial(clear, zt_ref[e]))

        def clear_unused(tile, carry):
            clear(tile)
            return carry

        lax.fori_loop(nu_ref[0], n_tiles, clear_unused, 0)

    def scatter(src_hbm, first_tok):
        def issue(r, carry):
            src = src_hbm.at[pl.ds(pl.multiple_of((first_tok + r) * SLAB, SLAB), SLAB)]
            for k in range(TOP_K):
                dst = xs_hbm.at[pl.ds(pl.multiple_of(pos_ref[0, 0, r * TOP_K + k] * SLAB, SLAB), SLAB)]
                pltpu.make_async_copy(src, dst, sem).start()
            return carry

        lax.fori_loop(0, tt, issue, 0)

    pl.when(i < n_first)(lambda: scatter(xa_hbm, i * tt))
    pl.when(i >= n_first)(lambda: scatter(xb_hbm, (i - n_first) * tt))
    n_rows = tt * TOP_K * SLAB
    pltpu.make_async_copy(xa_hbm.at[pl.ds(0, n_rows)], xs_hbm.at[pl.ds(0, n_rows)], sem).wait()


def _dispatch(x1s_a, x1s_b, pos, zero_tiles, n_used, *, tt, tm, n_tiles):
    n_a = x1s_a.shape[0] // SLAB
    n_b = x1s_b.shape[0] // SLAB
    assert n_a % tt == 0 and n_b % tt == 0 and n_a >= tt * TOP_K
    n_steps = (n_a + n_b) // tt
    return pl.pallas_call(
        functools.partial(_dispatch_kernel, tt, tm, n_a // tt, n_tiles),
        grid_spec=pltpu.PrefetchScalarGridSpec(
            num_scalar_prefetch=2,
            grid=(n_steps,),
            in_specs=[
                pl.BlockSpec((1, 1, tt * TOP_K), lambda i, zt, nu: (i, 0, 0), memory_space=pltpu.SMEM),
                pl.BlockSpec(memory_space=pl.ANY),
                pl.BlockSpec(memory_space=pl.ANY),
            ],
            out_specs=pl.BlockSpec(memory_space=pl.ANY),
            scratch_shapes=[pltpu.VMEM((tm * SLAB, LANES), _F32), pltpu.SemaphoreType.DMA],
        ),
        out_shape=jax.ShapeDtypeStruct((n_tiles * tm * SLAB, LANES), _F32),
        compiler_params=pltpu.CompilerParams(dimension_semantics=("arbitrary",)),
        name="moe_dispatch",
    )(zero_tiles, n_used, pos.reshape(n_steps, 1, tt * TOP_K), x1s_a, x1s_b)


def _expert_kernel(tm, te_ref, nu_ref, xs_ref, wgu_ref, bgu_ref, wd_ref, bd_ref, ys_ref, wgu_b, wd_b):
    i = pl.program_id(0)

    @pl.when(i < nu_ref[0])
    def _tile():
        prev_e = te_ref[jnp.maximum(i - 1, 0)]

        @pl.when(jnp.logical_or(i == 0, te_ref[i] != prev_e))
        def _new_expert():
            wgu_b[...] = wgu_ref[0].astype(_BF16)
            wd_b[...] = wd_ref[0].astype(_BF16)

        x = _slab_rows(xs_ref, 0, tm).astype(_BF16)
        hgu = _dot(x, wgu_b[...]) + bgu_ref[0]
        glu = jnp.minimum(hgu[:, 0:D_FF], SWIGLU_LIMIT)
        lin = jnp.clip(hgu[:, D_FF:2 * D_FF], -SWIGLU_LIMIT, SWIGLU_LIMIT)
        hh = glu * jax.nn.sigmoid(SWIGLU_ALPHA * glu) * (lin + 1.0)
        y = _dot(hh.astype(_BF16), wd_b[...]) + bd_ref[0]
        for c in range(SLAB):
            ys_ref[pl.ds(c, tm, stride=SLAB), :] = y[:, c * LANES:(c + 1) * LANES]

    @pl.when(i >= nu_ref[0])
    def _unused_tile():
        ys_ref[...] = jnp.zeros_like(ys_ref)


def _experts(xs, tile_expert, n_used, prm, *, tm):
    n_tiles = xs.shape[0] // (tm * SLAB)
    row_map = lambda i, te, nu: (jnp.minimum(i, nu[0] - 1), 0)
    out_map = lambda i, te, nu: (i, 0)
    exp_map = lambda i, te, nu: (te[i], 0, 0)
    return pl.pallas_call(
        functools.partial(_expert_kernel, tm),
        grid_spec=pltpu.PrefetchScalarGridSpec(
            num_scalar_prefetch=2,
            grid=(n_tiles,),
            in_specs=[
                pl.BlockSpec((tm * SLAB, LANES), row_map),
                pl.BlockSpec((1, D_MODEL, 2 * D_FF), exp_map),
                pl.BlockSpec((1, 1, 2 * D_FF), exp_map),
                pl.BlockSpec((1, D_FF, D_MODEL), exp_map),
                pl.BlockSpec((1, 1, D_MODEL), exp_map),
            ],
            out_specs=pl.BlockSpec((tm * SLAB, LANES), out_map),
            scratch_shapes=[pltpu.VMEM((D_MODEL, 2 * D_FF), _BF16), pltpu.VMEM((D_FF, D_MODEL), _BF16)],
        ),
        out_shape=jax.ShapeDtypeStruct(xs.shape, _F32),
        compiler_params=pltpu.CompilerParams(
            dimension_semantics=("arbitrary",), vmem_limit_bytes=VMEM_LIMIT_BYTES),
        name="moe_experts",
    )(tile_expert, n_used, xs, prm["wgu"], prm["bgu"], prm["wd"], prm["bd"])


def _combine_kernel(tt, n_first, pos_ref, gate_ref, xa_ref, xb_ref, ys_hbm, g2_ref, b2_ref, ya_ref, yb_ref, ybuf, sem):
    i = pl.program_id(0)

    def issue(r, carry):
        for k in range(TOP_K):
            src = ys_hbm.at[pl.ds(pl.multiple_of(pos_ref[0, 0, r * TOP_K + k] * SLAB, SLAB), SLAB)]
            dst = ybuf.at[pl.ds(pl.multiple_of((k * tt + r) * SLAB, SLAB), SLAB)]
            pltpu.make_async_copy(src, dst, sem).start()
        return carry

    lax.fori_loop(0, tt, issue, 0)
    n_rows = tt * TOP_K * SLAB
    pltpu.make_async_copy(ys_hbm.at[pl.ds(0, n_rows)], ybuf, sem).wait()

    gates = gate_ref[...]
    acc = gates[:, 0:1] * _slab_rows(ybuf, 0, tt)
    for k in range(1, TOP_K):
        acc = acc + gates[:, k:k + 1] * _slab_rows(ybuf, k * tt, tt)

    def finish(x_ref, y_ref):
        x1 = _slab_rows(x_ref, 0, tt)
        y_ref[...] = _layer_norm(DN_ALPHA * x1 + acc, g2_ref[...], b2_ref[...])

    pl.when(i < n_first)(lambda: finish(xa_ref, ya_ref))
    pl.when(i >= n_first)(lambda: finish(xb_ref, yb_ref))


def _combine(x1s_a, x1s_b, ys, pos, gates, prm, *, tt):
    n_a = x1s_a.shape[0] // SLAB
    n_b = x1s_b.shape[0] // SLAB
    assert n_a % tt == 0 and n_b % tt == 0
    n_first = n_a // tt
    n_steps = (n_a + n_b) // tt
    a_map = lambda i: (jnp.minimum(i, n_first - 1), 0)
    b_map = lambda i: (jnp.maximum(i - n_first, 0), 0)
    return pl.pallas_call(
        functools.partial(_combine_kernel, tt, n_first),
        grid=(n_steps,),
        in_specs=[
            pl.BlockSpec((1, 1, tt * TOP_K), lambda i: (i, 0, 0), memory_space=pltpu.SMEM),
            pl.BlockSpec((tt, TOP_K), lambda i: (i, 0)),
            pl.BlockSpec((tt * SLAB, LANES), a_map),
            pl.BlockSpec((tt * SLAB, LANES), b_map),
            pl.BlockSpec(memory_space=pl.ANY),
            pl.BlockSpec((1, D_MODEL), lambda i: (0, 0)),
            pl.BlockSpec((1, D_MODEL), lambda i: (0, 0)),
        ],
        out_specs=(pl.BlockSpec((tt, D_MODEL), a_map), pl.BlockSpec((tt, D_MODEL), b_map)),
        out_shape=(jax.ShapeDtypeStruct((n_a, D_MODEL), _F32), jax.ShapeDtypeStruct((n_b, D_MODEL), _F32)),
        scratch_shapes=[pltpu.VMEM((TOP_K * tt * SLAB, LANES), _F32), pltpu.SemaphoreType.DMA],
        compiler_params=pltpu.CompilerParams(dimension_semantics=("arbitrary",)),
        name="moe_combine",
    )(pos.reshape(n_steps, 1, tt * TOP_K), gates, x1s_a, x1s_b, ys, prm["g2"], prm["b2"])


def _route(meta_a, meta_b, gate_a, gate_b, counts_a, counts_b, tm, n_tiles):
    meta = jnp.concatenate([meta_a[:, 0:2 * TOP_K], meta_b[:, 0:2 * TOP_K]], axis=0)
    idx = meta[:, 0:TOP_K]
    rank = meta[:, TOP_K:2 * TOP_K]
    gates = jnp.concatenate([gate_a[:, 0:TOP_K], gate_b[:, 0:TOP_K]], axis=0)
    cp = counts_a.reshape(-1).astype(jnp.int32)
    cs = counts_b.reshape(-1).astype(jnp.int32)
    is_b = (jnp.arange(meta.shape[0]) >= meta_a.shape[0])[:, None]
    rank = rank + jnp.where(is_b, cp[idx], 0)
    tiles_per = (cp + cs + tm - 1) // tm
    tile_end = jnp.cumsum(tiles_per)
    tile_start = tile_end - tiles_per
    pos = tile_start[idx] * tm + rank
    n_used = tile_end[-1:]
    tile_ids = jnp.arange(n_tiles, dtype=jnp.int32)
    tile_expert = jnp.minimum(jnp.sum(tile_ids[:, None] >= tile_end[None, :], axis=1), N_EXPERTS - 1)
    last_e = tile_expert[jnp.maximum(n_used[0] - 1, 0)]
    tile_expert = jnp.where(tile_ids < n_used[0], tile_expert, last_e).astype(jnp.int32)
    zero_tiles = jnp.where(tiles_per > 0, tile_end - 1, -1).astype(jnp.int32)
    return pos.astype(jnp.int32), gates, tile_expert, n_used.astype(jnp.int32), zero_tiles


def _rope_tables(pos):
    half = ROT_DIM // 2
    inv_freq = ROPE_THETA ** (-jnp.arange(half, dtype=_F32) / half)
    ang = pos.astype(_F32)[:, None] * inv_freq[None, :]
    cos, sin = jnp.cos(ang), jnp.sin(ang)
    n = pos.shape[0]
    ones = jnp.ones((n, HEAD_DIM - ROT_DIM), _F32)
    zeros = jnp.zeros((n, HEAD_DIM - ROT_DIM), _F32)
    zh = jnp.zeros((n, half), _F32)
    cos_t = jnp.concatenate([cos, cos, ones], axis=1)
    sin_up = jnp.concatenate([-sin, zh, zeros], axis=1)
    sin_dn = jnp.concatenate([zh, sin, zeros], axis=1)
    tab = jnp.stack([cos_t, sin_up, sin_dn])
    return jnp.concatenate([tab, tab], axis=2)


def _block_diag(w):
    hd, d, _ = w.shape
    eye = jnp.eye(hd, dtype=w.dtype)
    return (eye[:, None, :, None] * w[:, :, None, :]).reshape(hd * d, hd * d)


def kernel(x_prompt, x_sample, cache_k, cache_v, state_conv, state_h, w_in, attn_sinks, w_attn_out, conv_w, conv_b, gate_a_w, gate_a_b, gate_x_w, gate_x_b, lru_lambda, w_rnn_out, w_out, ln1_g, ln1_b, router_w, router_b, w_gate_up, b_gate_up, w_down, b_down, ln2_g, ln2_b):
    assert w_in.shape[0] == DEPTH == 1
    l = 0
    bsz, seq, _ = x_prompt.shape
    dbsz, dseq, _ = x_sample.shape
    row = lambda a: a.reshape(1, -1)
    prm = {
        "w_in": w_in[l].astype(_BF16),
        "conv_w": conv_w[l], "conv_b": row(conv_b[l]),
        "wg": jnp.concatenate([_block_diag(gate_a_w[l]), _block_diag(gate_x_w[l])], axis=1).astype(_BF16),
        "bg": row(jnp.concatenate([gate_a_b[l], gate_x_b[l]])),
        "lam": row(lru_lambda[l]),
        "wao": w_attn_out[l].astype(_BF16), "wro": w_rnn_out[l].astype(_BF16), "wout": w_out[l].astype(_BF16),
        "g1": row(ln1_g[l]), "b1": row(ln1_b[l]),
        "rw": router_w[l].astype(_BF16), "rb": row(router_b[l]),
        "wgu": w_gate_up[l], "bgu": b_gate_up[l][:, None, :],
        "wd": w_down[l], "bd": b_down[l][:, None, :],
        "g2": row(ln2_g[l]), "b2": row(ln2_b[l]),
    }
    sinks = attn_sinks[l]

    rope_p = _rope_tables(jnp.arange(seq, dtype=jnp.int32))
    rope_s = _rope_tables(PAST_LEN + jnp.arange(dseq, dtype=jnp.int32))

    n_prompt = bsz * seq
    n_tok = n_prompt + dbsz * dseq
    zeros_kv = jnp.zeros((bsz, WINDOW, KV_DIM), _F32)
    x1_p, meta_p, gate_p, cnt_p, pk, pv, pc, ph = _mixer(
        x_prompt, rope_p, zeros_kv, zeros_kv, jnp.zeros((bsz, CONV_W - 1, D_RNN), _F32),
        jnp.zeros((bsz, 1, D_RNN), _F32), sinks, prm, has_past=False, nb=1, lt=min(MIXER_ROWS, seq), lq=CHUNK)
    x1_s, meta_s, gate_s, cnt_s, sk, sv, sc, sh = _mixer(
        x_sample, rope_s, cache_k[l].reshape(dbsz, WINDOW, KV_DIM), cache_v[l].reshape(dbsz, WINDOW, KV_DIM),
        state_conv[l], state_h[l][:, None, :], sinks, prm, has_past=True, nb=min(MIXER_ROWS // dseq, dbsz),
        lt=dseq, lq=dseq)

    tm = EXPERT_ROWS
    n_tiles = (n_tok * TOP_K + N_EXPERTS * (tm - 1)) // tm + 1
    pos, gates, tile_expert, n_used, zero_tiles = _route(meta_p, meta_s, gate_p, gate_s, cnt_p, cnt_s, tm, n_tiles)
    xs = _dispatch(x1_p, x1_s, pos, zero_tiles, n_used, tt=TOKEN_ROWS, tm=tm, n_tiles=n_tiles)
    ysort = _experts(xs, tile_expert, n_used, prm, tm=tm)
    yp, ys = _combine(x1_p, x1_s, ysort, pos, gates, prm, tt=TOKEN_ROWS)
    yp = yp.reshape(bsz, seq, D_MODEL)
    ys = ys.reshape(dbsz, dseq, D_MODEL)

    kv5 = lambda a: a.reshape(1, a.shape[0], WINDOW, N_KV_HEADS, HEAD_DIM)
    return (yp, ys, kv5(pk), kv5(pv), pc[None], ph.reshape(1, bsz, D_RNN),
            kv5(sk), kv5(sv), sc[None], sh.reshape(1, dbsz, D_RNN))
```

```python
import functools

import jax
import jax.numpy as jnp
import numpy as np
from jax import lax
from jax.experimental import pallas as pl
from jax.experimental.pallas import tpu as pltpu

D_MODEL = 1024
PAST_LEN = 1024
CHUNK = 64
N_HEADS = 16
N_KV_HEADS = 4
HEAD_DIM = 64
GROUP = N_HEADS // N_KV_HEADS
ROT_DIM = HEAD_DIM // 4
ROPE_THETA = 500000.0
WINDOW = 128
ATTN_SCALE = HEAD_DIM ** -0.5
NEG_INF = -1e30
D_RNN = 1280
RNN_HEADS = 16
RNN_HEAD_DIM = D_RNN // RNN_HEADS
CONV_W = 4
LRU_C = 8.0
N_EXPERTS = 32
TOP_K = 4
D_FF = 1024
SWIGLU_ALPHA = 1.702
SWIGLU_LIMIT = 7.0
LN_EPS = 1e-5
DEPTH = 1
DN_ALPHA = (2.0 * DEPTH) ** 0.25

Q_DIM = N_HEADS * HEAD_DIM
KV_DIM = N_KV_HEADS * HEAD_DIM
OFF_K = Q_DIM
OFF_V = OFF_K + KV_DIM
OFF_XR = OFF_V + KV_DIM
OFF_YR = OFF_XR + D_RNN
OFF_GA = OFF_YR + D_RNN
OFF_GR = OFF_GA + D_MODEL
IN_DIM = OFF_GR + D_MODEL

LANES = 128
SLAB = D_MODEL // LANES
MIXER_ROWS = 256
EXPERT_ROWS = 256
TOKEN_ROWS = 256
CONV_HDR = 8
VMEM_LIMIT_BYTES = 56 * 1024 * 1024

_BF16 = jnp.bfloat16
_F32 = jnp.float32


def _dot(a, b):
    return jnp.dot(a, b, preferred_element_type=_F32)


def _dot_nt(a, b):
    return lax.dot_general(a, b, (((1,), (1,)), ((), ())), preferred_element_type=_F32)


def _layer_norm(x, g, b):
    mu = jnp.mean(x, axis=-1, keepdims=True)
    xc = x - mu
    var = jnp.mean(xc * xc, axis=-1, keepdims=True)
    return xc * lax.rsqrt(var + LN_EPS) * g + b


def _rope(x, cos_t, sin_up, sin_dn):
    return x * cos_t + pltpu.roll(x, LANES - ROT_DIM // 2, 1) * sin_up + pltpu.roll(x, ROT_DIM // 2, 1) * sin_dn


def _lo_hi(slab, h):
    blk = slab[:, (h // 2) * LANES:(h // 2 + 1) * LANES]
    lane = lax.broadcasted_iota(jnp.int32, blk.shape, 1)
    if h % 2 == 0:
        lo = jnp.where(lane < HEAD_DIM, blk, 0.0)
        hi = pltpu.roll(lo, HEAD_DIM, 1)
    else:
        hi = jnp.where(lane >= HEAD_DIM, blk, 0.0)
        lo = pltpu.roll(hi, HEAD_DIM, 1)
    return lo.astype(_BF16), hi.astype(_BF16)


def _mixer_kernel(has_past, nb, lt, lq,
                  x_ref, rope_ref, w_in_ref, k0_ref, v0_ref, c0_ref, h0_ref, sinks_ref,
                  convw_ref, convb_ref, wg_ref, bg_ref, lam_ref, wao_ref, wro_ref, wout_ref,
                  g1_ref, b1_ref, rw_ref, rb_ref,
                  x1s_ref, meta_ref, gate_ref, cnt_ref, kn_ref, vn_ref, cn_ref, hn_ref,
                  kw_ref, vw_ref, xp_ref, xc_ref, a_ref, b_ref, hc_ref, oat_ref, run_ref):
    s_idx = pl.program_id(1)
    n_s = pl.num_programs(1)
    rows = nb * lt
    n_chunks = lt // lq

    @pl.when(jnp.logical_and(pl.program_id(0) == 0, s_idx == 0))
    def _init_counts():
        run_ref[...] = jnp.zeros_like(run_ref)

    @pl.when(s_idx == 0)
    def _init():
        for bi in range(nb):
            for h in range(N_KV_HEADS):
                klo, khi = _lo_hi(k0_ref[bi], h)
                vlo, vhi = _lo_hi(v0_ref[bi], h)
                kw_ref[bi, h, 0, 0:WINDOW, :] = klo
                kw_ref[bi, h, 1, 0:WINDOW, :] = khi
                vw_ref[bi, h, 0, 0:WINDOW, :] = vlo
                vw_ref[bi, h, 1, 0:WINDOW, :] = vhi
            xp_ref[bi, CONV_HDR - (CONV_W - 1):CONV_HDR, :] = c0_ref[bi]
            hc_ref[bi] = h0_ref[bi]

    x = x_ref[...].reshape(rows, D_MODEL)
    xb = x.astype(_BF16)

    cos_t = jnp.concatenate([rope_ref[0]] * nb, axis=0) if nb > 1 else rope_ref[0]
    sin_up = jnp.concatenate([rope_ref[1]] * nb, axis=0) if nb > 1 else rope_ref[1]
    sin_dn = jnp.concatenate([rope_ref[2]] * nb, axis=0) if nb > 1 else rope_ref[2]

    zq = _dot(xb, w_in_ref[:, 0:Q_DIM])
    q_blocks = []
    for c in range(Q_DIM // LANES):
        qr = _rope(zq[:, c * LANES:(c + 1) * LANES], cos_t, sin_up, sin_dn)
        q_blocks.append((qr * ATTN_SCALE).astype(_BF16))
    zk = _dot(xb, w_in_ref[:, OFF_K:OFF_V])
    k_rot = jnp.concatenate(
        [_rope(zk[:, c * LANES:(c + 1) * LANES], cos_t, sin_up, sin_dn) for c in range(KV_DIM // LANES)], axis=1)
    v_new = _dot(xb, w_in_ref[:, OFF_V:OFF_XR])

    for bi in range(nb):
        r0 = bi * lt
        for h in range(N_KV_HEADS):
            klo, khi = _lo_hi(k_rot[r0:r0 + lt], h)
            vlo, vhi = _lo_hi(v_new[r0:r0 + lt], h)
            kw_ref[bi, h, 0, WINDOW:WINDOW + lt, :] = klo
            kw_ref[bi, h, 1, WINDOW:WINDOW + lt, :] = khi
            vw_ref[bi, h, 0, WINDOW:WINDOW + lt, :] = vlo
            vw_ref[bi, h, 1, WINDOW:WINDOW + lt, :] = vhi

    for bi in range(nb):
        r0 = bi * lt
        if lt >= WINDOW:
            kn_ref[bi] = k_rot[r0 + lt - WINDOW:r0 + lt]
            vn_ref[bi] = v_new[r0 + lt - WINDOW:r0 + lt]
        else:
            kn_ref[bi, 0:WINDOW - lt, :] = k0_ref[bi, lt:WINDOW, :]
            kn_ref[bi, WINDOW - lt:WINDOW, :] = k_rot[r0:r0 + lt]
            vn_ref[bi, 0:WINDOW - lt, :] = v0_ref[bi, lt:WINDOW, :]
            vn_ref[bi, WINDOW - lt:WINDOW, :] = v_new[r0:r0 + lt]

    w2 = 2 * lq
    lane2 = lax.broadcasted_iota(jnp.int32, (lq, w2), 1)
    is_a2 = lane2 < lq
    lane_p = lax.broadcasted_iota(jnp.int32, (lq, LANES), 1)
    first = s_idx == 0
    for bi in range(nb):
        for j in range(n_chunks):
            p0 = j * lq
            o0 = WINDOW + j * lq
            q0 = bi * lt + j * lq
            for h in range(N_KV_HEADS):
                kmat = jnp.concatenate([kw_ref[bi, h, 0, p0:p0 + WINDOW, :], kw_ref[bi, h, 1, p0:p0 + WINDOW, :],
                                        kw_ref[bi, h, 0, o0:o0 + lq, :], kw_ref[bi, h, 1, o0:o0 + lq, :]], axis=0)
                vmat = jnp.concatenate([vw_ref[bi, h, 0, p0:p0 + WINDOW, :], vw_ref[bi, h, 1, p0:p0 + WINDOW, :],
                                        vw_ref[bi, h, 0, o0:o0 + lq, :], vw_ref[bi, h, 1, o0:o0 + lq, :]], axis=0)
                qs = jnp.concatenate([q_blocks[2 * h][q0:q0 + lq], q_blocks[2 * h + 1][q0:q0 + lq]], axis=0)
                sc = _dot_nt(qs, kmat)
                probs = []
                scales = []
                for p in range(2):
                    sp = sc[p * lq:(p + 1) * lq]
                    c0 = sp[:, 0:LANES]
                    c1 = sp[:, LANES:2 * LANES]
                    c2 = sp[:, 2 * LANES:2 * LANES + w2]
                    if not has_past and j * lq < WINDOW:
                        n_bad = WINDOW - j * lq
                        bad = jnp.logical_and(first, lane_p < n_bad)
                        c0 = jnp.where(bad, NEG_INF, c0)
                        c1 = jnp.where(bad, NEG_INF, c1)
                    sink_a = sinks_ref[4 * h + 2 * p]
                    sink_b = sinks_ref[4 * h + 2 * p + 1]
                    ma = jnp.maximum(jnp.max(c0, axis=1, keepdims=True), sink_a)
                    ma = jnp.maximum(ma, jnp.max(jnp.where(is_a2, c2, NEG_INF), axis=1, keepdims=True))
                    mb = jnp.maximum(jnp.max(c1, axis=1, keepdims=True), sink_b)
                    mb = jnp.maximum(mb, jnp.max(jnp.where(is_a2, NEG_INF, c2), axis=1, keepdims=True))
                    e0 = jnp.exp(c0 - ma)
                    e1 = jnp.exp(c1 - mb)
                    e2 = jnp.exp(c2 - jnp.where(is_a2, ma, mb))
                    da = (jnp.sum(e0, axis=1, keepdims=True) + jnp.sum(jnp.where(is_a2, e2, 0.0), axis=1, keepdims=True)
                          + jnp.exp(sink_a - ma))
                    db = (jnp.sum(e1, axis=1, keepdims=True) + jnp.sum(jnp.where(is_a2, 0.0, e2), axis=1, keepdims=True)
                          + jnp.exp(sink_b - mb))
                    probs.append(jnp.concatenate([e0, e1, e2], axis=1).astype(_BF16))
                    scales.append(jnp.where(lane_p < HEAD_DIM, 1.0 / da, 1.0 / db))
                ov = _dot(jnp.concatenate(probs, axis=0), vmat)
                for p in range(2):
                    oat_ref[q0:q0 + lq, (2 * h + p) * LANES:(2 * h + p + 1) * LANES] = (
                        ov[p * lq:(p + 1) * lq] * scales[p]).astype(_BF16)

    if lt >= WINDOW:
        @pl.when(s_idx + 1 < n_s)
        def _carry_kv():
            for bi in range(nb):
                for h in range(N_KV_HEADS):
                    for v in range(2):
                        kw_ref[bi, h, v, 0:WINDOW, :] = kw_ref[bi, h, v, lt:lt + WINDOW, :]
                        vw_ref[bi, h, v, 0:WINDOW, :] = vw_ref[bi, h, v, lt:lt + WINDOW, :]

    xr = _dot(xb, w_in_ref[:, OFF_XR:OFF_YR])
    for bi in range(nb):
        r0 = bi * lt
        xp_ref[bi, CONV_HDR:CONV_HDR + lt, :] = xr[r0:r0 + lt]
        acc = xp_ref[bi, CONV_HDR - 3:CONV_HDR - 3 + lt, :] * convw_ref[0:1, :]
        for t in range(1, CONV_W):
            acc = acc + xp_ref[bi, CONV_HDR - 3 + t:CONV_HDR - 3 + t + lt, :] * convw_ref[t:t + 1, :]
        xc_ref[r0:r0 + lt, :] = acc + convb_ref[...]
        tail = xp_ref[bi, CONV_HDR + lt - (CONV_W - 1):CONV_HDR + lt, :]
        cn_ref[bi] = tail
        xp_ref[bi, CONV_HDR - (CONV_W - 1):CONV_HDR, :] = tail

    xc = xc_ref[...]
    gates = _dot(xc.astype(_BF16), wg_ref[...]) + bg_ref[...]
    r_gate = jax.nn.sigmoid(gates[:, 0:D_RNN])
    i_gate = jax.nn.sigmoid(gates[:, D_RNN:2 * D_RNN])
    lam = lam_ref[...]
    softplus_neg = jnp.maximum(-lam, 0.0) + jnp.log(1.0 + jnp.exp(-jnp.abs(lam)))
    log_a = (-LRU_C * r_gate) * softplus_neg
    a_ref[...] = jnp.exp(log_a)
    th = jnp.tanh(log_a)
    b_ref[...] = jnp.sqrt(-2.0 * th / (1.0 - th)) * i_gate * xc

    for bi in range(nb):
        r0 = bi * lt

        def scan_step(t, hprev, r0=r0):
            hcur = a_ref[pl.ds(r0 + t, 1), :] * hprev + b_ref[pl.ds(r0 + t, 1), :]
            b_ref[pl.ds(r0 + t, 1), :] = hcur
            return hcur

        h_last = lax.fori_loop(0, lt, scan_step, hc_ref[bi], unroll=8)
        hc_ref[bi] = h_last
        hn_ref[bi] = h_last

    yr = _dot(xb, w_in_ref[:, OFF_YR:OFF_GA])
    rnn = b_ref[...] * jax.nn.gelu(yr)

    ga = _dot(xb, w_in_ref[:, OFF_GA:OFF_GR])
    gr = _dot(xb, w_in_ref[:, OFF_GR:IN_DIM])
    merged = (jax.nn.sigmoid(ga) * _dot(oat_ref[...], wao_ref[...])
              + jax.nn.sigmoid(gr) * _dot(rnn.astype(_BF16), wro_ref[...]))
    mix = _dot(merged.astype(_BF16), wout_ref[...])
    x1 = _layer_norm(DN_ALPHA * x + mix, g1_ref[...], b1_ref[...])
    for c in range(SLAB):
        x1s_ref[pl.ds(c, rows, stride=SLAB), :] = x1[:, c * LANES:(c + 1) * LANES]

    logits = _dot(x1.astype(_BF16), rw_ref[...]) + rb_ref[...]
    e_iota = lax.broadcasted_iota(jnp.int32, logits.shape, 1)
    work = logits
    top_vals = []
    top_idx = []
    top_sel = []
    for _ in range(TOP_K):
        m = jnp.max(work, axis=1, keepdims=True)
        idx = jnp.min(jnp.where(work == m, e_iota, N_EXPERTS), axis=1, keepdims=True)
        sel = e_iota == idx
        top_vals.append(m)
        top_idx.append(idx)
        top_sel.append(sel)
        work = jnp.where(sel, -jnp.inf, work)
    exps = [jnp.exp(v - top_vals[0]) for v in top_vals]
    denom = exps[0]
    for e in exps[1:]:
        denom = denom + e

    onehot = jnp.zeros_like(logits)
    for sel in top_sel:
        onehot = onehot + jnp.where(sel, 1.0, 0.0)
    r_i = lax.broadcasted_iota(jnp.int32, (rows, rows), 0)
    c_i = lax.broadcasted_iota(jnp.int32, (rows, rows), 1)
    earlier = jnp.where(c_i < r_i, 1.0, 0.0).astype(_BF16)
    before = _dot(earlier, onehot.astype(_BF16)) + run_ref[...]
    run_ref[...] = run_ref[...] + jnp.sum(onehot, axis=0, keepdims=True)
    cnt_ref[...] = run_ref[...]

    lane_m = lax.broadcasted_iota(jnp.int32, (rows, LANES), 1)
    meta_i = jnp.zeros((rows, LANES), jnp.int32)
    meta_g = jnp.zeros((rows, LANES), _F32)
    for k in range(TOP_K):
        rank_k = jnp.sum(jnp.where(top_sel[k], before, 0.0), axis=1, keepdims=True).astype(jnp.int32)
        meta_i = jnp.where(lane_m == k, top_idx[k], meta_i)
        meta_i = jnp.where(lane_m == TOP_K + k, rank_k, meta_i)
        meta_g = jnp.where(lane_m == k, exps[k] / denom, meta_g)
    meta_ref[...] = meta_i
    gate_ref[...] = meta_g


def _const_spec(shape):
    nd = len(shape)
    return pl.BlockSpec(shape, lambda b, s, _nd=nd: (0,) * _nd, pipeline_mode=pl.Buffered(1))


def _mixer(x, rope, k0, v0, c0, h0, sinks, prm, *, has_past, nb, lt, lq):
    bsz, seq, _ = x.shape
    assert bsz % nb == 0 and seq % lt == 0 and lt % lq == 0 and lq % 32 == 0
    assert lt >= WINDOW or seq == lt
    n_s = seq // lt
    rows = nb * lt
    n_tok = bsz * seq
    kernel = functools.partial(_mixer_kernel, has_past, nb, lt, lq)
    batch_spec = lambda shape: pl.BlockSpec((nb,) + shape, lambda b, s: (b,) + (0,) * len(shape))
    in_specs = [
        pl.BlockSpec((nb, lt, D_MODEL), lambda b, s: (b, s, 0)),
        pl.BlockSpec((3, lt, LANES), lambda b, s: (0, s, 0)),
        _const_spec((D_MODEL, IN_DIM)),
        batch_spec((WINDOW, KV_DIM)), batch_spec((WINDOW, KV_DIM)),
        batch_spec((CONV_W - 1, D_RNN)), batch_spec((1, D_RNN)),
        pl.BlockSpec(memory_space=pltpu.SMEM),
        _const_spec((CONV_W, D_RNN)), _const_spec((1, D_RNN)),
        _const_spec((D_RNN, 2 * D_RNN)), _const_spec((1, 2 * D_RNN)), _const_spec((1, D_RNN)),
        _const_spec((Q_DIM, D_MODEL)), _const_spec((D_RNN, D_MODEL)), _const_spec((D_MODEL, D_MODEL)),
        _const_spec((1, D_MODEL)), _const_spec((1, D_MODEL)),
        _const_spec((D_MODEL, N_EXPERTS)), _const_spec((1, N_EXPERTS)),
    ]
    tok_map = lambda b, s: (b * n_s + s, 0)
    out_shape = (
        jax.ShapeDtypeStruct((n_tok * SLAB, LANES), _F32),
        jax.ShapeDtypeStruct((n_tok, LANES), jnp.int32),
        jax.ShapeDtypeStruct((n_tok, LANES), _F32),
        jax.ShapeDtypeStruct((1, N_EXPERTS), _F32),
        jax.ShapeDtypeStruct((bsz, WINDOW, KV_DIM), _F32),
        jax.ShapeDtypeStruct((bsz, WINDOW, KV_DIM), _F32),
        jax.ShapeDtypeStruct((bsz, CONV_W - 1, D_RNN), _F32),
        jax.ShapeDtypeStruct((bsz, 1, D_RNN), _F32),
    )
    out_specs = (
        pl.BlockSpec((rows * SLAB, LANES), tok_map),
        pl.BlockSpec((rows, LANES), tok_map),
        pl.BlockSpec((rows, LANES), tok_map),
        pl.BlockSpec((1, N_EXPERTS), lambda b, s: (0, 0)),
        batch_spec((WINDOW, KV_DIM)), batch_spec((WINDOW, KV_DIM)),
        batch_spec((CONV_W - 1, D_RNN)), batch_spec((1, D_RNN)),
    )
    scratch = [
        pltpu.VMEM((nb, N_KV_HEADS, 2, WINDOW + lt, LANES), _BF16),
        pltpu.VMEM((nb, N_KV_HEADS, 2, WINDOW + lt, LANES), _BF16),
        pltpu.VMEM((nb, CONV_HDR + lt, D_RNN), _F32),
        pltpu.VMEM((rows, D_RNN), _F32),
        pltpu.VMEM((rows, D_RNN), _F32),
        pltpu.VMEM((rows, D_RNN), _F32),
        pltpu.VMEM((nb, 1, D_RNN), _F32),
        pltpu.VMEM((rows, Q_DIM), _BF16),
        pltpu.VMEM((1, N_EXPERTS), _F32),
    ]
    args = [x, rope, prm["w_in"], k0, v0, c0, h0, sinks, prm["conv_w"], prm["conv_b"], prm["wg"], prm["bg"],
            prm["lam"], prm["wao"], prm["wro"], prm["wout"], prm["g1"], prm["b1"], prm["rw"], prm["rb"]]
    return pl.pallas_call(
        kernel,
        grid=(bsz // nb, n_s),
        in_specs=in_specs,
        out_specs=out_specs,
        out_shape=out_shape,
        scratch_shapes=scratch,
        compiler_params=pltpu.CompilerParams(
            dimension_semantics=("arbitrary", "arbitrary"), vmem_limit_bytes=VMEM_LIMIT_BYTES),
        name="mixer_past" if has_past else "mixer_prompt",
    )(*args)


def _slab_rows(ref, first_tok, n):
    return jnp.concatenate([ref[pl.ds(first_tok * SLAB + c, n, stride=SLAB), :] for c in range(SLAB)], axis=1)


def _dispatch_kernel(tt, tm, n_first, n_tiles, zt_ref, nu_ref, pos_ref, xa_ref, xb_ref, xs_hbm, zero_ref, sem):
    i = pl.program_id(0)

    @pl.when(i == 0)
    def _zero_padding():
        zero_ref[...] = jnp.zeros_like(zero_ref)

        def clear(tile):
            dst = xs_hbm.at[pl.ds(pl.multiple_of(tile * (tm * SLAB), SLAB), tm * SLAB)]
            cp = pltpu.make_async_copy(zero_ref, dst, sem)
            cp.start()
            cp.wait()

        for e in range(N_EXPERTS):
            pl.when(zt_ref[e] >= 0)(functools.partial(clear, zt_ref[e]))

        def clear_unused(tile, carry):
            clear(tile)
            return carry

        lax.fori_loop(nu_ref[0], n_tiles, clear_unused, 0)

    def scatter(src_ref):
        def issue(r, carry):
            src = src_ref.at[pl.ds(pl.multiple_of(r * SLAB, SLAB), SLAB)]
            for k in range(TOP_K):
                dst = xs_hbm.at[pl.ds(pl.multiple_of(pos_ref[0, 0, r * TOP_K + k] * SLAB, SLAB), SLAB)]
                pltpu.make_async_copy(src, dst, sem).start()
            return carry

        lax.fori_loop(0, tt, issue, 0)

    pl.when(i < n_first)(lambda: scatter(xa_ref))
    pl.when(i >= n_first)(lambda: scatter(xb_ref))
    n_rows = tt * TOP_K * SLAB
    pltpu.make_async_copy(xs_hbm.at[pl.ds(0, n_rows)], xs_hbm.at[pl.ds(0, n_rows)], sem).wait()


def _dispatch(x1s_a, x1s_b, pos, zero_tiles, n_used, *, tt, tm, n_tiles):
    n_a = x1s_a.shape[0] // SLAB
    n_b = x1s_b.shape[0] // SLAB
    assert n_a % tt == 0 and n_b % tt == 0 and n_tiles * tm >= tt * TOP_K
    n_first = n_a // tt
    n_steps = (n_a + n_b) // tt
    return pl.pallas_call(
        functools.partial(_dispatch_kernel, tt, tm, n_first, n_tiles),
        grid_spec=pltpu.PrefetchScalarGridSpec(
            num_scalar_prefetch=2,
            grid=(n_steps,),
            in_specs=[
                pl.BlockSpec((1, 1, tt * TOP_K), lambda i, zt, nu: (i, 0, 0), memory_space=pltpu.SMEM),
                pl.BlockSpec((tt * SLAB, LANES), lambda i, zt, nu: (jnp.minimum(i, n_first - 1), 0)),
                pl.BlockSpec((tt * SLAB, LANES), lambda i, zt, nu: (jnp.maximum(i - n_first, 0), 0)),
            ],
            out_specs=pl.BlockSpec(memory_space=pl.ANY),
            scratch_shapes=[pltpu.VMEM((tm * SLAB, LANES), _F32), pltpu.SemaphoreType.DMA],
        ),
        out_shape=jax.ShapeDtypeStruct((n_tiles * tm * SLAB, LANES), _F32),
        compiler_params=pltpu.CompilerParams(dimension_semantics=("arbitrary",)),
        name="moe_dispatch",
    )(zero_tiles, n_used, pos.reshape(n_steps, 1, tt * TOP_K), x1s_a, x1s_b)


def _expert_kernel(tm, te_ref, nu_ref, xs_ref, wgu_ref, bgu_ref, wd_ref, bd_ref, ys_ref, wgu_b, wd_b):
    i = pl.program_id(0)

    @pl.when(i < nu_ref[0])
    def _tile():
        prev_e = te_ref[jnp.maximum(i - 1, 0)]

        @pl.when(jnp.logical_or(i == 0, te_ref[i] != prev_e))
        def _new_expert():
            wgu_b[...] = wgu_ref[0].astype(_BF16)
            wd_b[...] = wd_ref[0].astype(_BF16)

        x = _slab_rows(xs_ref, 0, tm).astype(_BF16)
        hgu = _dot(x, wgu_b[...]) + bgu_ref[0]
        glu = jnp.minimum(hgu[:, 0:D_FF], SWIGLU_LIMIT)
        lin = jnp.clip(hgu[:, D_FF:2 * D_FF], -SWIGLU_LIMIT, SWIGLU_LIMIT)
        hh = glu * jax.nn.sigmoid(SWIGLU_ALPHA * glu) * (lin + 1.0)
        y = _dot(hh.astype(_BF16), wd_b[...]) + bd_ref[0]
        for c in range(SLAB):
            ys_ref[pl.ds(c, tm, stride=SLAB), :] = y[:, c * LANES:(c + 1) * LANES]

    @pl.when(i >= nu_ref[0])
    def _unused_tile():
        ys_ref[...] = jnp.zeros_like(ys_ref)


def _experts(xs, tile_expert, n_used, prm, *, tm):
    n_tiles = xs.shape[0] // (tm * SLAB)
    row_map = lambda i, te, nu: (jnp.minimum(i, nu[0] - 1), 0)
    out_map = lambda i, te, nu: (i, 0)
    exp_map = lambda i, te, nu: (te[i], 0, 0)
    return pl.pallas_call(
        functools.partial(_expert_kernel, tm),
        grid_spec=pltpu.PrefetchScalarGridSpec(
            num_scalar_prefetch=2,
            grid=(n_tiles,),
            in_specs=[
                pl.BlockSpec((tm * SLAB, LANES), row_map),
                pl.BlockSpec((1, D_MODEL, 2 * D_FF), exp_map),
                pl.BlockSpec((1, 1, 2 * D_FF), exp_map),
                pl.BlockSpec((1, D_FF, D_MODEL), exp_map),
                pl.BlockSpec((1, 1, D_MODEL), exp_map),
            ],
            out_specs=pl.BlockSpec((tm * SLAB, LANES), out_map),
            scratch_shapes=[pltpu.VMEM((D_MODEL, 2 * D_FF), _BF16), pltpu.VMEM((D_FF, D_MODEL), _BF16)],
        ),
        out_shape=jax.ShapeDtypeStruct(xs.shape, _F32),
        compiler_params=pltpu.CompilerParams(
            dimension_semantics=("arbitrary",), vmem_limit_bytes=VMEM_LIMIT_BYTES),
        name="moe_experts",
    )(tile_expert, n_used, xs, prm["wgu"], prm["bgu"], prm["wd"], prm["bd"])


def _combine_kernel(tt, n_first, pos_ref, gate_ref, xa_ref, xb_ref, ys_hbm, g2_ref, b2_ref, ya_ref, yb_ref, ybuf, sem):
    i = pl.program_id(0)

    def issue(r, carry):
        for k in range(TOP_K):
            src = ys_hbm.at[pl.ds(pl.multiple_of(pos_ref[0, 0, r * TOP_K + k] * SLAB, SLAB), SLAB)]
            dst = ybuf.at[pl.ds(pl.multiple_of((k * tt + r) * SLAB, SLAB), SLAB)]
            pltpu.make_async_copy(src, dst, sem).start()
        return carry

    lax.fori_loop(0, tt, issue, 0)
    n_rows = tt * TOP_K * SLAB
    pltpu.make_async_copy(ys_hbm.at[pl.ds(0, n_rows)], ybuf, sem).wait()

    gates = gate_ref[...]
    acc = gates[:, 0:1] * _slab_rows(ybuf, 0, tt)
    for k in range(1, TOP_K):
        acc = acc + gates[:, k:k + 1] * _slab_rows(ybuf, k * tt, tt)

    def finish(x_ref, y_ref):
        x1 = _slab_rows(x_ref, 0, tt)
        y_ref[...] = _layer_norm(DN_ALPHA * x1 + acc, g2_ref[...], b2_ref[...])

    pl.when(i < n_first)(lambda: finish(xa_ref, ya_ref))
    pl.when(i >= n_first)(lambda: finish(xb_ref, yb_ref))


def _combine(x1s_a, x1s_b, ys, pos, gates, prm, *, tt):
    n_a = x1s_a.shape[0] // SLAB
    n_b = x1s_b.shape[0] // SLAB
    assert n_a % tt == 0 and n_b % tt == 0
    n_first = n_a // tt
    n_steps = (n_a + n_b) // tt
    a_map = lambda i: (jnp.minimum(i, n_first - 1), 0)
    b_map = lambda i: (jnp.maximum(i - n_first, 0), 0)
    return pl.pallas_call(
        functools.partial(_combine_kernel, tt, n_first),
        grid=(n_steps,),
        in_specs=[
            pl.BlockSpec((1, 1, tt * TOP_K), lambda i: (i, 0, 0), memory_space=pltpu.SMEM),
            pl.BlockSpec((tt, TOP_K), lambda i: (i, 0)),
            pl.BlockSpec((tt * SLAB, LANES), a_map),
            pl.BlockSpec((tt * SLAB, LANES), b_map),
            pl.BlockSpec(memory_space=pl.ANY),
            pl.BlockSpec((1, D_MODEL), lambda i: (0, 0)),
            pl.BlockSpec((1, D_MODEL), lambda i: (0, 0)),
        ],
        out_specs=(pl.BlockSpec((tt, D_MODEL), a_map), pl.BlockSpec((tt, D_MODEL), b_map)),
        out_shape=(jax.ShapeDtypeStruct((n_a, D_MODEL), _F32), jax.ShapeDtypeStruct((n_b, D_MODEL), _F32)),
        scratch_shapes=[pltpu.VMEM((TOP_K * tt * SLAB, LANES), _F32), pltpu.SemaphoreType.DMA],
        compiler_params=pltpu.CompilerParams(dimension_semantics=("arbitrary",)),
        name="moe_combine",
    )(pos.reshape(n_steps, 1, tt * TOP_K), gates, x1s_a, x1s_b, ys, prm["g2"], prm["b2"])


def _route(meta_a, meta_b, gate_a, gate_b, counts_a, counts_b, tm, n_tiles):
    meta = jnp.concatenate([meta_a[:, 0:2 * TOP_K], meta_b[:, 0:2 * TOP_K]], axis=0)
    idx = meta[:, 0:TOP_K]
    rank = meta[:, TOP_K:2 * TOP_K]
    gates = jnp.concatenate([gate_a[:, 0:TOP_K], gate_b[:, 0:TOP_K]], axis=0)
    cp = counts_a.reshape(-1).astype(jnp.int32)
    cs = counts_b.reshape(-1).astype(jnp.int32)
    is_b = (jnp.arange(meta.shape[0]) >= meta_a.shape[0])[:, None]
    rank = rank + jnp.where(is_b, cp[idx], 0)
    tiles_per = (cp + cs + tm - 1) // tm
    tile_end = jnp.cumsum(tiles_per)
    tile_start = tile_end - tiles_per
    pos = tile_start[idx] * tm + rank
    n_used = tile_end[-1:]
    tile_ids = jnp.arange(n_tiles, dtype=jnp.int32)
    tile_expert = jnp.minimum(jnp.sum(tile_ids[:, None] >= tile_end[None, :], axis=1), N_EXPERTS - 1)
    last_e = tile_expert[jnp.maximum(n_used[0] - 1, 0)]
    tile_expert = jnp.where(tile_ids < n_used[0], tile_expert, last_e).astype(jnp.int32)
    zero_tiles = jnp.where(tiles_per > 0, tile_end - 1, -1).astype(jnp.int32)
    return pos.astype(jnp.int32), gates, tile_expert, n_used.astype(jnp.int32), zero_tiles


def _rope_tables(pos):
    half = ROT_DIM // 2
    inv_freq = ROPE_THETA ** (-jnp.arange(half, dtype=_F32) / half)
    ang = pos.astype(_F32)[:, None] * inv_freq[None, :]
    cos, sin = jnp.cos(ang), jnp.sin(ang)
    n = pos.shape[0]
    ones = jnp.ones((n, HEAD_DIM - ROT_DIM), _F32)
    zeros = jnp.zeros((n, HEAD_DIM - ROT_DIM), _F32)
    zh = jnp.zeros((n, half), _F32)
    cos_t = jnp.concatenate([cos, cos, ones], axis=1)
    sin_up = jnp.concatenate([-sin, zh, zeros], axis=1)
    sin_dn = jnp.concatenate([zh, sin, zeros], axis=1)
    tab = jnp.stack([cos_t, sin_up, sin_dn])
    return jnp.concatenate([tab, tab], axis=2)


def _block_diag(w):
    hd, d, _ = w.shape
    eye = jnp.eye(hd, dtype=w.dtype)
    return (eye[:, None, :, None] * w[:, :, None, :]).reshape(hd * d, hd * d)


def kernel(x_prompt, x_sample, cache_k, cache_v, state_conv, state_h, w_in, attn_sinks, w_attn_out, conv_w, conv_b, gate_a_w, gate_a_b, gate_x_w, gate_x_b, lru_lambda, w_rnn_out, w_out, ln1_g, ln1_b, router_w, router_b, w_gate_up, b_gate_up, w_down, b_down, ln2_g, ln2_b):
    assert w_in.shape[0] == DEPTH == 1
    l = 0
    bsz, seq, _ = x_prompt.shape
    dbsz, dseq, _ = x_sample.shape
    row = lambda a: a.reshape(1, -1)
    prm = {
        "w_in": w_in[l].astype(_BF16),
        "conv_w": conv_w[l], "conv_b": row(conv_b[l]),
        "wg": jnp.concatenate([_block_diag(gate_a_w[l]), _block_diag(gate_x_w[l])], axis=1).astype(_BF16),
        "bg": row(jnp.concatenate([gate_a_b[l], gate_x_b[l]])),
        "lam": row(lru_lambda[l]),
        "wao": w_attn_out[l].astype(_BF16), "wro": w_rnn_out[l].astype(_BF16), "wout": w_out[l].astype(_BF16),
        "g1": row(ln1_g[l]), "b1": row(ln1_b[l]),
        "rw": router_w[l].astype(_BF16), "rb": row(router_b[l]),
        "wgu": w_gate_up[l], "bgu": b_gate_up[l][:, None, :],
        "wd": w_down[l], "bd": b_down[l][:, None, :],
        "g2": row(ln2_g[l]), "b2": row(ln2_b[l]),
    }
    sinks = attn_sinks[l]

    rope_p = _rope_tables(jnp.arange(seq, dtype=jnp.int32))
    rope_s = _rope_tables(PAST_LEN + jnp.arange(dseq, dtype=jnp.int32))

    n_prompt = bsz * seq
    n_tok = n_prompt + dbsz * dseq
    zeros_kv = jnp.zeros((bsz, WINDOW, KV_DIM), _F32)
    x1_p, meta_p, gate_p, cnt_p, pk, pv, pc, ph = _mixer(
        x_prompt, rope_p, zeros_kv, zeros_kv, jnp.zeros((bsz, CONV_W - 1, D_RNN), _F32),
        jnp.zeros((bsz, 1, D_RNN), _F32), sinks, prm, has_past=False, nb=1, lt=min(MIXER_ROWS, seq), lq=CHUNK)
    x1_s, meta_s, gate_s, cnt_s, sk, sv, sc, sh = _mixer(
        x_sample, rope_s, cache_k[l].reshape(dbsz, WINDOW, KV_DIM), cache_v[l].reshape(dbsz, WINDOW, KV_DIM),
        state_conv[l], state_h[l][:, None, :], sinks, prm, has_past=True, nb=min(MIXER_ROWS // dseq, dbsz),
        lt=dseq, lq=dseq)

    tm = EXPERT_ROWS
    n_tiles = (n_tok * TOP_K + N_EXPERTS * (tm - 1)) // tm + 1
    pos, gates, tile_expert, n_used, zero_tiles = _route(meta_p, meta_s, gate_p, gate_s, cnt_p, cnt_s, tm, n_tiles)
    xs = _dispatch(x1_p, x1_s, pos, zero_tiles, n_used, tt=TOKEN_ROWS, tm=tm, n_tiles=n_tiles)
    ysort = _experts(xs, tile_expert, n_used, prm, tm=tm)
    yp, ys = _combine(x1_p, x1_s, ysort, pos, gates, prm, tt=TOKEN_ROWS)
    yp = yp.reshape(bsz, seq, D_MODEL)
    ys = ys.reshape(dbsz, dseq, D_MODEL)

    kv5 = lambda a: a.reshape(1, a.shape[0], WINDOW, N_KV_HEADS, HEAD_DIM)
    return (yp, ys, kv5(pk), kv5(pv), pc[None], ph.reshape(1, bsz, D_RNN),
            kv5(sk), kv5(sv), sc[None], sh.reshape(1, dbsz, D_RNN))
```

```python
import functools

import jax
import jax.numpy as jnp
import numpy as np
from jax import lax
from jax.experimental import pallas as pl
from jax.experimental.pallas import tpu as pltpu

D_MODEL = 1024
PAST_LEN = 1024
CHUNK = 64
N_HEADS = 16
N_KV_HEADS = 4
HEAD_DIM = 64
GROUP = N_HEADS // N_KV_HEADS
ROT_DIM = HEAD_DIM // 4
ROPE_THETA = 500000.0
WINDOW = 128
ATTN_SCALE = HEAD_DIM ** -0.5
NEG_INF = -1e30
D_RNN = 1280
RNN_HEADS = 16
RNN_HEAD_DIM = D_RNN // RNN_HEADS
CONV_W = 4
LRU_C = 8.0
N_EXPERTS = 32
TOP_K = 4
D_FF = 1024
SWIGLU_ALPHA = 1.702
SWIGLU_LIMIT = 7.0
LN_EPS = 1e-5
DEPTH = 1
DN_ALPHA = (2.0 * DEPTH) ** 0.25

Q_DIM = N_HEADS * HEAD_DIM
KV_DIM = N_KV_HEADS * HEAD_DIM
OFF_K = Q_DIM
OFF_V = OFF_K + KV_DIM
OFF_XR = OFF_V + KV_DIM
OFF_YR = OFF_XR + D_RNN
OFF_GA = OFF_YR + D_RNN
OFF_GR = OFF_GA + D_MODEL
IN_DIM = OFF_GR + D_MODEL

LANES = 128
SLAB = D_MODEL // LANES
MIXER_ROWS = 256
EXPERT_ROWS = 256
TOKEN_ROWS = 256
SCAN_S = 8
GATE_COLS = 256
GATE_K = 512
GATE_STARTS = tuple(min(max((j * GATE_COLS // RNN_HEAD_DIM) * RNN_HEAD_DIM // LANES * LANES, 0), D_RNN - GATE_K)
                    for j in range(D_RNN // GATE_COLS))
CONV_HDR = 8
VMEM_LIMIT_BYTES = 56 * 1024 * 1024

_BF16 = jnp.bfloat16
_F32 = jnp.float32


def _dot(a, b):
    return jnp.dot(a, b, preferred_element_type=_F32)


def _dot_nt(a, b):
    return lax.dot_general(a, b, (((1,), (1,)), ((), ())), preferred_element_type=_F32)


def _sigmoid(x):
    return 0.5 * jnp.tanh(0.5 * x) + 0.5


def _layer_norm(x, g, b):
    mu = jnp.mean(x, axis=-1, keepdims=True)
    xc = x - mu
    var = jnp.mean(xc * xc, axis=-1, keepdims=True)
    return xc * lax.rsqrt(var + LN_EPS) * g + b


def _rope(x, cos_t, sin_up, sin_dn):
    return x * cos_t + pltpu.roll(x, LANES - ROT_DIM // 2, 1) * sin_up + pltpu.roll(x, ROT_DIM // 2, 1) * sin_dn


def _lo_hi(slab, h):
    blk = slab[:, (h // 2) * LANES:(h // 2 + 1) * LANES]
    lane = lax.broadcasted_iota(jnp.int32, blk.shape, 1)
    if h % 2 == 0:
        lo = jnp.where(lane < HEAD_DIM, blk, 0.0)
        hi = pltpu.roll(lo, HEAD_DIM, 1)
    else:
        hi = jnp.where(lane >= HEAD_DIM, blk, 0.0)
        lo = pltpu.roll(hi, HEAD_DIM, 1)
    return lo.astype(_BF16), hi.astype(_BF16)


def _mixer_kernel(has_past, nb, lt, lq,
                  x_ref, rope_ref, w_in_ref, k0_ref, v0_ref, c0_ref, h0_ref, sinkrows_ref,
                  convw_ref, convb_ref, wg_ref, bg_ref, lam_ref, wao_ref, wro_ref, wout_ref,
                  g1_ref, b1_ref, rwt_ref, rb_ref,
                  x1s_ref, meta_ref, gate_ref, cnt_ref, kn_ref, vn_ref, cn_ref, hn_ref,
                  kw_ref, vw_ref, xp_ref, xc_ref, a_ref, b_ref, hin_ref, hc_ref, oat_ref, run_ref):
    s_idx = pl.program_id(1)
    n_s = pl.num_programs(1)
    rows = nb * lt
    n_chunks = lt // lq

    @pl.when(jnp.logical_and(pl.program_id(0) == 0, s_idx == 0))
    def _init_counts():
        run_ref[...] = jnp.zeros_like(run_ref)

    @pl.when(s_idx == 0)
    def _init():
        for bi in range(nb):
            for h in range(N_KV_HEADS):
                klo, khi = _lo_hi(k0_ref[bi], h)
                vlo, vhi = _lo_hi(v0_ref[bi], h)
                kw_ref[bi, h, 0, 0:WINDOW, :] = klo
                kw_ref[bi, h, 1, 0:WINDOW, :] = khi
                vw_ref[bi, h, 0, 0:WINDOW, :] = vlo
                vw_ref[bi, h, 1, 0:WINDOW, :] = vhi
            xp_ref[bi, CONV_HDR - (CONV_W - 1):CONV_HDR, :] = c0_ref[bi]
            hc_ref[bi] = h0_ref[bi]

    x = x_ref[...].reshape(rows, D_MODEL)
    xb = x.astype(_BF16)

    cos_t = jnp.concatenate([rope_ref[0]] * nb, axis=0) if nb > 1 else rope_ref[0]
    sin_up = jnp.concatenate([rope_ref[1]] * nb, axis=0) if nb > 1 else rope_ref[1]
    sin_dn = jnp.concatenate([rope_ref[2]] * nb, axis=0) if nb > 1 else rope_ref[2]

    zq = _dot(xb, w_in_ref[:, 0:Q_DIM])
    q_blocks = []
    for c in range(Q_DIM // LANES):
        qr = _rope(zq[:, c * LANES:(c + 1) * LANES], cos_t, sin_up, sin_dn)
        q_blocks.append((qr * ATTN_SCALE).astype(_BF16))
    zk = _dot(xb, w_in_ref[:, OFF_K:OFF_V])
    k_rot = jnp.concatenate(
        [_rope(zk[:, c * LANES:(c + 1) * LANES], cos_t, sin_up, sin_dn) for c in range(KV_DIM // LANES)], axis=1)
    v_new = _dot(xb, w_in_ref[:, OFF_V:OFF_XR])

    for bi in range(nb):
        r0 = bi * lt
        for h in range(N_KV_HEADS):
            klo, khi = _lo_hi(k_rot[r0:r0 + lt], h)
            vlo, vhi = _lo_hi(v_new[r0:r0 + lt], h)
            kw_ref[bi, h, 0, WINDOW:WINDOW + lt, :] = klo
            kw_ref[bi, h, 1, WINDOW:WINDOW + lt, :] = khi
            vw_ref[bi, h, 0, WINDOW:WINDOW + lt, :] = vlo
            vw_ref[bi, h, 1, WINDOW:WINDOW + lt, :] = vhi

    for bi in range(nb):
        r0 = bi * lt
        if lt >= WINDOW:
            kn_ref[bi] = k_rot[r0 + lt - WINDOW:r0 + lt]
            vn_ref[bi] = v_new[r0 + lt - WINDOW:r0 + lt]
        else:
            kn_ref[bi, 0:WINDOW - lt, :] = k0_ref[bi, lt:WINDOW, :]
            kn_ref[bi, WINDOW - lt:WINDOW, :] = k_rot[r0:r0 + lt]
            vn_ref[bi, 0:WINDOW - lt, :] = v0_ref[bi, lt:WINDOW, :]
            vn_ref[bi, WINDOW - lt:WINDOW, :] = v_new[r0:r0 + lt]

    w2 = 2 * lq
    wk = 2 * WINDOW + w2
    r_all = N_KV_HEADS * 2 * lq
    is_a2 = lax.broadcasted_iota(jnp.int32, (r_all, w2), 1) < lq
    lane_lo = lax.broadcasted_iota(jnp.int32, (r_all, LANES), 1) < HEAD_DIM
    key_row = lax.broadcasted_iota(jnp.int32, (wk, LANES), 0)
    key_is_a = jnp.logical_or(key_row < WINDOW, jnp.logical_and(key_row >= 2 * WINDOW, key_row < 2 * WINDOW + lq))
    key_lane_lo = lax.broadcasted_iota(jnp.int32, (wk, LANES), 1) < HEAD_DIM
    head_ones = jnp.where(key_is_a == key_lane_lo, 1.0, 0.0).astype(_BF16)
    sink_rows = sinkrows_ref[...]
    sink_a = sink_rows[:, 0:1]
    sink_b = sink_rows[:, HEAD_DIM:HEAD_DIM + 1]
    first = s_idx == 0
    for bi in range(nb):
        for j in range(n_chunks):
            p0 = j * lq
            o0 = WINDOW + j * lq
            q0 = bi * lt + j * lq
            scores = []
            vmats = []
            for h in range(N_KV_HEADS):
                kmat = jnp.concatenate([kw_ref[bi, h, 0, p0:p0 + WINDOW, :], kw_ref[bi, h, 1, p0:p0 + WINDOW, :],
                                        kw_ref[bi, h, 0, o0:o0 + lq, :], kw_ref[bi, h, 1, o0:o0 + lq, :]], axis=0)
                vmat = jnp.concatenate([vw_ref[bi, h, 0, p0:p0 + WINDOW, :], vw_ref[bi, h, 1, p0:p0 + WINDOW, :],
                                        vw_ref[bi, h, 0, o0:o0 + lq, :], vw_ref[bi, h, 1, o0:o0 + lq, :]], axis=0)
                vmats.append(jnp.concatenate([vmat, head_ones], axis=1))
                qs = jnp.concatenate([q_blocks[2 * h][q0:q0 + lq], q_blocks[2 * h + 1][q0:q0 + lq]], axis=0)
                scores.append(_dot_nt(qs, kmat))
            sc = jnp.concatenate(scores, axis=0)
            c0 = sc[:, 0:LANES]
            c1 = sc[:, LANES:2 * LANES]
            c2 = sc[:, 2 * LANES:wk]
            if not has_past and j * lq < WINDOW:
                bad = jnp.logical_and(first, lax.broadcasted_iota(jnp.int32, c0.shape, 1) < WINDOW - j * lq)
                c0 = jnp.where(bad, NEG_INF, c0)
                c1 = jnp.where(bad, NEG_INF, c1)
            if w2 == LANES:
                ma = jnp.max(jnp.maximum(c0, jnp.where(is_a2, c2, NEG_INF)), axis=1, keepdims=True)
                mb = jnp.max(jnp.maximum(c1, jnp.where(is_a2, NEG_INF, c2)), axis=1, keepdims=True)
            else:
                ma = jnp.maximum(jnp.max(c0, axis=1, keepdims=True),
                                 jnp.max(jnp.where(is_a2, c2, NEG_INF), axis=1, keepdims=True))
                mb = jnp.maximum(jnp.max(c1, axis=1, keepdims=True),
                                 jnp.max(jnp.where(is_a2, NEG_INF, c2), axis=1, keepdims=True))
            ma = jnp.maximum(ma, sink_a)
            mb = jnp.maximum(mb, sink_b)
            probs = jnp.concatenate([jnp.exp(c0 - ma), jnp.exp(c1 - mb), jnp.exp(c2 - jnp.where(is_a2, ma, mb))],
                                    axis=1).astype(_BF16)
            sink_term = jnp.exp(sink_rows - jnp.where(lane_lo, ma, mb))
            for h in range(N_KV_HEADS):
                rs = slice(h * 2 * lq, (h + 1) * 2 * lq)
                od = _dot(probs[rs], vmats[h])
                out = od[:, 0:LANES] / (od[:, LANES:2 * LANES] + sink_term[rs])
                for p in range(2):
                    oat_ref[q0:q0 + lq, (2 * h + p) * LANES:(2 * h + p + 1) * LANES] = (
                        out[p * lq:(p + 1) * lq].astype(_BF16))

    if lt >= WINDOW:
        @pl.when(s_idx + 1 < n_s)
        def _carry_kv():
            for bi in range(nb):
                for h in range(N_KV_HEADS):
                    for v in range(2):
                        kw_ref[bi, h, v, 0:WINDOW, :] = kw_ref[bi, h, v, lt:lt + WINDOW, :]
                        vw_ref[bi, h, v, 0:WINDOW, :] = vw_ref[bi, h, v, lt:lt + WINDOW, :]

    xr = _dot(xb, w_in_ref[:, OFF_XR:OFF_YR])
    for bi in range(nb):
        r0 = bi * lt
        xp_ref[bi, CONV_HDR:CONV_HDR + lt, :] = xr[r0:r0 + lt]
        acc = xp_ref[bi, CONV_HDR - 3:CONV_HDR - 3 + lt, :] * convw_ref[0:1, :]
        for t in range(1, CONV_W):
            acc = acc + xp_ref[bi, CONV_HDR - 3 + t:CONV_HDR - 3 + t + lt, :] * convw_ref[t:t + 1, :]
        xc_ref[r0:r0 + lt, :] = acc + convb_ref[...]
        tail = xp_ref[bi, CONV_HDR + lt - (CONV_W - 1):CONV_HDR + lt, :]
        cn_ref[bi] = tail
        xp_ref[bi, CONV_HDR - (CONV_W - 1):CONV_HDR, :] = tail

    xc = xc_ref[...]
    xcb = xc.astype(_BF16)
    lam = lam_ref[...]
    softplus_neg = jnp.maximum(-lam, 0.0) + jnp.log(1.0 + jnp.exp(-jnp.abs(lam)))
    for j in range(D_RNN // GATE_COLS):
        cs = slice(j * GATE_COLS, (j + 1) * GATE_COLS)
        g = _dot(xcb[:, GATE_STARTS[j]:GATE_STARTS[j] + GATE_K], wg_ref[j]) + bg_ref[j]
        r_gate = _sigmoid(g[:, 0:GATE_COLS])
        i_gate = _sigmoid(g[:, GATE_COLS:2 * GATE_COLS])
        log_a = (-LRU_C * r_gate) * softplus_neg[:, cs]
        a = jnp.exp(log_a)
        bv = jnp.sqrt(-jnp.tanh(log_a) * (1.0 + a * a)) * i_gate * xc[:, cs]
        for cc in range(GATE_COLS // LANES):
            c = j * (GATE_COLS // LANES) + cc
            a_ref[c] = a[:, cc * LANES:(cc + 1) * LANES]
            b_ref[c] = bv[:, cc * LANES:(cc + 1) * LANES]

    n_lb = D_RNN // LANES
    n_g = lt // SCAN_S
    for bi in range(nb):
        r0 = bi * lt
        a_tot, h_tot = [], []
        for c in range(n_lb):
            at = a_ref[c, pl.ds(r0, n_g, stride=SCAN_S), :]
            ht = b_ref[c, pl.ds(r0, n_g, stride=SCAN_S), :]
            for k in range(1, SCAN_S):
                ak = a_ref[c, pl.ds(r0 + k, n_g, stride=SCAN_S), :]
                ht = ak * ht + b_ref[c, pl.ds(r0 + k, n_g, stride=SCAN_S), :]
                at = ak * at
            a_tot.append(at)
            h_tot.append(ht)
        carry = [hc_ref[bi][:, c * LANES:(c + 1) * LANES] for c in range(n_lb)]
        for g in range(n_g):
            for c in range(n_lb):
                hin_ref[c, g:g + 1, :] = carry[c]
                carry[c] = a_tot[c][g:g + 1, :] * carry[c] + h_tot[c][g:g + 1, :]
        h_last = jnp.concatenate(carry, axis=1)
        hc_ref[bi] = h_last
        hn_ref[bi] = h_last
        for c in range(n_lb):
            hcur = hin_ref[c]
            for k in range(SCAN_S):
                hcur = a_ref[c, pl.ds(r0 + k, n_g, stride=SCAN_S), :] * hcur + b_ref[c, pl.ds(r0 + k, n_g, stride=SCAN_S), :]
                b_ref[c, pl.ds(r0 + k, n_g, stride=SCAN_S), :] = hcur

    yr = _dot(xb, w_in_ref[:, OFF_YR:OFF_GA])
    rnn = jnp.concatenate([b_ref[c] for c in range(n_lb)], axis=1) * jax.nn.gelu(yr)

    ga = _dot(xb, w_in_ref[:, OFF_GA:OFF_GR])
    gr = _dot(xb, w_in_ref[:, OFF_GR:IN_DIM])
    merged = (_sigmoid(ga) * _dot(oat_ref[...], wao_ref[...])
              + _sigmoid(gr) * _dot(rnn.astype(_BF16), wro_ref[...]))
    mix = _dot(merged.astype(_BF16), wout_ref[...])
    x1 = _layer_norm(DN_ALPHA * x + mix, g1_ref[...], b1_ref[...])
    for c in range(SLAB):
        x1s_ref[pl.ds(c, rows, stride=SLAB), :] = x1[:, c * LANES:(c + 1) * LANES]

    logits = _dot_nt(rwt_ref[...], x1.astype(_BF16)) + rb_ref[...]
    e_iota = lax.broadcasted_iota(jnp.int32, logits.shape, 0)
    work = logits
    top_vals = []
    top_idx = []
    top_sel = []
    for _ in range(TOP_K):
        m = jnp.max(work, axis=0, keepdims=True)
        idx = jnp.min(jnp.where(work == m, e_iota, N_EXPERTS), axis=0, keepdims=True)
        sel = e_iota == idx
        top_vals.append(m)
        top_idx.append(idx)
        top_sel.append(sel)
        work = jnp.where(sel, -jnp.inf, work)
    exps = [jnp.exp(v - top_vals[0]) for v in top_vals]
    denom = exps[0]
    for e in exps[1:]:
        denom = denom + e

    onehot = jnp.zeros_like(logits)
    for sel in top_sel:
        onehot = onehot + jnp.where(sel, 1.0, 0.0)
    r_i = lax.broadcasted_iota(jnp.int32, (rows, rows), 0)
    c_i = lax.broadcasted_iota(jnp.int32, (rows, rows), 1)
    earlier = jnp.where(r_i < c_i, 1.0, 0.0).astype(_BF16)
    before = _dot(onehot.astype(_BF16), earlier) + run_ref[...]
    run_ref[...] = run_ref[...] + jnp.sum(onehot, axis=1, keepdims=True)
    cnt_ref[...] = run_ref[...]

    sub_m = lax.broadcasted_iota(jnp.int32, (2 * TOP_K, rows), 0)
    meta_i = jnp.zeros((2 * TOP_K, rows), jnp.int32)
    meta_g = jnp.zeros((2 * TOP_K, rows), _F32)
    for k in range(TOP_K):
        rank_k = jnp.sum(jnp.where(top_sel[k], before, 0.0), axis=0, keepdims=True).astype(jnp.int32)
        meta_i = jnp.where(sub_m == k, top_idx[k], meta_i)
        meta_i = jnp.where(sub_m == TOP_K + k, rank_k, meta_i)
        meta_g = jnp.where(sub_m == k, exps[k] / denom, meta_g)
    meta_ref[...] = meta_i
    gate_ref[...] = meta_g


def _const_spec(shape):
    nd = len(shape)
    return pl.BlockSpec(shape, lambda b, s, _nd=nd: (0,) * _nd, pipeline_mode=pl.Buffered(1))


def _mixer(x, rope, k0, v0, c0, h0, sinks, prm, *, has_past, nb, lt, lq):
    bsz, seq, _ = x.shape
    assert bsz % nb == 0 and seq % lt == 0 and lt % lq == 0 and lq % 32 == 0
    assert lt >= WINDOW or seq == lt
    n_s = seq // lt
    rows = nb * lt
    n_tok = bsz * seq
    assert lt % SCAN_S == 0
    kernel = functools.partial(_mixer_kernel, has_past, nb, lt, lq)
    r_all = N_KV_HEADS * 2 * lq
    sink_rows = jnp.broadcast_to(sinks.reshape(N_KV_HEADS * 2, 1, 2, 1), (N_KV_HEADS * 2, lq, 2, HEAD_DIM))
    sink_rows = sink_rows.reshape(r_all, LANES)
    n_gate = D_RNN // GATE_COLS
    batch_spec = lambda shape: pl.BlockSpec((nb,) + shape, lambda b, s: (b,) + (0,) * len(shape))
    in_specs = [
        pl.BlockSpec((nb, lt, D_MODEL), lambda b, s: (b, s, 0)),
        pl.BlockSpec((3, lt, LANES), lambda b, s: (0, s, 0)),
        _const_spec((D_MODEL, IN_DIM)),
        batch_spec((WINDOW, KV_DIM)), batch_spec((WINDOW, KV_DIM)),
        batch_spec((CONV_W - 1, D_RNN)), batch_spec((1, D_RNN)),
        _const_spec((r_all, LANES)),
        _const_spec((CONV_W, D_RNN)), _const_spec((1, D_RNN)),
        _const_spec((n_gate, GATE_K, 2 * GATE_COLS)), _const_spec((n_gate, 1, 2 * GATE_COLS)), _const_spec((1, D_RNN)),
        _const_spec((Q_DIM, D_MODEL)), _const_spec((D_RNN, D_MODEL)), _const_spec((D_MODEL, D_MODEL)),
        _const_spec((1, D_MODEL)), _const_spec((1, D_MODEL)),
        _const_spec((N_EXPERTS, D_MODEL)), _const_spec((N_EXPERTS, 1)),
    ]
    tok_map = lambda b, s: (b * n_s + s, 0)
    tok_map_t = lambda b, s: (0, b * n_s + s)
    out_shape = (
        jax.ShapeDtypeStruct((n_tok * SLAB, LANES), _F32),
        jax.ShapeDtypeStruct((2 * TOP_K, n_tok), jnp.int32),
        jax.ShapeDtypeStruct((2 * TOP_K, n_tok), _F32),
        jax.ShapeDtypeStruct((N_EXPERTS, 1), _F32),
        jax.ShapeDtypeStruct((bsz, WINDOW, KV_DIM), _F32),
        jax.ShapeDtypeStruct((bsz, WINDOW, KV_DIM), _F32),
        jax.ShapeDtypeStruct((bsz, CONV_W - 1, D_RNN), _F32),
        jax.ShapeDtypeStruct((bsz, 1, D_RNN), _F32),
    )
    out_specs = (
        pl.BlockSpec((rows * SLAB, LANES), tok_map),
        pl.BlockSpec((2 * TOP_K, rows), tok_map_t),
        pl.BlockSpec((2 * TOP_K, rows), tok_map_t),
        pl.BlockSpec((N_EXPERTS, 1), lambda b, s: (0, 0)),
        batch_spec((WINDOW, KV_DIM)), batch_spec((WINDOW, KV_DIM)),
        batch_spec((CONV_W - 1, D_RNN)), batch_spec((1, D_RNN)),
    )
    scratch = [
        pltpu.VMEM((nb, N_KV_HEADS, 2, WINDOW + lt, LANES), _BF16),
        pltpu.VMEM((nb, N_KV_HEADS, 2, WINDOW + lt, LANES), _BF16),
        pltpu.VMEM((nb, CONV_HDR + lt, D_RNN), _F32),
        pltpu.VMEM((rows, D_RNN), _F32),
        pltpu.VMEM((D_RNN // LANES, rows, LANES), _F32),
        pltpu.VMEM((D_RNN // LANES, rows, LANES), _F32),
        pltpu.VMEM((D_RNN // LANES, lt // SCAN_S, LANES), _F32),
        pltpu.VMEM((nb, 1, D_RNN), _F32),
        pltpu.VMEM((rows, Q_DIM), _BF16),
        pltpu.VMEM((N_EXPERTS, 1), _F32),
    ]
    args = [x, rope, prm["w_in"], k0, v0, c0, h0, sink_rows, prm["conv_w"], prm["conv_b"], prm["wg"], prm["bg"],
            prm["lam"], prm["wao"], prm["wro"], prm["wout"], prm["g1"], prm["b1"], prm["rwt"], prm["rb"]]
    return pl.pallas_call(
        kernel,
        grid=(bsz // nb, n_s),
        in_specs=in_specs,
        out_specs=out_specs,
        out_shape=out_shape,
        scratch_shapes=scratch,
        compiler_params=pltpu.CompilerParams(
            dimension_semantics=("arbitrary", "arbitrary"), vmem_limit_bytes=VMEM_LIMIT_BYTES),
        name="mixer_past" if has_past else "mixer_prompt",
    )(*args)


def _slab_rows(ref, first_tok, n):
    return jnp.concatenate([ref[pl.ds(first_tok * SLAB + c, n, stride=SLAB), :] for c in range(SLAB)], axis=1)


def _dispatch_kernel(tt, tm, n_first, n_tiles, zt_ref, nu_ref, pos_ref, xa_ref, xb_ref, xs_hbm, zero_ref, sem):
    i = pl.program_id(0)

    @pl.when(i == 0)
    def _zero_padding():
        zero_ref[...] = jnp.zeros_like(zero_ref)

        def clear(tile):
            dst = xs_hbm.at[pl.ds(pl.multiple_of(tile * (tm * SLAB), SLAB), tm * SLAB)]
            cp = pltpu.make_async_copy(zero_ref, dst, sem)
            cp.start()
            cp.wait()

        for e in range(N_EXPERTS):
            pl.when(zt_ref[e] >= 0)(functools.partial(clear, zt_ref[e]))

        def clear_unused(tile, carry):
            clear(tile)
            return carry

        lax.fori_loop(nu_ref[0], n_tiles, clear_unused, 0)

    def scatter(src_ref):
        def issue(r, carry):
            src = src_ref.at[pl.ds(pl.multiple_of(r * SLAB, SLAB), SLAB)]
            for k in range(TOP_K):
                dst = xs_hbm.at[pl.ds(pl.multiple_of(pos_ref[0, 0, r * TOP_K + k] * SLAB, SLAB), SLAB)]
                pltpu.make_async_copy(src, dst, sem).start()
            return carry

        lax.fori_loop(0, tt, issue, 0)

    pl.when(i < n_first)(lambda: scatter(xa_ref))
    pl.when(i >= n_first)(lambda: scatter(xb_ref))
    n_rows = tt * TOP_K * SLAB
    pltpu.make_async_copy(xs_hbm.at[pl.ds(0, n_rows)], xs_hbm.at[pl.ds(0, n_rows)], sem).wait()


def _dispatch(x1s_a, x1s_b, pos, zero_tiles, n_used, *, tt, tm, n_tiles):
    n_a = x1s_a.shape[0] // SLAB
    n_b = x1s_b.shape[0] // SLAB
    assert n_a % tt == 0 and n_b % tt == 0 and n_tiles * tm >= tt * TOP_K
    n_first = n_a // tt
    n_steps = (n_a + n_b) // tt
    return pl.pallas_call(
        functools.partial(_dispatch_kernel, tt, tm, n_first, n_tiles),
        grid_spec=pltpu.PrefetchScalarGridSpec(
            num_scalar_prefetch=2,
            grid=(n_steps,),
            in_specs=[
                pl.BlockSpec((1, 1, tt * TOP_K), lambda i, zt, nu: (i, 0, 0), memory_space=pltpu.SMEM),
                pl.BlockSpec((tt * SLAB, LANES), lambda i, zt, nu: (jnp.minimum(i, n_first - 1), 0)),
                pl.BlockSpec((tt * SLAB, LANES), lambda i, zt, nu: (jnp.maximum(i - n_first, 0), 0)),
            ],
            out_specs=pl.BlockSpec(memory_space=pl.ANY),
            scratch_shapes=[pltpu.VMEM((tm * SLAB, LANES), _F32), pltpu.SemaphoreType.DMA],
        ),
        out_shape=jax.ShapeDtypeStruct((n_tiles * tm * SLAB, LANES), _F32),
        compiler_params=pltpu.CompilerParams(dimension_semantics=("arbitrary",)),
        name="moe_dispatch",
    )(zero_tiles, n_used, pos.reshape(n_steps, 1, tt * TOP_K), x1s_a, x1s_b)


def _expert_kernel(tm, te_ref, nu_ref, xs_ref, wgu_ref, bgu_ref, wd_ref, bd_ref, ys_ref, wgu_b, wd_b):
    i = pl.program_id(0)

    @pl.when(i < nu_ref[0])
    def _tile():
        prev_e = te_ref[jnp.maximum(i - 1, 0)]

        @pl.when(jnp.logical_or(i == 0, te_ref[i] != prev_e))
        def _new_expert():
            wgu_b[...] = wgu_ref[0].astype(_BF16)
            wd_b[...] = wd_ref[0].astype(_BF16)

        x = _slab_rows(xs_ref, 0, tm).astype(_BF16)
        hgu = _dot(x, wgu_b[...]) + bgu_ref[0]
        glu = jnp.minimum(hgu[:, 0:D_FF], SWIGLU_LIMIT)
        lin = jnp.clip(hgu[:, D_FF:2 * D_FF], -SWIGLU_LIMIT, SWIGLU_LIMIT)
        hh = glu * _sigmoid(SWIGLU_ALPHA * glu) * (lin + 1.0)
        y = _dot(hh.astype(_BF16), wd_b[...]) + bd_ref[0]
        for c in range(SLAB):
            ys_ref[pl.ds(c, tm, stride=SLAB), :] = y[:, c * LANES:(c + 1) * LANES]

    @pl.when(i >= nu_ref[0])
    def _unused_tile():
        ys_ref[...] = jnp.zeros_like(ys_ref)


def _experts(xs, tile_expert, n_used, prm, *, tm):
    n_tiles = xs.shape[0] // (tm * SLAB)
    row_map = lambda i, te, nu: (jnp.minimum(i, nu[0] - 1), 0)
    out_map = lambda i, te, nu: (i, 0)
    exp_map = lambda i, te, nu: (te[i], 0, 0)
    return pl.pallas_call(
        functools.partial(_expert_kernel, tm),
        grid_spec=pltpu.PrefetchScalarGridSpec(
            num_scalar_prefetch=2,
            grid=(n_tiles,),
            in_specs=[
                pl.BlockSpec((tm * SLAB, LANES), row_map),
                pl.BlockSpec((1, D_MODEL, 2 * D_FF), exp_map),
                pl.BlockSpec((1, 1, 2 * D_FF), exp_map),
                pl.BlockSpec((1, D_FF, D_MODEL), exp_map),
                pl.BlockSpec((1, 1, D_MODEL), exp_map),
            ],
            out_specs=pl.BlockSpec((tm * SLAB, LANES), out_map),
            scratch_shapes=[pltpu.VMEM((D_MODEL, 2 * D_FF), _BF16), pltpu.VMEM((D_FF, D_MODEL), _BF16)],
        ),
        out_shape=jax.ShapeDtypeStruct(xs.shape, _F32),
        compiler_params=pltpu.CompilerParams(
            dimension_semantics=("arbitrary",), vmem_limit_bytes=VMEM_LIMIT_BYTES),
        name="moe_experts",
    )(tile_expert, n_used, xs, prm["wgu"], prm["bgu"], prm["wd"], prm["bd"])


def _combine_kernel(tt, n_first, pos_ref, gate_ref, xa_ref, xb_ref, ys_hbm, g2_ref, b2_ref, ya_ref, yb_ref, ybuf, sem):
    i = pl.program_id(0)

    def issue(r, carry):
        for k in range(TOP_K):
            src = ys_hbm.at[pl.ds(pl.multiple_of(pos_ref[0, 0, r * TOP_K + k] * SLAB, SLAB), SLAB)]
            dst = ybuf.at[pl.ds(pl.multiple_of((k * tt + r) * SLAB, SLAB), SLAB)]
            pltpu.make_async_copy(src, dst, sem).start()
        return carry

    lax.fori_loop(0, tt, issue, 0)
    n_rows = tt * TOP_K * SLAB
    pltpu.make_async_copy(ys_hbm.at[pl.ds(0, n_rows)], ybuf, sem).wait()

    gates = gate_ref[...]
    acc = gates[:, 0:1] * _slab_rows(ybuf, 0, tt)
    for k in range(1, TOP_K):
        acc = acc + gates[:, k:k + 1] * _slab_rows(ybuf, k * tt, tt)

    def finish(x_ref, y_ref):
        x1 = _slab_rows(x_ref, 0, tt)
        y_ref[...] = _layer_norm(DN_ALPHA * x1 + acc, g2_ref[...], b2_ref[...])

    pl.when(i < n_first)(lambda: finish(xa_ref, ya_ref))
    pl.when(i >= n_first)(lambda: finish(xb_ref, yb_ref))


def _combine(x1s_a, x1s_b, ys, pos, gates, prm, *, tt):
    n_a = x1s_a.shape[0] // SLAB
    n_b = x1s_b.shape[0] // SLAB
    assert n_a % tt == 0 and n_b % tt == 0
    n_first = n_a // tt
    n_steps = (n_a + n_b) // tt
    a_map = lambda i: (jnp.minimum(i, n_first - 1), 0)
    b_map = lambda i: (jnp.maximum(i - n_first, 0), 0)
    return pl.pallas_call(
        functools.partial(_combine_kernel, tt, n_first),
        grid=(n_steps,),
        in_specs=[
            pl.BlockSpec((1, 1, tt * TOP_K), lambda i: (i, 0, 0), memory_space=pltpu.SMEM),
            pl.BlockSpec((tt, TOP_K), lambda i: (i, 0)),
            pl.BlockSpec((tt * SLAB, LANES), a_map),
            pl.BlockSpec((tt * SLAB, LANES), b_map),
            pl.BlockSpec(memory_space=pl.ANY),
            pl.BlockSpec((1, D_MODEL), lambda i: (0, 0)),
            pl.BlockSpec((1, D_MODEL), lambda i: (0, 0)),
        ],
        out_specs=(pl.BlockSpec((tt, D_MODEL), a_map), pl.BlockSpec((tt, D_MODEL), b_map)),
        out_shape=(jax.ShapeDtypeStruct((n_a, D_MODEL), _F32), jax.ShapeDtypeStruct((n_b, D_MODEL), _F32)),
        scratch_shapes=[pltpu.VMEM((TOP_K * tt * SLAB, LANES), _F32), pltpu.SemaphoreType.DMA],
        compiler_params=pltpu.CompilerParams(dimension_semantics=("arbitrary",)),
        name="moe_combine",
    )(pos.reshape(n_steps, 1, tt * TOP_K), gates, x1s_a, x1s_b, ys, prm["g2"], prm["b2"])


def _route(meta_a, meta_b, gate_a, gate_b, counts_a, counts_b, tm, n_tiles):
    meta = jnp.concatenate([meta_a, meta_b], axis=1).T
    idx = meta[:, 0:TOP_K]
    rank = meta[:, TOP_K:2 * TOP_K]
    gates = jnp.concatenate([gate_a[0:TOP_K], gate_b[0:TOP_K]], axis=1).T
    cp = counts_a.reshape(-1).astype(jnp.int32)
    cs = counts_b.reshape(-1).astype(jnp.int32)
    is_b = (jnp.arange(meta.shape[0]) >= meta_a.shape[1])[:, None]
    rank = rank + jnp.where(is_b, cp[idx], 0)
    tiles_per = (cp + cs + tm - 1) // tm
    tile_end = jnp.cumsum(tiles_per)
    tile_start = tile_end - tiles_per
    pos = tile_start[idx] * tm + rank
    n_used = tile_end[-1:]
    tile_ids = jnp.arange(n_tiles, dtype=jnp.int32)
    tile_expert = jnp.minimum(jnp.sum(tile_ids[:, None] >= tile_end[None, :], axis=1), N_EXPERTS - 1)
    last_e = tile_expert[jnp.maximum(n_used[0] - 1, 0)]
    tile_expert = jnp.where(tile_ids < n_used[0], tile_expert, last_e).astype(jnp.int32)
    zero_tiles = jnp.where(tiles_per > 0, tile_end - 1, -1).astype(jnp.int32)
    return pos.astype(jnp.int32), gates, tile_expert, n_used.astype(jnp.int32), zero_tiles


def _rope_tables(pos):
    half = ROT_DIM // 2
    inv_freq = ROPE_THETA ** (-jnp.arange(half, dtype=_F32) / half)
    ang = pos.astype(_F32)[:, None] * inv_freq[None, :]
    cos, sin = jnp.cos(ang), jnp.sin(ang)
    n = pos.shape[0]
    ones = jnp.ones((n, HEAD_DIM - ROT_DIM), _F32)
    zeros = jnp.zeros((n, HEAD_DIM - ROT_DIM), _F32)
    zh = jnp.zeros((n, half), _F32)
    cos_t = jnp.concatenate([cos, cos, ones], axis=1)
    sin_up = jnp.concatenate([-sin, zh, zeros], axis=1)
    sin_dn = jnp.concatenate([zh, sin, zeros], axis=1)
    tab = jnp.stack([cos_t, sin_up, sin_dn])
    return jnp.concatenate([tab, tab], axis=2)


def _block_diag(w):
    hd, d, _ = w.shape
    eye = jnp.eye(hd, dtype=w.dtype)
    return (eye[:, None, :, None] * w[:, :, None, :]).reshape(hd * d, hd * d)


def _gate_blocks(wa, wx):
    blocks = []
    head_of = np.arange(D_RNN) // RNN_HEAD_DIM
    for j, k0 in enumerate(GATE_STARTS):
        cols = np.arange(j * GATE_COLS, (j + 1) * GATE_COLS)
        reach = np.flatnonzero(np.isin(head_of, head_of[cols]))
        assert reach.min() >= k0 and reach.max() < k0 + GATE_K
        cs = slice(j * GATE_COLS, (j + 1) * GATE_COLS)
        blocks.append(jnp.concatenate([wa[k0:k0 + GATE_K, cs], wx[k0:k0 + GATE_K, cs]], axis=1))
    return jnp.stack(blocks)


def kernel(x_prompt, x_sample, cache_k, cache_v, state_conv, state_h, w_in, attn_sinks, w_attn_out, conv_w, conv_b, gate_a_w, gate_a_b, gate_x_w, gate_x_b, lru_lambda, w_rnn_out, w_out, ln1_g, ln1_b, router_w, router_b, w_gate_up, b_gate_up, w_down, b_down, ln2_g, ln2_b):
    assert w_in.shape[0] == DEPTH == 1
    l = 0
    bsz, seq, _ = x_prompt.shape
    dbsz, dseq, _ = x_sample.shape
    row = lambda a: a.reshape(1, -1)
    prm = {
        "w_in": w_in[l].astype(_BF16),
        "conv_w": conv_w[l], "conv_b": row(conv_b[l]),
        "wg": _gate_blocks(_block_diag(gate_a_w[l]), _block_diag(gate_x_w[l])).astype(_BF16),
        "bg": jnp.concatenate([gate_a_b[l].reshape(-1, 1, GATE_COLS), gate_x_b[l].reshape(-1, 1, GATE_COLS)], axis=2),
        "lam": row(lru_lambda[l]),
        "wao": w_attn_out[l].astype(_BF16), "wro": w_rnn_out[l].astype(_BF16), "wout": w_out[l].astype(_BF16),
        "g1": row(ln1_g[l]), "b1": row(ln1_b[l]),
        "rwt": router_w[l].T.astype(_BF16), "rb": router_b[l].reshape(-1, 1),
        "wgu": w_gate_up[l], "bgu": b_gate_up[l][:, None, :],
        "wd": w_down[l], "bd": b_down[l][:, None, :],
        "g2": row(ln2_g[l]), "b2": row(ln2_b[l]),
    }
    sinks = attn_sinks[l]

    rope_p = _rope_tables(jnp.arange(seq, dtype=jnp.int32))
    rope_s = _rope_tables(PAST_LEN + jnp.arange(dseq, dtype=jnp.int32))

    n_prompt = bsz * seq
    n_tok = n_prompt + dbsz * dseq
    zeros_kv = jnp.zeros((bsz, WINDOW, KV_DIM), _F32)
    x1_p, meta_p, gate_p, cnt_p, pk, pv, pc, ph = _mixer(
        x_prompt, rope_p, zeros_kv, zeros_kv, jnp.zeros((bsz, CONV_W - 1, D_RNN), _F32),
        jnp.zeros((bsz, 1, D_RNN), _F32), sinks, prm, has_past=False, nb=1, lt=min(MIXER_ROWS, seq), lq=CHUNK)
    x1_s, meta_s, gate_s, cnt_s, sk, sv, sc, sh = _mixer(
        x_sample, rope_s, cache_k[l].reshape(dbsz, WINDOW, KV_DIM), cache_v[l].reshape(dbsz, WINDOW, KV_DIM),
        state_conv[l], state_h[l][:, None, :], sinks, prm, has_past=True, nb=min(MIXER_ROWS // dseq, dbsz),
        lt=dseq, lq=dseq)

    tm = EXPERT_ROWS
    n_tiles = (n_tok * TOP_K + N_EXPERTS * (tm - 1)) // tm + 1
    pos, gates, tile_expert, n_used, zero_tiles = _route(meta_p, meta_s, gate_p, gate_s, cnt_p, cnt_s, tm, n_tiles)
    xs = _dispatch(x1_p, x1_s, pos, zero_tiles, n_used, tt=TOKEN_ROWS, tm=tm, n_tiles=n_tiles)
    ysort = _experts(xs, tile_expert, n_used, prm, tm=tm)
    yp, ys = _combine(x1_p, x1_s, ysort, pos, gates, prm, tt=TOKEN_ROWS)
    yp = yp.reshape(bsz, seq, D_MODEL)
    ys = ys.reshape(dbsz, dseq, D_MODEL)

    kv5 = lambda a: a.reshape(1, a.shape[0], WINDOW, N_KV_HEADS, HEAD_DIM)
    return (yp, ys, kv5(pk), kv5(pv), pc[None], ph.reshape(1, bsz, D_RNN),
            kv5(sk), kv5(sv), sc[None], sh.reshape(1, dbsz, D_RNN))
```

```python
import functools

import jax
import jax.numpy as jnp
import numpy as np
from jax import lax
from jax.experimental import pallas as pl
from jax.experimental.pallas import tpu as pltpu

D_MODEL = 1024
PAST_LEN = 1024
CHUNK = 64
N_HEADS = 16
N_KV_HEADS = 4
HEAD_DIM = 64
GROUP = N_HEADS // N_KV_HEADS
ROT_DIM = HEAD_DIM // 4
ROPE_THETA = 500000.0
WINDOW = 128
ATTN_SCALE = HEAD_DIM ** -0.5
NEG_INF = -1e30
D_RNN = 1280
RNN_HEADS = 16
RNN_HEAD_DIM = D_RNN // RNN_HEADS
CONV_W = 4
LRU_C = 8.0
N_EXPERTS = 32
TOP_K = 4
D_FF = 1024
SWIGLU_ALPHA = 1.702
SWIGLU_LIMIT = 7.0
LN_EPS = 1e-5
DEPTH = 1
DN_ALPHA = (2.0 * DEPTH) ** 0.25

Q_DIM = N_HEADS * HEAD_DIM
KV_DIM = N_KV_HEADS * HEAD_DIM
OFF_K = Q_DIM
OFF_V = OFF_K + KV_DIM
OFF_XR = OFF_V + KV_DIM
OFF_YR = OFF_XR + D_RNN
OFF_GA = OFF_YR + D_RNN
OFF_GR = OFF_GA + D_MODEL
IN_DIM = OFF_GR + D_MODEL

LANES = 128
SLAB = D_MODEL // LANES
MIXER_ROWS = 256
EXPERT_ROWS = 256
TOKEN_ROWS = 256
SCAN_S = 8
GATE_COLS = 256
GATE_K = 512
GATE_STARTS = tuple(min(max((j * GATE_COLS // RNN_HEAD_DIM) * RNN_HEAD_DIM // LANES * LANES, 0), D_RNN - GATE_K)
                    for j in range(D_RNN // GATE_COLS))
CONV_HDR = 8
VMEM_LIMIT_BYTES = 56 * 1024 * 1024

_BF16 = jnp.bfloat16
_F32 = jnp.float32


def _dot(a, b):
    return jnp.dot(a, b, preferred_element_type=_F32)


def _dot_nt(a, b):
    return lax.dot_general(a, b, (((1,), (1,)), ((), ())), preferred_element_type=_F32)


def _sigmoid(x):
    return 0.5 * jnp.tanh(0.5 * x) + 0.5


def _layer_norm(x, g, b):
    mu = jnp.mean(x, axis=-1, keepdims=True)
    xc = x - mu
    var = jnp.mean(xc * xc, axis=-1, keepdims=True)
    return xc * lax.rsqrt(var + LN_EPS) * g + b


def _rope(x, cos_t, sin_up, sin_dn):
    return x * cos_t + pltpu.roll(x, LANES - ROT_DIM // 2, 1) * sin_up + pltpu.roll(x, ROT_DIM // 2, 1) * sin_dn


def _lo_hi(slab, h):
    blk = slab[:, (h // 2) * LANES:(h // 2 + 1) * LANES]
    lane = lax.broadcasted_iota(jnp.int32, blk.shape, 1)
    if h % 2 == 0:
        lo = jnp.where(lane < HEAD_DIM, blk, 0.0)
        hi = pltpu.roll(lo, HEAD_DIM, 1)
    else:
        hi = jnp.where(lane >= HEAD_DIM, blk, 0.0)
        lo = pltpu.roll(hi, HEAD_DIM, 1)
    return lo.astype(_BF16), hi.astype(_BF16)


def _mixer_kernel(has_past, nb, lt, lq,
                  x_ref, rope_ref, w_in_ref, k0_ref, v0_ref, c0_ref, h0_ref, sinkrows_ref,
                  convw_ref, convb_ref, wg_ref, bg_ref, lam_ref, wao_ref, wro_ref, wout_ref,
                  g1_ref, b1_ref, rwt_ref, rb_ref,
                  x1s_ref, meta_ref, gate_ref, cnt_ref, kn_ref, vn_ref, cn_ref, hn_ref,
                  kw_ref, vw_ref, xp_ref, xc_ref, a_ref, b_ref, hin_ref, hc_ref, oat_ref, run_ref):
    s_idx = pl.program_id(1)
    n_s = pl.num_programs(1)
    rows = nb * lt
    n_chunks = lt // lq

    @pl.when(jnp.logical_and(pl.program_id(0) == 0, s_idx == 0))
    def _init_counts():
        run_ref[...] = jnp.zeros_like(run_ref)

    @pl.when(s_idx == 0)
    def _init():
        for bi in range(nb):
            for h in range(N_KV_HEADS):
                klo, khi = _lo_hi(k0_ref[bi], h)
                vlo, vhi = _lo_hi(v0_ref[bi], h)
                kw_ref[bi, h, 0, 0:WINDOW, :] = klo
                kw_ref[bi, h, 1, 0:WINDOW, :] = khi
                vw_ref[bi, h, 0, 0:WINDOW, :] = vlo
                vw_ref[bi, h, 1, 0:WINDOW, :] = vhi
            xp_ref[bi, CONV_HDR - (CONV_W - 1):CONV_HDR, :] = c0_ref[bi]
            hc_ref[bi] = h0_ref[bi]

    x = x_ref[...].reshape(rows, D_MODEL)
    xb = x.astype(_BF16)

    cos_t = jnp.concatenate([rope_ref[0]] * nb, axis=0) if nb > 1 else rope_ref[0]
    sin_up = jnp.concatenate([rope_ref[1]] * nb, axis=0) if nb > 1 else rope_ref[1]
    sin_dn = jnp.concatenate([rope_ref[2]] * nb, axis=0) if nb > 1 else rope_ref[2]

    zq = _dot(xb, w_in_ref[:, 0:Q_DIM])
    q_blocks = []
    for c in range(Q_DIM // LANES):
        qr = _rope(zq[:, c * LANES:(c + 1) * LANES], cos_t, sin_up, sin_dn)
        q_blocks.append((qr * ATTN_SCALE).astype(_BF16))
    zk = _dot(xb, w_in_ref[:, OFF_K:OFF_V])
    k_rot = jnp.concatenate(
        [_rope(zk[:, c * LANES:(c + 1) * LANES], cos_t, sin_up, sin_dn) for c in range(KV_DIM // LANES)], axis=1)
    v_new = _dot(xb, w_in_ref[:, OFF_V:OFF_XR])

    for bi in range(nb):
        r0 = bi * lt
        for h in range(N_KV_HEADS):
            klo, khi = _lo_hi(k_rot[r0:r0 + lt], h)
            vlo, vhi = _lo_hi(v_new[r0:r0 + lt], h)
            kw_ref[bi, h, 0, WINDOW:WINDOW + lt, :] = klo
            kw_ref[bi, h, 1, WINDOW:WINDOW + lt, :] = khi
            vw_ref[bi, h, 0, WINDOW:WINDOW + lt, :] = vlo
            vw_ref[bi, h, 1, WINDOW:WINDOW + lt, :] = vhi

    for bi in range(nb):
        r0 = bi * lt
        if lt >= WINDOW:
            kn_ref[bi] = k_rot[r0 + lt - WINDOW:r0 + lt]
            vn_ref[bi] = v_new[r0 + lt - WINDOW:r0 + lt]
        else:
            kn_ref[bi, 0:WINDOW - lt, :] = k0_ref[bi, lt:WINDOW, :]
            kn_ref[bi, WINDOW - lt:WINDOW, :] = k_rot[r0:r0 + lt]
            vn_ref[bi, 0:WINDOW - lt, :] = v0_ref[bi, lt:WINDOW, :]
            vn_ref[bi, WINDOW - lt:WINDOW, :] = v_new[r0:r0 + lt]

    w2 = 2 * lq
    wk = 2 * WINDOW + w2
    r_all = N_KV_HEADS * 2 * lq
    is_a2 = lax.broadcasted_iota(jnp.int32, (r_all, w2), 1) < lq
    lane_lo = lax.broadcasted_iota(jnp.int32, (r_all, LANES), 1) < HEAD_DIM
    key_row = lax.broadcasted_iota(jnp.int32, (wk, LANES), 0)
    key_is_a = jnp.logical_or(key_row < WINDOW, jnp.logical_and(key_row >= 2 * WINDOW, key_row < 2 * WINDOW + lq))
    key_lane_lo = lax.broadcasted_iota(jnp.int32, (wk, LANES), 1) < HEAD_DIM
    head_ones = jnp.where(key_is_a == key_lane_lo, 1.0, 0.0).astype(_BF16)
    sink_rows = sinkrows_ref[...]
    sink_a = sink_rows[:, 0:1]
    sink_b = sink_rows[:, HEAD_DIM:HEAD_DIM + 1]
    first = s_idx == 0
    for bi in range(nb):
        for j in range(n_chunks):
            p0 = j * lq
            o0 = WINDOW + j * lq
            q0 = bi * lt + j * lq
            scores = []
            vmats = []
            for h in range(N_KV_HEADS):
                kmat = jnp.concatenate([kw_ref[bi, h, 0, p0:p0 + WINDOW, :], kw_ref[bi, h, 1, p0:p0 + WINDOW, :],
                                        kw_ref[bi, h, 0, o0:o0 + lq, :], kw_ref[bi, h, 1, o0:o0 + lq, :]], axis=0)
                vmat = jnp.concatenate([vw_ref[bi, h, 0, p0:p0 + WINDOW, :], vw_ref[bi, h, 1, p0:p0 + WINDOW, :],
                                        vw_ref[bi, h, 0, o0:o0 + lq, :], vw_ref[bi, h, 1, o0:o0 + lq, :]], axis=0)
                vmats.append(jnp.concatenate([vmat, head_ones], axis=1))
                qs = jnp.concatenate([q_blocks[2 * h][q0:q0 + lq], q_blocks[2 * h + 1][q0:q0 + lq]], axis=0)
                scores.append(_dot_nt(qs, kmat))
            sc = jnp.concatenate(scores, axis=0)
            c0 = sc[:, 0:LANES]
            c1 = sc[:, LANES:2 * LANES]
            c2 = sc[:, 2 * LANES:wk]
            if not has_past and j * lq < WINDOW:
                bad = jnp.logical_and(first, lax.broadcasted_iota(jnp.int32, c0.shape, 1) < WINDOW - j * lq)
                c0 = jnp.where(bad, NEG_INF, c0)
                c1 = jnp.where(bad, NEG_INF, c1)
            if w2 == LANES:
                ma = jnp.max(jnp.maximum(c0, jnp.where(is_a2, c2, NEG_INF)), axis=1, keepdims=True)
                mb = jnp.max(jnp.maximum(c1, jnp.where(is_a2, NEG_INF, c2)), axis=1, keepdims=True)
            else:
                ma = jnp.maximum(jnp.max(c0, axis=1, keepdims=True),
                                 jnp.max(jnp.where(is_a2, c2, NEG_INF), axis=1, keepdims=True))
                mb = jnp.maximum(jnp.max(c1, axis=1, keepdims=True),
                                 jnp.max(jnp.where(is_a2, NEG_INF, c2), axis=1, keepdims=True))
            ma = jnp.maximum(ma, sink_a)
            mb = jnp.maximum(mb, sink_b)
            probs = jnp.concatenate([jnp.exp(c0 - ma), jnp.exp(c1 - mb), jnp.exp(c2 - jnp.where(is_a2, ma, mb))],
                                    axis=1).astype(_BF16)
            sink_term = jnp.exp(sink_rows - jnp.where(lane_lo, ma, mb))
            for h in range(N_KV_HEADS):
                rs = slice(h * 2 * lq, (h + 1) * 2 * lq)
                od = _dot(probs[rs], vmats[h])
                out = od[:, 0:LANES] / (od[:, LANES:2 * LANES] + sink_term[rs])
                for p in range(2):
                    oat_ref[q0:q0 + lq, (2 * h + p) * LANES:(2 * h + p + 1) * LANES] = (
                        out[p * lq:(p + 1) * lq].astype(_BF16))

    if lt >= WINDOW:
        @pl.when(s_idx + 1 < n_s)
        def _carry_kv():
            for bi in range(nb):
                for h in range(N_KV_HEADS):
                    for v in range(2):
                        kw_ref[bi, h, v, 0:WINDOW, :] = kw_ref[bi, h, v, lt:lt + WINDOW, :]
                        vw_ref[bi, h, v, 0:WINDOW, :] = vw_ref[bi, h, v, lt:lt + WINDOW, :]

    xr = _dot(xb, w_in_ref[:, OFF_XR:OFF_YR])
    for bi in range(nb):
        r0 = bi * lt
        xp_ref[bi, CONV_HDR:CONV_HDR + lt, :] = xr[r0:r0 + lt]
        acc = xp_ref[bi, CONV_HDR - 3:CONV_HDR - 3 + lt, :] * convw_ref[0:1, :]
        for t in range(1, CONV_W):
            acc = acc + xp_ref[bi, CONV_HDR - 3 + t:CONV_HDR - 3 + t + lt, :] * convw_ref[t:t + 1, :]
        xc_ref[r0:r0 + lt, :] = acc + convb_ref[...]
        tail = xp_ref[bi, CONV_HDR + lt - (CONV_W - 1):CONV_HDR + lt, :]
        cn_ref[bi] = tail
        xp_ref[bi, CONV_HDR - (CONV_W - 1):CONV_HDR, :] = tail

    xc = xc_ref[...]
    xcb = xc.astype(_BF16)
    lam = lam_ref[...]
    softplus_neg = jnp.maximum(-lam, 0.0) + jnp.log(1.0 + jnp.exp(-jnp.abs(lam)))
    for j in range(D_RNN // GATE_COLS):
        cs = slice(j * GATE_COLS, (j + 1) * GATE_COLS)
        g = _dot(xcb[:, GATE_STARTS[j]:GATE_STARTS[j] + GATE_K], wg_ref[j]) + bg_ref[j]
        r_gate = _sigmoid(g[:, 0:GATE_COLS])
        i_gate = _sigmoid(g[:, GATE_COLS:2 * GATE_COLS])
        log_a = (-LRU_C * r_gate) * softplus_neg[:, cs]
        a = jnp.exp(log_a)
        bv = jnp.sqrt(-jnp.tanh(log_a) * (1.0 + a * a)) * i_gate * xc[:, cs]
        for cc in range(GATE_COLS // LANES):
            c = j * (GATE_COLS // LANES) + cc
            a_ref[c] = a[:, cc * LANES:(cc + 1) * LANES]
            b_ref[c] = bv[:, cc * LANES:(cc + 1) * LANES]

    n_lb = D_RNN // LANES
    n_g = lt // SCAN_S
    for bi in range(nb):
        r0 = bi * lt
        a_tot, h_tot = [], []
        for c in range(n_lb):
            at = a_ref[c, pl.ds(r0, n_g, stride=SCAN_S), :]
            ht = b_ref[c, pl.ds(r0, n_g, stride=SCAN_S), :]
            for k in range(1, SCAN_S):
                ak = a_ref[c, pl.ds(r0 + k, n_g, stride=SCAN_S), :]
                ht = ak * ht + b_ref[c, pl.ds(r0 + k, n_g, stride=SCAN_S), :]
                at = ak * at
            a_tot.append(at)
            h_tot.append(ht)
        carry = [hc_ref[bi][:, c * LANES:(c + 1) * LANES] for c in range(n_lb)]
        for g in range(n_g):
            for c in range(n_lb):
                hin_ref[c, g:g + 1, :] = carry[c]
                carry[c] = a_tot[c][g:g + 1, :] * carry[c] + h_tot[c][g:g + 1, :]
        h_last = jnp.concatenate(carry, axis=1)
        hc_ref[bi] = h_last
        hn_ref[bi] = h_last
        for c in range(n_lb):
            hcur = hin_ref[c]
            for k in range(SCAN_S):
                hcur = a_ref[c, pl.ds(r0 + k, n_g, stride=SCAN_S), :] * hcur + b_ref[c, pl.ds(r0 + k, n_g, stride=SCAN_S), :]
                b_ref[c, pl.ds(r0 + k, n_g, stride=SCAN_S), :] = hcur

    yr = _dot(xb, w_in_ref[:, OFF_YR:OFF_GA])
    rnn = jnp.concatenate([b_ref[c] for c in range(n_lb)], axis=1) * jax.nn.gelu(yr)

    ga = _dot(xb, w_in_ref[:, OFF_GA:OFF_GR])
    gr = _dot(xb, w_in_ref[:, OFF_GR:IN_DIM])
    merged = (_sigmoid(ga) * _dot(oat_ref[...], wao_ref[...])
              + _sigmoid(gr) * _dot(rnn.astype(_BF16), wro_ref[...]))
    mix = _dot(merged.astype(_BF16), wout_ref[...])
    x1 = _layer_norm(DN_ALPHA * x + mix, g1_ref[...], b1_ref[...])
    for c in range(SLAB):
        x1s_ref[pl.ds(c, rows, stride=SLAB), :] = x1[:, c * LANES:(c + 1) * LANES]

    logits = _dot_nt(rwt_ref[...], x1.astype(_BF16)) + rb_ref[...]
    e_iota = lax.broadcasted_iota(jnp.int32, logits.shape, 0)
    work = logits
    top_vals = []
    top_idx = []
    top_sel = []
    for _ in range(TOP_K):
        m = jnp.max(work, axis=0, keepdims=True)
        idx = jnp.min(jnp.where(work == m, e_iota, N_EXPERTS), axis=0, keepdims=True)
        sel = e_iota == idx
        top_vals.append(m)
        top_idx.append(idx)
        top_sel.append(sel)
        work = jnp.where(sel, -jnp.inf, work)
    exps = [jnp.exp(v - top_vals[0]) for v in top_vals]
    denom = exps[0]
    for e in exps[1:]:
        denom = denom + e

    onehot = jnp.zeros_like(logits)
    for sel in top_sel:
        onehot = onehot + jnp.where(sel, 1.0, 0.0)
    r_i = lax.broadcasted_iota(jnp.int32, (rows, rows), 0)
    c_i = lax.broadcasted_iota(jnp.int32, (rows, rows), 1)
    earlier = jnp.where(r_i < c_i, 1.0, 0.0).astype(_BF16)
    before = _dot(onehot.astype(_BF16), earlier) + run_ref[...]
    run_ref[...] = run_ref[...] + jnp.sum(onehot, axis=1, keepdims=True)
    cnt_ref[...] = run_ref[...]

    sub_m = lax.broadcasted_iota(jnp.int32, (2 * TOP_K, rows), 0)
    meta_i = jnp.zeros((2 * TOP_K, rows), jnp.int32)
    meta_g = jnp.zeros((2 * TOP_K, rows), _F32)
    for k in range(TOP_K):
        rank_k = jnp.sum(jnp.where(top_sel[k], before, 0.0), axis=0, keepdims=True).astype(jnp.int32)
        meta_i = jnp.where(sub_m == k, top_idx[k], meta_i)
        meta_i = jnp.where(sub_m == TOP_K + k, rank_k, meta_i)
        meta_g = jnp.where(sub_m == k, exps[k] / denom, meta_g)
    meta_ref[...] = meta_i
    gate_ref[...] = meta_g


def _const_spec(shape):
    nd = len(shape)
    return pl.BlockSpec(shape, lambda b, s, _nd=nd: (0,) * _nd, pipeline_mode=pl.Buffered(1))


def _mixer(x, rope, k0, v0, c0, h0, sinks, prm, *, has_past, nb, lt, lq):
    bsz, seq, _ = x.shape
    assert bsz % nb == 0 and seq % lt == 0 and lt % lq == 0 and lq % 32 == 0
    assert lt >= WINDOW or seq == lt
    n_s = seq // lt
    rows = nb * lt
    n_tok = bsz * seq
    assert lt % SCAN_S == 0
    kernel = functools.partial(_mixer_kernel, has_past, nb, lt, lq)
    r_all = N_KV_HEADS * 2 * lq
    sink_rows = jnp.broadcast_to(sinks.reshape(N_KV_HEADS * 2, 1, 2, 1), (N_KV_HEADS * 2, lq, 2, HEAD_DIM))
    sink_rows = sink_rows.reshape(r_all, LANES)
    n_gate = D_RNN // GATE_COLS
    batch_spec = lambda shape: pl.BlockSpec((nb,) + shape, lambda b, s: (b,) + (0,) * len(shape))
    in_specs = [
        pl.BlockSpec((nb, lt, D_MODEL), lambda b, s: (b, s, 0)),
        pl.BlockSpec((3, lt, LANES), lambda b, s: (0, s, 0)),
        _const_spec((D_MODEL, IN_DIM)),
        batch_spec((WINDOW, KV_DIM)), batch_spec((WINDOW, KV_DIM)),
        batch_spec((CONV_W - 1, D_RNN)), batch_spec((1, D_RNN)),
        _const_spec((r_all, LANES)),
        _const_spec((CONV_W, D_RNN)), _const_spec((1, D_RNN)),
        _const_spec((n_gate, GATE_K, 2 * GATE_COLS)), _const_spec((n_gate, 1, 2 * GATE_COLS)), _const_spec((1, D_RNN)),
        _const_spec((Q_DIM, D_MODEL)), _const_spec((D_RNN, D_MODEL)), _const_spec((D_MODEL, D_MODEL)),
        _const_spec((1, D_MODEL)), _const_spec((1, D_MODEL)),
        _const_spec((N_EXPERTS, D_MODEL)), _const_spec((N_EXPERTS, 1)),
    ]
    tok_map = lambda b, s: (b * n_s + s, 0)
    tok_map_t = lambda b, s: (0, b * n_s + s)
    out_shape = (
        jax.ShapeDtypeStruct((n_tok * SLAB, LANES), _F32),
        jax.ShapeDtypeStruct((2 * TOP_K, n_tok), jnp.int32),
        jax.ShapeDtypeStruct((2 * TOP_K, n_tok), _F32),
        jax.ShapeDtypeStruct((N_EXPERTS, 1), _F32),
        jax.ShapeDtypeStruct((bsz, WINDOW, KV_DIM), _F32),
        jax.ShapeDtypeStruct((bsz, WINDOW, KV_DIM), _F32),
        jax.ShapeDtypeStruct((bsz, CONV_W - 1, D_RNN), _F32),
        jax.ShapeDtypeStruct((bsz, 1, D_RNN), _F32),
    )
    out_specs = (
        pl.BlockSpec((rows * SLAB, LANES), tok_map),
        pl.BlockSpec((2 * TOP_K, rows), tok_map_t),
        pl.BlockSpec((2 * TOP_K, rows), tok_map_t),
        pl.BlockSpec((N_EXPERTS, 1), lambda b, s: (0, 0)),
        batch_spec((WINDOW, KV_DIM)), batch_spec((WINDOW, KV_DIM)),
        batch_spec((CONV_W - 1, D_RNN)), batch_spec((1, D_RNN)),
    )
    scratch = [
        pltpu.VMEM((nb, N_KV_HEADS, 2, WINDOW + lt, LANES), _BF16),
        pltpu.VMEM((nb, N_KV_HEADS, 2, WINDOW + lt, LANES), _BF16),
        pltpu.VMEM((nb, CONV_HDR + lt, D_RNN), _F32),
        pltpu.VMEM((rows, D_RNN), _F32),
        pltpu.VMEM((D_RNN // LANES, rows, LANES), _F32),
        pltpu.VMEM((D_RNN // LANES, rows, LANES), _F32),
        pltpu.VMEM((D_RNN // LANES, lt // SCAN_S, LANES), _F32),
        pltpu.VMEM((nb, 1, D_RNN), _F32),
        pltpu.VMEM((rows, Q_DIM), _BF16),
        pltpu.VMEM((N_EXPERTS, 1), _F32),
    ]
    args = [x, rope, prm["w_in"], k0, v0, c0, h0, sink_rows, prm["conv_w"], prm["conv_b"], prm["wg"], prm["bg"],
            prm["lam"], prm["wao"], prm["wro"], prm["wout"], prm["g1"], prm["b1"], prm["rwt"], prm["rb"]]
    return pl.pallas_call(
        kernel,
        grid=(bsz // nb, n_s),
        in_specs=in_specs,
        out_specs=out_specs,
        out_shape=out_shape,
        scratch_shapes=scratch,
        compiler_params=pltpu.CompilerParams(
            dimension_semantics=("arbitrary", "arbitrary"), vmem_limit_bytes=VMEM_LIMIT_BYTES),
        name="mixer_past" if has_past else "mixer_prompt",
    )(*args)


def _slab_rows(ref, first_tok, n):
    return jnp.concatenate([ref[pl.ds(first_tok * SLAB + c, n, stride=SLAB), :] for c in range(SLAB)], axis=1)


def _dispatch_kernel(tt, tm, n_first, n_tiles, zt_ref, nu_ref, pos_ref, xa_ref, xb_ref, xs_hbm, zero_ref, sem):
    i = pl.program_id(0)

    @pl.when(i == 0)
    def _zero_padding():
        zero_ref[...] = jnp.zeros_like(zero_ref)

        def clear(tile):
            dst = xs_hbm.at[pl.ds(pl.multiple_of(tile * (tm * SLAB), SLAB), tm * SLAB)]
            cp = pltpu.make_async_copy(zero_ref, dst, sem)
            cp.start()
            cp.wait()

        for e in range(N_EXPERTS):
            pl.when(zt_ref[e] >= 0)(functools.partial(clear, zt_ref[e]))

        def clear_unused(tile, carry):
            clear(tile)
            return carry

        lax.fori_loop(nu_ref[0], n_tiles, clear_unused, 0)

    def scatter(src_ref):
        def issue(r, carry):
            src = src_ref.at[pl.ds(pl.multiple_of(r * SLAB, SLAB), SLAB)]
            for k in range(TOP_K):
                dst = xs_hbm.at[pl.ds(pl.multiple_of(pos_ref[0, 0, r * TOP_K + k] * SLAB, SLAB), SLAB)]
                pltpu.make_async_copy(src, dst, sem).start(priority=k % 2)
            return carry

        lax.fori_loop(0, tt, issue, 0)

    pl.when(i < n_first)(lambda: scatter(xa_ref))
    pl.when(i >= n_first)(lambda: scatter(xb_ref))
    n_rows = tt * TOP_K * SLAB
    pltpu.make_async_copy(xs_hbm.at[pl.ds(0, n_rows)], xs_hbm.at[pl.ds(0, n_rows)], sem).wait()


def _dispatch(x1s_a, x1s_b, pos, zero_tiles, n_used, *, tt, tm, n_tiles):
    n_a = x1s_a.shape[0] // SLAB
    n_b = x1s_b.shape[0] // SLAB
    assert n_a % tt == 0 and n_b % tt == 0 and n_tiles * tm >= tt * TOP_K
    n_first = n_a // tt
    n_steps = (n_a + n_b) // tt
    return pl.pallas_call(
        functools.partial(_dispatch_kernel, tt, tm, n_first, n_tiles),
        grid_spec=pltpu.PrefetchScalarGridSpec(
            num_scalar_prefetch=2,
            grid=(n_steps,),
            in_specs=[
                pl.BlockSpec((1, 1, tt * TOP_K), lambda i, zt, nu: (i, 0, 0), memory_space=pltpu.SMEM),
                pl.BlockSpec((tt * SLAB, LANES), lambda i, zt, nu: (jnp.minimum(i, n_first - 1), 0)),
                pl.BlockSpec((tt * SLAB, LANES), lambda i, zt, nu: (jnp.maximum(i - n_first, 0), 0)),
            ],
            out_specs=pl.BlockSpec(memory_space=pl.ANY),
            scratch_shapes=[pltpu.VMEM((tm * SLAB, LANES), _F32), pltpu.SemaphoreType.DMA],
        ),
        out_shape=jax.ShapeDtypeStruct((n_tiles * tm * SLAB, LANES), _F32),
        compiler_params=pltpu.CompilerParams(dimension_semantics=("arbitrary",)),
        name="moe_dispatch",
    )(zero_tiles, n_used, pos.reshape(n_steps, 1, tt * TOP_K), x1s_a, x1s_b)


def _expert_kernel(tm, te_ref, nu_ref, xs_ref, wgu_ref, bgu_ref, wd_ref, bd_ref, ys_ref, wgu_b, wd_b):
    i = pl.program_id(0)

    @pl.when(i < nu_ref[0])
    def _tile():
        prev_e = te_ref[jnp.maximum(i - 1, 0)]

        @pl.when(jnp.logical_or(i == 0, te_ref[i] != prev_e))
        def _new_expert():
            wgu_b[...] = wgu_ref[0].astype(_BF16)
            wd_b[...] = wd_ref[0].astype(_BF16)

        x = _slab_rows(xs_ref, 0, tm).astype(_BF16)
        hgu = _dot(x, wgu_b[...]) + bgu_ref[0]
        glu = jnp.minimum(hgu[:, 0:D_FF], SWIGLU_LIMIT)
        lin = jnp.clip(hgu[:, D_FF:2 * D_FF], -SWIGLU_LIMIT, SWIGLU_LIMIT)
        hh = glu * _sigmoid(SWIGLU_ALPHA * glu) * (lin + 1.0)
        y = _dot(hh.astype(_BF16), wd_b[...]) + bd_ref[0]
        for c in range(SLAB):
            ys_ref[pl.ds(c, tm, stride=SLAB), :] = y[:, c * LANES:(c + 1) * LANES]

    @pl.when(i >= nu_ref[0])
    def _unused_tile():
        ys_ref[...] = jnp.zeros_like(ys_ref)


def _experts(xs, tile_expert, n_used, prm, *, tm):
    n_tiles = xs.shape[0] // (tm * SLAB)
    row_map = lambda i, te, nu: (jnp.minimum(i, nu[0] - 1), 0)
    out_map = lambda i, te, nu: (i, 0)
    exp_map = lambda i, te, nu: (te[i], 0, 0)
    return pl.pallas_call(
        functools.partial(_expert_kernel, tm),
        grid_spec=pltpu.PrefetchScalarGridSpec(
            num_scalar_prefetch=2,
            grid=(n_tiles,),
            in_specs=[
                pl.BlockSpec((tm * SLAB, LANES), row_map),
                pl.BlockSpec((1, D_MODEL, 2 * D_FF), exp_map),
                pl.BlockSpec((1, 1, 2 * D_FF), exp_map),
                pl.BlockSpec((1, D_FF, D_MODEL), exp_map),
                pl.BlockSpec((1, 1, D_MODEL), exp_map),
            ],
            out_specs=pl.BlockSpec((tm * SLAB, LANES), out_map),
            scratch_shapes=[pltpu.VMEM((D_MODEL, 2 * D_FF), _BF16), pltpu.VMEM((D_FF, D_MODEL), _BF16)],
        ),
        out_shape=jax.ShapeDtypeStruct(xs.shape, _F32),
        compiler_params=pltpu.CompilerParams(
            dimension_semantics=("arbitrary",), vmem_limit_bytes=VMEM_LIMIT_BYTES),
        name="moe_experts",
    )(tile_expert, n_used, xs, prm["wgu"], prm["bgu"], prm["wd"], prm["bd"])


def _combine_kernel(tt, n_first, n_steps, pos_ref, pos_next_ref, gate_ref, xa_ref, xb_ref, ys_hbm, g2_ref, b2_ref,
                    ya_ref, yb_ref, ybuf, sems):
    i = pl.program_id(0)
    n_rows = tt * TOP_K * SLAB

    def gather(p_ref, slot):
        def issue(r, carry):
            for k in range(TOP_K):
                src = ys_hbm.at[pl.ds(pl.multiple_of(p_ref[0, 0, r * TOP_K + k] * SLAB, SLAB), SLAB)]
                dst = ybuf.at[slot, pl.ds(pl.multiple_of((k * tt + r) * SLAB, SLAB), SLAB)]
                pltpu.make_async_copy(src, dst, sems.at[slot]).start(priority=k % 2)
            return carry

        lax.fori_loop(0, tt, issue, 0)

    pl.when(i == 0)(lambda: gather(pos_ref, 0))

    def tile(slot):
        pl.when(i + 1 < n_steps)(lambda: gather(pos_next_ref, 1 - slot))
        pltpu.make_async_copy(ys_hbm.at[pl.ds(0, n_rows)], ybuf.at[slot], sems.at[slot]).wait()
        rows_ref = ybuf.at[slot]
        gates = gate_ref[...]
        acc = gates[:, 0:1] * _slab_rows(rows_ref, 0, tt)
        for k in range(1, TOP_K):
            acc = acc + gates[:, k:k + 1] * _slab_rows(rows_ref, k * tt, tt)

        def finish(x_ref, y_ref):
            x1 = _slab_rows(x_ref, 0, tt)
            y_ref[...] = _layer_norm(DN_ALPHA * x1 + acc, g2_ref[...], b2_ref[...])

        pl.when(i < n_first)(lambda: finish(xa_ref, ya_ref))
        pl.when(i >= n_first)(lambda: finish(xb_ref, yb_ref))

    for slot in range(2):
        pl.when(i % 2 == slot)(functools.partial(tile, slot))


def _combine(x1s_a, x1s_b, ys, pos, gates, prm, *, tt):
    n_a = x1s_a.shape[0] // SLAB
    n_b = x1s_b.shape[0] // SLAB
    assert n_a % tt == 0 and n_b % tt == 0
    n_first = n_a // tt
    n_steps = (n_a + n_b) // tt
    a_map = lambda i: (jnp.minimum(i, n_first - 1), 0)
    b_map = lambda i: (jnp.maximum(i - n_first, 0), 0)
    pos3 = pos.reshape(n_steps, 1, tt * TOP_K)
    return pl.pallas_call(
        functools.partial(_combine_kernel, tt, n_first, n_steps),
        grid=(n_steps,),
        in_specs=[
            pl.BlockSpec((1, 1, tt * TOP_K), lambda i: (i, 0, 0), memory_space=pltpu.SMEM),
            pl.BlockSpec((1, 1, tt * TOP_K), lambda i: (jnp.minimum(i + 1, n_steps - 1), 0, 0),
                         memory_space=pltpu.SMEM),
            pl.BlockSpec((tt, TOP_K), lambda i: (i, 0)),
            pl.BlockSpec((tt * SLAB, LANES), a_map),
            pl.BlockSpec((tt * SLAB, LANES), b_map),
            pl.BlockSpec(memory_space=pl.ANY),
            pl.BlockSpec((1, D_MODEL), lambda i: (0, 0)),
            pl.BlockSpec((1, D_MODEL), lambda i: (0, 0)),
        ],
        out_specs=(pl.BlockSpec((tt, D_MODEL), a_map), pl.BlockSpec((tt, D_MODEL), b_map)),
        out_shape=(jax.ShapeDtypeStruct((n_a, D_MODEL), _F32), jax.ShapeDtypeStruct((n_b, D_MODEL), _F32)),
        scratch_shapes=[pltpu.VMEM((2, TOP_K * tt * SLAB, LANES), _F32), pltpu.SemaphoreType.DMA((2,))],
        compiler_params=pltpu.CompilerParams(
            dimension_semantics=("arbitrary",), vmem_limit_bytes=VMEM_LIMIT_BYTES),
        name="moe_combine",
    )(pos3, pos3, gates, x1s_a, x1s_b, ys, prm["g2"], prm["b2"])


def _route(meta_a, meta_b, gate_a, gate_b, counts_a, counts_b, tm, n_tiles):
    meta = jnp.concatenate([meta_a, meta_b], axis=1).T
    idx = meta[:, 0:TOP_K]
    rank = meta[:, TOP_K:2 * TOP_K]
    gates = jnp.concatenate([gate_a[0:TOP_K], gate_b[0:TOP_K]], axis=1).T
    cp = counts_a.reshape(-1).astype(jnp.int32)
    cs = counts_b.reshape(-1).astype(jnp.int32)
    is_b = (jnp.arange(meta.shape[0]) >= meta_a.shape[1])[:, None]
    rank = rank + jnp.where(is_b, cp[idx], 0)
    tiles_per = (cp + cs + tm - 1) // tm
    tile_end = jnp.cumsum(tiles_per)
    tile_start = tile_end - tiles_per
    pos = tile_start[idx] * tm + rank
    n_used = tile_end[-1:]
    tile_ids = jnp.arange(n_tiles, dtype=jnp.int32)
    tile_expert = jnp.minimum(jnp.sum(tile_ids[:, None] >= tile_end[None, :], axis=1), N_EXPERTS - 1)
    last_e = tile_expert[jnp.maximum(n_used[0] - 1, 0)]
    tile_expert = jnp.where(tile_ids < n_used[0], tile_expert, last_e).astype(jnp.int32)
    zero_tiles = jnp.where(tiles_per > 0, tile_end - 1, -1).astype(jnp.int32)
    return pos.astype(jnp.int32), gates, tile_expert, n_used.astype(jnp.int32), zero_tiles


def _rope_tables(pos):
    half = ROT_DIM // 2
    inv_freq = ROPE_THETA ** (-jnp.arange(half, dtype=_F32) / half)
    ang = pos.astype(_F32)[:, None] * inv_freq[None, :]
    cos, sin = jnp.cos(ang), jnp.sin(ang)
    n = pos.shape[0]
    ones = jnp.ones((n, HEAD_DIM - ROT_DIM), _F32)
    zeros = jnp.zeros((n, HEAD_DIM - ROT_DIM), _F32)
    zh = jnp.zeros((n, half), _F32)
    cos_t = jnp.concatenate([cos, cos, ones], axis=1)
    sin_up = jnp.concatenate([-sin, zh, zeros], axis=1)
    sin_dn = jnp.concatenate([zh, sin, zeros], axis=1)
    tab = jnp.stack([cos_t, sin_up, sin_dn])
    return jnp.concatenate([tab, tab], axis=2)


def _block_diag(w):
    hd, d, _ = w.shape
    eye = jnp.eye(hd, dtype=w.dtype)
    return (eye[:, None, :, None] * w[:, :, None, :]).reshape(hd * d, hd * d)


def _gate_blocks(wa, wx):
    blocks = []
    head_of = np.arange(D_RNN) // RNN_HEAD_DIM
    for j, k0 in enumerate(GATE_STARTS):
        cols = np.arange(j * GATE_COLS, (j + 1) * GATE_COLS)
        reach = np.flatnonzero(np.isin(head_of, head_of[cols]))
        assert reach.min() >= k0 and reach.max() < k0 + GATE_K
        cs = slice(j * GATE_COLS, (j + 1) * GATE_COLS)
        blocks.append(jnp.concatenate([wa[k0:k0 + GATE_K, cs], wx[k0:k0 + GATE_K, cs]], axis=1))
    return jnp.stack(blocks)


def kernel(x_prompt, x_sample, cache_k, cache_v, state_conv, state_h, w_in, attn_sinks, w_attn_out, conv_w, conv_b, gate_a_w, gate_a_b, gate_x_w, gate_x_b, lru_lambda, w_rnn_out, w_out, ln1_g, ln1_b, router_w, router_b, w_gate_up, b_gate_up, w_down, b_down, ln2_g, ln2_b):
    assert w_in.shape[0] == DEPTH == 1
    l = 0
    bsz, seq, _ = x_prompt.shape
    dbsz, dseq, _ = x_sample.shape
    row = lambda a: a.reshape(1, -1)
    prm = {
        "w_in": w_in[l].astype(_BF16),
        "conv_w": conv_w[l], "conv_b": row(conv_b[l]),
        "wg": _gate_blocks(_block_diag(gate_a_w[l]), _block_diag(gate_x_w[l])).astype(_BF16),
        "bg": jnp.concatenate([gate_a_b[l].reshape(-1, 1, GATE_COLS), gate_x_b[l].reshape(-1, 1, GATE_COLS)], axis=2),
        "lam": row(lru_lambda[l]),
        "wao": w_attn_out[l].astype(_BF16), "wro": w_rnn_out[l].astype(_BF16), "wout": w_out[l].astype(_BF16),
        "g1": row(ln1_g[l]), "b1": row(ln1_b[l]),
        "rwt": router_w[l].T.astype(_BF16), "rb": router_b[l].reshape(-1, 1),
        "wgu": w_gate_up[l], "bgu": b_gate_up[l][:, None, :],
        "wd": w_down[l], "bd": b_down[l][:, None, :],
        "g2": row(ln2_g[l]), "b2": row(ln2_b[l]),
    }
    sinks = attn_sinks[l]

    rope_p = _rope_tables(jnp.arange(seq, dtype=jnp.int32))
    rope_s = _rope_tables(PAST_LEN + jnp.arange(dseq, dtype=jnp.int32))

    n_prompt = bsz * seq
    n_tok = n_prompt + dbsz * dseq
    zeros_kv = jnp.zeros((bsz, WINDOW, KV_DIM), _F32)
    x1_p, meta_p, gate_p, cnt_p, pk, pv, pc, ph = _mixer(
        x_prompt, rope_p, zeros_kv, zeros_kv, jnp.zeros((bsz, CONV_W - 1, D_RNN), _F32),
        jnp.zeros((bsz, 1, D_RNN), _F32), sinks, prm, has_past=False, nb=1, lt=min(MIXER_ROWS, seq), lq=CHUNK)
    x1_s, meta_s, gate_s, cnt_s, sk, sv, sc, sh = _mixer(
        x_sample, rope_s, cache_k[l].reshape(dbsz, WINDOW, KV_DIM), cache_v[l].reshape(dbsz, WINDOW, KV_DIM),
        state_conv[l], state_h[l][:, None, :], sinks, prm, has_past=True, nb=min(MIXER_ROWS // dseq, dbsz),
        lt=dseq, lq=dseq)

    tm = EXPERT_ROWS
    n_tiles = (n_tok * TOP_K + N_EXPERTS * (tm - 1)) // tm + 1
    pos, gates, tile_expert, n_used, zero_tiles = _route(meta_p, meta_s, gate_p, gate_s, cnt_p, cnt_s, tm, n_tiles)
    xs = _dispatch(x1_p, x1_s, pos, zero_tiles, n_used, tt=TOKEN_ROWS, tm=tm, n_tiles=n_tiles)
    ysort = _experts(xs, tile_expert, n_used, prm, tm=tm)
    yp, ys = _combine(x1_p, x1_s, ysort, pos, gates, prm, tt=TOKEN_ROWS)
    yp = yp.reshape(bsz, seq, D_MODEL)
    ys = ys.reshape(dbsz, dseq, D_MODEL)

    kv5 = lambda a: a.reshape(1, a.shape[0], WINDOW, N_KV_HEADS, HEAD_DIM)
    return (yp, ys, kv5(pk), kv5(pv), pc[None], ph.reshape(1, bsz, D_RNN),
            kv5(sk), kv5(sv), sc[None], sh.reshape(1, dbsz, D_RNN))
```

```python
import functools

import jax
import jax.numpy as jnp
import numpy as np
from jax import lax
from jax.experimental import pallas as pl
from jax.experimental.pallas import tpu as pltpu

D_MODEL = 1024
PAST_LEN = 1024
CHUNK = 64
N_HEADS = 16
N_KV_HEADS = 4
HEAD_DIM = 64
GROUP = N_HEADS // N_KV_HEADS
ROT_DIM = HEAD_DIM // 4
ROPE_THETA = 500000.0
WINDOW = 128
ATTN_SCALE = HEAD_DIM ** -0.5
NEG_INF = -1e30
D_RNN = 1280
RNN_HEADS = 16
RNN_HEAD_DIM = D_RNN // RNN_HEADS
CONV_W = 4
LRU_C = 8.0
N_EXPERTS = 32
TOP_K = 4
D_FF = 1024
SWIGLU_ALPHA = 1.702
SWIGLU_LIMIT = 7.0
LN_EPS = 1e-5
DEPTH = 1
DN_ALPHA = (2.0 * DEPTH) ** 0.25

Q_DIM = N_HEADS * HEAD_DIM
KV_DIM = N_KV_HEADS * HEAD_DIM
OFF_K = Q_DIM
OFF_V = OFF_K + KV_DIM
OFF_XR = OFF_V + KV_DIM
OFF_YR = OFF_XR + D_RNN
OFF_GA = OFF_YR + D_RNN
OFF_GR = OFF_GA + D_MODEL
IN_DIM = OFF_GR + D_MODEL

LANES = 128
SLAB = D_MODEL // LANES
MIXER_ROWS = 256
EXPERT_ROWS = 512
TOKEN_ROWS = 256
ATT_GROUP = 4
SCAN_S = 8
GATE_COLS = 256
GATE_K = 512
GATE_STARTS = tuple(min(max((j * GATE_COLS // RNN_HEAD_DIM) * RNN_HEAD_DIM // LANES * LANES, 0), D_RNN - GATE_K)
                    for j in range(D_RNN // GATE_COLS))
CONV_HDR = 8
VMEM_LIMIT_BYTES = 56 * 1024 * 1024

_BF16 = jnp.bfloat16
_F32 = jnp.float32


def _dot(a, b):
    return jnp.dot(a, b, preferred_element_type=_F32)


def _dot_nt(a, b):
    return lax.dot_general(a, b, (((1,), (1,)), ((), ())), preferred_element_type=_F32)


def _half_logistic(h):
    return 0.5 * jnp.tanh(h) + 0.5


def _layer_norm(x, g, b):
    mu = jnp.mean(x, axis=-1, keepdims=True)
    xc = x - mu
    var = jnp.mean(xc * xc, axis=-1, keepdims=True)
    return xc * lax.rsqrt(var + LN_EPS) * g + b


def _rope(x, cos_t, sin_up, sin_dn):
    return x * cos_t + pltpu.roll(x, LANES - ROT_DIM // 2, 1) * sin_up + pltpu.roll(x, ROT_DIM // 2, 1) * sin_dn


def _lo_hi(slab, h):
    blk = slab[:, (h // 2) * LANES:(h // 2 + 1) * LANES]
    lane = lax.broadcasted_iota(jnp.int32, blk.shape, 1)
    if h % 2 == 0:
        lo = jnp.where(lane < HEAD_DIM, blk, 0.0)
        hi = pltpu.roll(lo, HEAD_DIM, 1)
    else:
        hi = jnp.where(lane >= HEAD_DIM, blk, 0.0)
        lo = pltpu.roll(hi, HEAD_DIM, 1)
    return lo.astype(_BF16), hi.astype(_BF16)


def _mixer_kernel(has_past, nb, lt, lq,
                  x_ref, rope_ref, w_in_ref, k0_ref, v0_ref, c0_ref, h0_ref, sinkrows_ref,
                  convw_ref, convb_ref, wg_ref, bg_ref, lam_ref, wao_ref, wro_ref, wout_ref,
                  g1_ref, b1_ref, rwt_ref, rb_ref,
                  x1s_ref, meta_ref, gate_ref, cnt_ref, kn_ref, vn_ref, cn_ref, hn_ref,
                  kw_ref, vw_ref, xp_ref, xc_ref, a_ref, b_ref, hin_ref, hc_ref, oat_ref, run_ref):
    s_idx = pl.program_id(1)
    n_s = pl.num_programs(1)
    rows = nb * lt
    n_chunks = lt // lq

    @pl.when(jnp.logical_and(pl.program_id(0) == 0, s_idx == 0))
    def _init_counts():
        run_ref[...] = jnp.zeros_like(run_ref)

    @pl.when(s_idx == 0)
    def _init():
        for bi in range(nb):
            for h in range(N_KV_HEADS):
                klo, khi = _lo_hi(k0_ref[bi], h)
                vlo, vhi = _lo_hi(v0_ref[bi], h)
                kw_ref[bi, h, 0, 0:WINDOW, :] = klo
                kw_ref[bi, h, 1, 0:WINDOW, :] = khi
                vw_ref[bi, h, 0, 0:WINDOW, :] = vlo
                vw_ref[bi, h, 1, 0:WINDOW, :] = vhi
            xp_ref[bi, CONV_HDR - (CONV_W - 1):CONV_HDR, :] = c0_ref[bi]
            hc_ref[bi] = h0_ref[bi]

    x = x_ref[...].reshape(rows, D_MODEL)
    xb = x.astype(_BF16)

    cos_t = jnp.concatenate([rope_ref[0]] * nb, axis=0) if nb > 1 else rope_ref[0]
    sin_up = jnp.concatenate([rope_ref[1]] * nb, axis=0) if nb > 1 else rope_ref[1]
    sin_dn = jnp.concatenate([rope_ref[2]] * nb, axis=0) if nb > 1 else rope_ref[2]

    zq = _dot(xb, w_in_ref[:, 0:Q_DIM])
    q_blocks = []
    for c in range(Q_DIM // LANES):
        qr = _rope(zq[:, c * LANES:(c + 1) * LANES], cos_t, sin_up, sin_dn)
        q_blocks.append((qr * ATTN_SCALE).astype(_BF16))
    zk = _dot(xb, w_in_ref[:, OFF_K:OFF_V])
    k_rot = jnp.concatenate(
        [_rope(zk[:, c * LANES:(c + 1) * LANES], cos_t, sin_up, sin_dn) for c in range(KV_DIM // LANES)], axis=1)
    v_new = _dot(xb, w_in_ref[:, OFF_V:OFF_XR])

    for bi in range(nb):
        r0 = bi * lt
        for h in range(N_KV_HEADS):
            klo, khi = _lo_hi(k_rot[r0:r0 + lt], h)
            vlo, vhi = _lo_hi(v_new[r0:r0 + lt], h)
            kw_ref[bi, h, 0, WINDOW:WINDOW + lt, :] = klo
            kw_ref[bi, h, 1, WINDOW:WINDOW + lt, :] = khi
            vw_ref[bi, h, 0, WINDOW:WINDOW + lt, :] = vlo
            vw_ref[bi, h, 1, WINDOW:WINDOW + lt, :] = vhi

    for bi in range(nb):
        r0 = bi * lt
        if lt >= WINDOW:
            kn_ref[bi] = k_rot[r0 + lt - WINDOW:r0 + lt]
            vn_ref[bi] = v_new[r0 + lt - WINDOW:r0 + lt]
        else:
            kn_ref[bi, 0:WINDOW - lt, :] = k0_ref[bi, lt:WINDOW, :]
            kn_ref[bi, WINDOW - lt:WINDOW, :] = k_rot[r0:r0 + lt]
            vn_ref[bi, 0:WINDOW - lt, :] = v0_ref[bi, lt:WINDOW, :]
            vn_ref[bi, WINDOW - lt:WINDOW, :] = v_new[r0:r0 + lt]

    w2 = 2 * lq
    wk = 2 * WINDOW + w2
    r_all = N_KV_HEADS * 2 * lq
    is_a2 = lax.broadcasted_iota(jnp.int32, (r_all, w2), 1) < lq
    lane_lo = lax.broadcasted_iota(jnp.int32, (r_all, LANES), 1) < HEAD_DIM
    key_row = lax.broadcasted_iota(jnp.int32, (wk, LANES), 0)
    key_is_a = jnp.logical_or(key_row < WINDOW, jnp.logical_and(key_row >= 2 * WINDOW, key_row < 2 * WINDOW + lq))
    key_lane_lo = lax.broadcasted_iota(jnp.int32, (wk, LANES), 1) < HEAD_DIM
    head_ones = jnp.where(key_is_a == key_lane_lo, 1.0, 0.0).astype(_BF16)
    sink_rows = sinkrows_ref[...]
    first = s_idx == 0
    units = [(bi, j) for bi in range(nb) for j in range(n_chunks)]
    for g0 in range(0, len(units), ATT_GROUP):
        group = units[g0:g0 + ATT_GROUP]
        scores = []
        vmats = []
        for bi, j in group:
            p0 = j * lq
            o0 = WINDOW + j * lq
            q0 = bi * lt + j * lq
            for h in range(N_KV_HEADS):
                kmat = jnp.concatenate([kw_ref[bi, h, 0, p0:p0 + WINDOW, :], kw_ref[bi, h, 1, p0:p0 + WINDOW, :],
                                        kw_ref[bi, h, 0, o0:o0 + lq, :], kw_ref[bi, h, 1, o0:o0 + lq, :]], axis=0)
                vmat = jnp.concatenate([vw_ref[bi, h, 0, p0:p0 + WINDOW, :], vw_ref[bi, h, 1, p0:p0 + WINDOW, :],
                                        vw_ref[bi, h, 0, o0:o0 + lq, :], vw_ref[bi, h, 1, o0:o0 + lq, :]], axis=0)
                vmats.append(jnp.concatenate([vmat, head_ones], axis=1))
                qs = jnp.concatenate([q_blocks[2 * h][q0:q0 + lq], q_blocks[2 * h + 1][q0:q0 + lq]], axis=0)
                sc_h = _dot_nt(qs, kmat)
                if not has_past and j * lq < WINDOW:
                    col = lax.broadcasted_iota(jnp.int32, sc_h.shape, 1)
                    n_bad = WINDOW - j * lq
                    bad = jnp.logical_or(col < n_bad, jnp.logical_and(col >= WINDOW, col < WINDOW + n_bad))
                    sc_h = jnp.where(jnp.logical_and(first, bad), NEG_INF, sc_h)
                scores.append(sc_h)
        n_u = len(group)
        sc = jnp.concatenate(scores, axis=0)
        tile_rows = lambda a: jnp.concatenate([a] * n_u, axis=0) if n_u > 1 else a
        is_a2g, lane_log, sinks_g = tile_rows(is_a2), tile_rows(lane_lo), tile_rows(sink_rows)
        c0 = sc[:, 0:LANES]
        c1 = sc[:, LANES:2 * LANES]
        c2 = sc[:, 2 * LANES:wk]
        if w2 == LANES:
            ma = jnp.max(jnp.maximum(c0, jnp.where(is_a2g, c2, NEG_INF)), axis=1, keepdims=True)
            mb = jnp.max(jnp.maximum(c1, jnp.where(is_a2g, NEG_INF, c2)), axis=1, keepdims=True)
        else:
            ma = jnp.maximum(jnp.max(c0, axis=1, keepdims=True),
                             jnp.max(jnp.where(is_a2g, c2, NEG_INF), axis=1, keepdims=True))
            mb = jnp.maximum(jnp.max(c1, axis=1, keepdims=True),
                             jnp.max(jnp.where(is_a2g, NEG_INF, c2), axis=1, keepdims=True))
        ma = jnp.maximum(ma, sinks_g[:, 0:1])
        mb = jnp.maximum(mb, sinks_g[:, HEAD_DIM:HEAD_DIM + 1])
        probs = jnp.concatenate([jnp.exp(c0 - ma), jnp.exp(c1 - mb), jnp.exp(c2 - jnp.where(is_a2g, ma, mb))],
                                axis=1).astype(_BF16)
        sink_term = jnp.exp(sinks_g - jnp.where(lane_log, ma, mb))
        for u, (bi, j) in enumerate(group):
            q0 = bi * lt + j * lq
            for h in range(N_KV_HEADS):
                r0 = (u * N_KV_HEADS + h) * 2 * lq
                od = _dot(probs[r0:r0 + 2 * lq], vmats[u * N_KV_HEADS + h])
                out = od[:, 0:LANES] / (od[:, LANES:2 * LANES] + sink_term[r0:r0 + 2 * lq])
                for p in range(2):
                    oat_ref[q0:q0 + lq, (2 * h + p) * LANES:(2 * h + p + 1) * LANES] = (
                        out[p * lq:(p + 1) * lq].astype(_BF16))

    if lt >= WINDOW:
        @pl.when(s_idx + 1 < n_s)
        def _carry_kv():
            for bi in range(nb):
                for h in range(N_KV_HEADS):
                    for v in range(2):
                        kw_ref[bi, h, v, 0:WINDOW, :] = kw_ref[bi, h, v, lt:lt + WINDOW, :]
                        vw_ref[bi, h, v, 0:WINDOW, :] = vw_ref[bi, h, v, lt:lt + WINDOW, :]

    xr = _dot(xb, w_in_ref[:, OFF_XR:OFF_YR])
    for bi in range(nb):
        r0 = bi * lt
        xp_ref[bi, CONV_HDR:CONV_HDR + lt, :] = xr[r0:r0 + lt]
        acc = xp_ref[bi, CONV_HDR - 3:CONV_HDR - 3 + lt, :] * convw_ref[0:1, :]
        for t in range(1, CONV_W):
            acc = acc + xp_ref[bi, CONV_HDR - 3 + t:CONV_HDR - 3 + t + lt, :] * convw_ref[t:t + 1, :]
        xc_ref[r0:r0 + lt, :] = acc + convb_ref[...]
        tail = xp_ref[bi, CONV_HDR + lt - (CONV_W - 1):CONV_HDR + lt, :]
        cn_ref[bi] = tail
        xp_ref[bi, CONV_HDR - (CONV_W - 1):CONV_HDR, :] = tail

    xc = xc_ref[...]
    xcb = xc.astype(_BF16)
    lam = lam_ref[...]
    softplus_neg = jnp.maximum(-lam, 0.0) + jnp.log(1.0 + jnp.exp(-jnp.abs(lam)))
    for j in range(D_RNN // GATE_COLS):
        cs = slice(j * GATE_COLS, (j + 1) * GATE_COLS)
        g = _dot(xcb[:, GATE_STARTS[j]:GATE_STARTS[j] + GATE_K], wg_ref[j]) + bg_ref[j]
        i_gate = _half_logistic(g[:, GATE_COLS:2 * GATE_COLS])
        m_sp = (-0.5 * LRU_C) * softplus_neg[:, cs]
        log_a = jnp.tanh(g[:, 0:GATE_COLS]) * m_sp + m_sp
        a = jnp.exp(log_a)
        bv = jnp.sqrt(-jnp.tanh(log_a) * (1.0 + a * a)) * i_gate * xc[:, cs]
        for cc in range(GATE_COLS // LANES):
            c = j * (GATE_COLS // LANES) + cc
            a_ref[c] = a[:, cc * LANES:(cc + 1) * LANES]
            b_ref[c] = bv[:, cc * LANES:(cc + 1) * LANES]

    n_lb = D_RNN // LANES
    n_g = lt // SCAN_S
    for bi in range(nb):
        r0 = bi * lt
        a_tot, h_tot = [], []
        for c in range(n_lb):
            at = a_ref[c, pl.ds(r0, n_g, stride=SCAN_S), :]
            ht = b_ref[c, pl.ds(r0, n_g, stride=SCAN_S), :]
            for k in range(1, SCAN_S):
                ak = a_ref[c, pl.ds(r0 + k, n_g, stride=SCAN_S), :]
                ht = ak * ht + b_ref[c, pl.ds(r0 + k, n_g, stride=SCAN_S), :]
                at = ak * at
            a_tot.append(at)
            h_tot.append(ht)
        carry = [hc_ref[bi][:, c * LANES:(c + 1) * LANES] for c in range(n_lb)]
        for g in range(n_g):
            for c in range(n_lb):
                hin_ref[c, g:g + 1, :] = carry[c]
                carry[c] = a_tot[c][g:g + 1, :] * carry[c] + h_tot[c][g:g + 1, :]
        h_last = jnp.concatenate(carry, axis=1)
        hc_ref[bi] = h_last
        hn_ref[bi] = h_last
        for c in range(n_lb):
            hcur = hin_ref[c]
            for k in range(SCAN_S):
                hcur = a_ref[c, pl.ds(r0 + k, n_g, stride=SCAN_S), :] * hcur + b_ref[c, pl.ds(r0 + k, n_g, stride=SCAN_S), :]
                b_ref[c, pl.ds(r0 + k, n_g, stride=SCAN_S), :] = hcur

    yr = _dot(xb, w_in_ref[:, OFF_YR:OFF_GA])
    rnn = jnp.concatenate([b_ref[c] for c in range(n_lb)], axis=1) * jax.nn.gelu(yr)

    ga = _dot(xb, w_in_ref[:, OFF_GA:OFF_GR])
    gr = _dot(xb, w_in_ref[:, OFF_GR:IN_DIM])
    merged = (_half_logistic(ga) * _dot(oat_ref[...], wao_ref[...])
              + _half_logistic(gr) * _dot(rnn.astype(_BF16), wro_ref[...]))
    mix = _dot(merged.astype(_BF16), wout_ref[...])
    x1 = _layer_norm(DN_ALPHA * x + mix, g1_ref[...], b1_ref[...])
    for c in range(SLAB):
        x1s_ref[pl.ds(c, rows, stride=SLAB), :] = x1[:, c * LANES:(c + 1) * LANES]

    logits = _dot_nt(rwt_ref[...], x1.astype(_BF16)) + rb_ref[...]
    e_iota = lax.broadcasted_iota(jnp.int32, logits.shape, 0)
    work = logits
    top_vals = []
    top_idx = []
    top_sel = []
    for _ in range(TOP_K):
        m = jnp.max(work, axis=0, keepdims=True)
        idx = jnp.min(jnp.where(work == m, e_iota, N_EXPERTS), axis=0, keepdims=True)
        sel = e_iota == idx
        top_vals.append(m)
        top_idx.append(idx)
        top_sel.append(sel)
        work = jnp.where(sel, -jnp.inf, work)
    exps = [jnp.exp(v - top_vals[0]) for v in top_vals]
    denom = exps[0]
    for e in exps[1:]:
        denom = denom + e

    onehot = jnp.zeros_like(logits)
    for sel in top_sel:
        onehot = onehot + jnp.where(sel, 1.0, 0.0)
    r_i = lax.broadcasted_iota(jnp.int32, (rows, rows), 0)
    c_i = lax.broadcasted_iota(jnp.int32, (rows, rows), 1)
    earlier = jnp.where(r_i < c_i, 1.0, 0.0).astype(_BF16)
    before = _dot(onehot.astype(_BF16), earlier) + run_ref[...]
    run_ref[...] = run_ref[...] + jnp.sum(onehot, axis=1, keepdims=True)
    cnt_ref[...] = run_ref[...]

    sub_m = lax.broadcasted_iota(jnp.int32, (2 * TOP_K, rows), 0)
    meta_i = jnp.zeros((2 * TOP_K, rows), jnp.int32)
    meta_g = jnp.zeros((2 * TOP_K, rows), _F32)
    for k in range(TOP_K):
        rank_k = jnp.sum(jnp.where(top_sel[k], before, 0.0), axis=0, keepdims=True).astype(jnp.int32)
        meta_i = jnp.where(sub_m == k, top_idx[k], meta_i)
        meta_i = jnp.where(sub_m == TOP_K + k, rank_k, meta_i)
        meta_g = jnp.where(sub_m == k, exps[k] / denom, meta_g)
    meta_ref[...] = meta_i
    gate_ref[...] = meta_g


def _const_spec(shape):
    nd = len(shape)
    return pl.BlockSpec(shape, lambda b, s, _nd=nd: (0,) * _nd, pipeline_mode=pl.Buffered(1))


def _mixer(x, rope, k0, v0, c0, h0, sinks, prm, *, has_past, nb, lt, lq):
    bsz, seq, _ = x.shape
    assert bsz % nb == 0 and seq % lt == 0 and lt % lq == 0 and lq % 32 == 0
    assert lt >= WINDOW or seq == lt
    n_s = seq // lt
    rows = nb * lt
    n_tok = bsz * seq
    assert lt % SCAN_S == 0
    kernel = functools.partial(_mixer_kernel, has_past, nb, lt, lq)
    r_all = N_KV_HEADS * 2 * lq
    sink_rows = jnp.broadcast_to(sinks.reshape(N_KV_HEADS * 2, 1, 2, 1), (N_KV_HEADS * 2, lq, 2, HEAD_DIM))
    sink_rows = sink_rows.reshape(r_all, LANES)
    n_gate = D_RNN // GATE_COLS
    batch_spec = lambda shape: pl.BlockSpec((nb,) + shape, lambda b, s: (b,) + (0,) * len(shape))
    in_specs = [
        pl.BlockSpec((nb, lt, D_MODEL), lambda b, s: (b, s, 0)),
        pl.BlockSpec((3, lt, LANES), lambda b, s: (0, s, 0)),
        _const_spec((D_MODEL, IN_DIM)),
        batch_spec((WINDOW, KV_DIM)), batch_spec((WINDOW, KV_DIM)),
        batch_spec((CONV_W - 1, D_RNN)), batch_spec((1, D_RNN)),
        _const_spec((r_all, LANES)),
        _const_spec((CONV_W, D_RNN)), _const_spec((1, D_RNN)),
        _const_spec((n_gate, GATE_K, 2 * GATE_COLS)), _const_spec((n_gate, 1, 2 * GATE_COLS)), _const_spec((1, D_RNN)),
        _const_spec((Q_DIM, D_MODEL)), _const_spec((D_RNN, D_MODEL)), _const_spec((D_MODEL, D_MODEL)),
        _const_spec((1, D_MODEL)), _const_spec((1, D_MODEL)),
        _const_spec((N_EXPERTS, D_MODEL)), _const_spec((N_EXPERTS, 1)),
    ]
    tok_map = lambda b, s: (b * n_s + s, 0)
    tok_map_t = lambda b, s: (0, b * n_s + s)
    out_shape = (
        jax.ShapeDtypeStruct((n_tok * SLAB, LANES), _F32),
        jax.ShapeDtypeStruct((2 * TOP_K, n_tok), jnp.int32),
        jax.ShapeDtypeStruct((2 * TOP_K, n_tok), _F32),
        jax.ShapeDtypeStruct((N_EXPERTS, 1), _F32),
        jax.ShapeDtypeStruct((bsz, WINDOW, KV_DIM), _F32),
        jax.ShapeDtypeStruct((bsz, WINDOW, KV_DIM), _F32),
        jax.ShapeDtypeStruct((bsz, CONV_W - 1, D_RNN), _F32),
        jax.ShapeDtypeStruct((bsz, 1, D_RNN), _F32),
    )
    out_specs = (
        pl.BlockSpec((rows * SLAB, LANES), tok_map),
        pl.BlockSpec((2 * TOP_K, rows), tok_map_t),
        pl.BlockSpec((2 * TOP_K, rows), tok_map_t),
        pl.BlockSpec((N_EXPERTS, 1), lambda b, s: (0, 0)),
        batch_spec((WINDOW, KV_DIM)), batch_spec((WINDOW, KV_DIM)),
        batch_spec((CONV_W - 1, D_RNN)), batch_spec((1, D_RNN)),
    )
    scratch = [
        pltpu.VMEM((nb, N_KV_HEADS, 2, WINDOW + lt, LANES), _BF16),
        pltpu.VMEM((nb, N_KV_HEADS, 2, WINDOW + lt, LANES), _BF16),
        pltpu.VMEM((nb, CONV_HDR + lt, D_RNN), _F32),
        pltpu.VMEM((rows, D_RNN), _F32),
        pltpu.VMEM((D_RNN // LANES, rows, LANES), _F32),
        pltpu.VMEM((D_RNN // LANES, rows, LANES), _F32),
        pltpu.VMEM((D_RNN // LANES, lt // SCAN_S, LANES), _F32),
        pltpu.VMEM((nb, 1, D_RNN), _F32),
        pltpu.VMEM((rows, Q_DIM), _BF16),
        pltpu.VMEM((N_EXPERTS, 1), _F32),
    ]
    args = [x, rope, prm["w_in"], k0, v0, c0, h0, sink_rows, prm["conv_w"], prm["conv_b"], prm["wg"], prm["bg"],
            prm["lam"], prm["wao"], prm["wro"], prm["wout"], prm["g1"], prm["b1"], prm["rwt"], prm["rb"]]
    return pl.pallas_call(
        kernel,
        grid=(bsz // nb, n_s),
        in_specs=in_specs,
        out_specs=out_specs,
        out_shape=out_shape,
        scratch_shapes=scratch,
        compiler_params=pltpu.CompilerParams(
            dimension_semantics=("arbitrary", "arbitrary"), vmem_limit_bytes=VMEM_LIMIT_BYTES),
        name="mixer_past" if has_past else "mixer_prompt",
    )(*args)


def _slab_rows(ref, first_tok, n):
    return jnp.concatenate([ref[pl.ds(first_tok * SLAB + c, n, stride=SLAB), :] for c in range(SLAB)], axis=1)


def _dispatch_kernel(tt, tm, n_first, n_tiles, zt_ref, nu_ref, pos_ref, xa_ref, xb_ref, xs_hbm, zero_ref, sem):
    i = pl.program_id(0)

    @pl.when(i == 0)
    def _zero_padding():
        zero_ref[...] = jnp.zeros_like(zero_ref)

        def clear(tile):
            dst = xs_hbm.at[pl.ds(pl.multiple_of(tile * (tm * SLAB), SLAB), tm * SLAB)]
            cp = pltpu.make_async_copy(zero_ref, dst, sem)
            cp.start()
            cp.wait()

        for e in range(N_EXPERTS):
            pl.when(zt_ref[e] >= 0)(functools.partial(clear, zt_ref[e]))

        def clear_unused(tile, carry):
            clear(tile)
            return carry

        lax.fori_loop(nu_ref[0], n_tiles, clear_unused, 0)

    def scatter(src_ref):
        def issue(r, carry):
            src = src_ref.at[pl.ds(pl.multiple_of(r * SLAB, SLAB), SLAB)]
            for k in range(TOP_K):
                dst = xs_hbm.at[pl.ds(pl.multiple_of(pos_ref[0, 0, r * TOP_K + k] * SLAB, SLAB), SLAB)]
                pltpu.make_async_copy(src, dst, sem).start(priority=k % 2)
            return carry

        lax.fori_loop(0, tt, issue, 0)

    pl.when(i < n_first)(lambda: scatter(xa_ref))
    pl.when(i >= n_first)(lambda: scatter(xb_ref))
    n_rows = tt * TOP_K * SLAB
    pltpu.make_async_copy(xs_hbm.at[pl.ds(0, n_rows)], xs_hbm.at[pl.ds(0, n_rows)], sem).wait()


def _dispatch(x1s_a, x1s_b, pos, zero_tiles, n_used, *, tt, tm, n_tiles):
    n_a = x1s_a.shape[0] // SLAB
    n_b = x1s_b.shape[0] // SLAB
    assert n_a % tt == 0 and n_b % tt == 0 and n_tiles * tm >= tt * TOP_K
    n_first = n_a // tt
    n_steps = (n_a + n_b) // tt
    return pl.pallas_call(
        functools.partial(_dispatch_kernel, tt, tm, n_first, n_tiles),
        grid_spec=pltpu.PrefetchScalarGridSpec(
            num_scalar_prefetch=2,
            grid=(n_steps,),
            in_specs=[
                pl.BlockSpec((1, 1, tt * TOP_K), lambda i, zt, nu: (i, 0, 0), memory_space=pltpu.SMEM),
                pl.BlockSpec((tt * SLAB, LANES), lambda i, zt, nu: (jnp.minimum(i, n_first - 1), 0)),
                pl.BlockSpec((tt * SLAB, LANES), lambda i, zt, nu: (jnp.maximum(i - n_first, 0), 0)),
            ],
            out_specs=pl.BlockSpec(memory_space=pl.ANY),
            scratch_shapes=[pltpu.VMEM((tm * SLAB, LANES), _F32), pltpu.SemaphoreType.DMA],
        ),
        out_shape=jax.ShapeDtypeStruct((n_tiles * tm * SLAB, LANES), _F32),
        compiler_params=pltpu.CompilerParams(dimension_semantics=("arbitrary",)),
        name="moe_dispatch",
    )(zero_tiles, n_used, pos.reshape(n_steps, 1, tt * TOP_K), x1s_a, x1s_b)


def _expert_kernel(tm, te_ref, nu_ref, xs_ref, wgu_ref, bgu_ref, wd_ref, bd_ref, ys_ref, wgu_b, wd_b):
    i = pl.program_id(0)

    @pl.when(i < nu_ref[0])
    def _tile():
        prev_e = te_ref[jnp.maximum(i - 1, 0)]

        @pl.when(jnp.logical_or(i == 0, te_ref[i] != prev_e))
        def _new_expert():
            wgu_b[...] = wgu_ref[0].astype(_BF16)
            wd_b[...] = wd_ref[0].astype(_BF16)

        x = _slab_rows(xs_ref, 0, tm).astype(_BF16)
        hgu = _dot(x, wgu_b[...]) + bgu_ref[0]
        glu = jnp.minimum(hgu[:, 0:D_FF], SWIGLU_LIMIT)
        lin = jnp.clip(hgu[:, D_FF:2 * D_FF], -SWIGLU_LIMIT, SWIGLU_LIMIT)
        hh = glu * _half_logistic((0.5 * SWIGLU_ALPHA) * glu) * (lin + 1.0)
        y = _dot(hh.astype(_BF16), wd_b[...]) + bd_ref[0]
        for c in range(SLAB):
            ys_ref[pl.ds(c, tm, stride=SLAB), :] = y[:, c * LANES:(c + 1) * LANES]

    @pl.when(i >= nu_ref[0])
    def _unused_tile():
        ys_ref[...] = jnp.zeros_like(ys_ref)


def _experts(xs, tile_expert, n_used, prm, *, tm):
    n_tiles = xs.shape[0] // (tm * SLAB)
    row_map = lambda i, te, nu: (jnp.minimum(i, nu[0] - 1), 0)
    out_map = lambda i, te, nu: (i, 0)
    exp_map = lambda i, te, nu: (te[i], 0, 0)
    return pl.pallas_call(
        functools.partial(_expert_kernel, tm),
        grid_spec=pltpu.PrefetchScalarGridSpec(
            num_scalar_prefetch=2,
            grid=(n_tiles,),
            in_specs=[
                pl.BlockSpec((tm * SLAB, LANES), row_map),
                pl.BlockSpec((1, D_MODEL, 2 * D_FF), exp_map),
                pl.BlockSpec((1, 1, 2 * D_FF), exp_map),
                pl.BlockSpec((1, D_FF, D_MODEL), exp_map),
                pl.BlockSpec((1, 1, D_MODEL), exp_map),
            ],
            out_specs=pl.BlockSpec((tm * SLAB, LANES), out_map),
            scratch_shapes=[pltpu.VMEM((D_MODEL, 2 * D_FF), _BF16), pltpu.VMEM((D_FF, D_MODEL), _BF16)],
        ),
        out_shape=jax.ShapeDtypeStruct(xs.shape, _F32),
        compiler_params=pltpu.CompilerParams(
            dimension_semantics=("arbitrary",), vmem_limit_bytes=VMEM_LIMIT_BYTES),
        name="moe_experts",
    )(tile_expert, n_used, xs, prm["wgu"], prm["bgu"], prm["wd"], prm["bd"])


def _combine_kernel(tt, n_first, n_steps, pos_ref, pos_next_ref, gate_ref, xa_ref, xb_ref, ys_hbm, g2_ref, b2_ref,
                    ya_ref, yb_ref, ybuf, sems):
    i = pl.program_id(0)
    n_rows = tt * TOP_K * SLAB

    def gather(p_ref, slot):
        def issue(r, carry):
            for k in range(TOP_K):
                src = ys_hbm.at[pl.ds(pl.multiple_of(p_ref[0, 0, r * TOP_K + k] * SLAB, SLAB), SLAB)]
                dst = ybuf.at[slot, pl.ds(pl.multiple_of((k * tt + r) * SLAB, SLAB), SLAB)]
                pltpu.make_async_copy(src, dst, sems.at[slot]).start(priority=k % 2)
            return carry

        lax.fori_loop(0, tt, issue, 0)

    pl.when(i == 0)(lambda: gather(pos_ref, 0))

    def tile(slot):
        pl.when(i + 1 < n_steps)(lambda: gather(pos_next_ref, 1 - slot))
        pltpu.make_async_copy(ys_hbm.at[pl.ds(0, n_rows)], ybuf.at[slot], sems.at[slot]).wait()
        rows_ref = ybuf.at[slot]
        gates = gate_ref[...]
        acc = gates[:, 0:1] * _slab_rows(rows_ref, 0, tt)
        for k in range(1, TOP_K):
            acc = acc + gates[:, k:k + 1] * _slab_rows(rows_ref, k * tt, tt)

        def finish(x_ref, y_ref):
            x1 = _slab_rows(x_ref, 0, tt)
            y_ref[...] = _layer_norm(DN_ALPHA * x1 + acc, g2_ref[...], b2_ref[...])

        pl.when(i < n_first)(lambda: finish(xa_ref, ya_ref))
        pl.when(i >= n_first)(lambda: finish(xb_ref, yb_ref))

    for slot in range(2):
        pl.when(i % 2 == slot)(functools.partial(tile, slot))


def _combine(x1s_a, x1s_b, ys, pos, gates, prm, *, tt):
    n_a = x1s_a.shape[0] // SLAB
    n_b = x1s_b.shape[0] // SLAB
    assert n_a % tt == 0 and n_b % tt == 0
    n_first = n_a // tt
    n_steps = (n_a + n_b) // tt
    a_map = lambda i: (jnp.minimum(i, n_first - 1), 0)
    b_map = lambda i: (jnp.maximum(i - n_first, 0), 0)
    pos3 = pos.reshape(n_steps, 1, tt * TOP_K)
    return pl.pallas_call(
        functools.partial(_combine_kernel, tt, n_first, n_steps),
        grid=(n_steps,),
        in_specs=[
            pl.BlockSpec((1, 1, tt * TOP_K), lambda i: (i, 0, 0), memory_space=pltpu.SMEM),
            pl.BlockSpec((1, 1, tt * TOP_K), lambda i: (jnp.minimum(i + 1, n_steps - 1), 0, 0),
                         memory_space=pltpu.SMEM),
            pl.BlockSpec((tt, TOP_K), lambda i: (i, 0)),
            pl.BlockSpec((tt * SLAB, LANES), a_map),
            pl.BlockSpec((tt * SLAB, LANES), b_map),
            pl.BlockSpec(memory_space=pl.ANY),
            pl.BlockSpec((1, D_MODEL), lambda i: (0, 0)),
            pl.BlockSpec((1, D_MODEL), lambda i: (0, 0)),
        ],
        out_specs=(pl.BlockSpec((tt, D_MODEL), a_map), pl.BlockSpec((tt, D_MODEL), b_map)),
        out_shape=(jax.ShapeDtypeStruct((n_a, D_MODEL), _F32), jax.ShapeDtypeStruct((n_b, D_MODEL), _F32)),
        scratch_shapes=[pltpu.VMEM((2, TOP_K * tt * SLAB, LANES), _F32), pltpu.SemaphoreType.DMA((2,))],
        compiler_params=pltpu.CompilerParams(
            dimension_semantics=("arbitrary",), vmem_limit_bytes=VMEM_LIMIT_BYTES),
        name="moe_combine",
    )(pos3, pos3, gates, x1s_a, x1s_b, ys, prm["g2"], prm["b2"])


def _route(meta_a, meta_b, gate_a, gate_b, counts_a, counts_b, tm, n_tiles):
    meta = jnp.concatenate([meta_a, meta_b], axis=1).T
    idx = meta[:, 0:TOP_K]
    rank = meta[:, TOP_K:2 * TOP_K]
    gates = jnp.concatenate([gate_a[0:TOP_K], gate_b[0:TOP_K]], axis=1).T
    cp = counts_a.reshape(-1).astype(jnp.int32)
    cs = counts_b.reshape(-1).astype(jnp.int32)
    is_b = (jnp.arange(meta.shape[0]) >= meta_a.shape[1])[:, None]
    rank = rank + jnp.where(is_b, cp[idx], 0)
    tiles_per = (cp + cs + tm - 1) // tm
    tile_end = jnp.cumsum(tiles_per)
    tile_start = tile_end - tiles_per
    pos = tile_start[idx] * tm + rank
    n_used = tile_end[-1:]
    tile_ids = jnp.arange(n_tiles, dtype=jnp.int32)
    tile_expert = jnp.minimum(jnp.sum(tile_ids[:, None] >= tile_end[None, :], axis=1), N_EXPERTS - 1)
    last_e = tile_expert[jnp.maximum(n_used[0] - 1, 0)]
    tile_expert = jnp.where(tile_ids < n_used[0], tile_expert, last_e).astype(jnp.int32)
    zero_tiles = jnp.where(tiles_per > 0, tile_end - 1, -1).astype(jnp.int32)
    return pos.astype(jnp.int32), gates, tile_expert, n_used.astype(jnp.int32), zero_tiles


def _rope_tables(pos):
    half = ROT_DIM // 2
    inv_freq = ROPE_THETA ** (-jnp.arange(half, dtype=_F32) / half)
    ang = pos.astype(_F32)[:, None] * inv_freq[None, :]
    cos, sin = jnp.cos(ang), jnp.sin(ang)
    n = pos.shape[0]
    ones = jnp.ones((n, HEAD_DIM - ROT_DIM), _F32)
    zeros = jnp.zeros((n, HEAD_DIM - ROT_DIM), _F32)
    zh = jnp.zeros((n, half), _F32)
    cos_t = jnp.concatenate([cos, cos, ones], axis=1)
    sin_up = jnp.concatenate([-sin, zh, zeros], axis=1)
    sin_dn = jnp.concatenate([zh, sin, zeros], axis=1)
    tab = jnp.stack([cos_t, sin_up, sin_dn])
    return jnp.concatenate([tab, tab], axis=2)


def _block_diag(w):
    hd, d, _ = w.shape
    eye = jnp.eye(hd, dtype=w.dtype)
    return (eye[:, None, :, None] * w[:, :, None, :]).reshape(hd * d, hd * d)


def _gate_blocks(wa, wx):
    blocks = []
    head_of = np.arange(D_RNN) // RNN_HEAD_DIM
    for j, k0 in enumerate(GATE_STARTS):
        cols = np.arange(j * GATE_COLS, (j + 1) * GATE_COLS)
        reach = np.flatnonzero(np.isin(head_of, head_of[cols]))
        assert reach.min() >= k0 and reach.max() < k0 + GATE_K
        cs = slice(j * GATE_COLS, (j + 1) * GATE_COLS)
        blocks.append(jnp.concatenate([wa[k0:k0 + GATE_K, cs], wx[k0:k0 + GATE_K, cs]], axis=1))
    return jnp.stack(blocks)


def kernel(x_prompt, x_sample, cache_k, cache_v, state_conv, state_h, w_in, attn_sinks, w_attn_out, conv_w, conv_b, gate_a_w, gate_a_b, gate_x_w, gate_x_b, lru_lambda, w_rnn_out, w_out, ln1_g, ln1_b, router_w, router_b, w_gate_up, b_gate_up, w_down, b_down, ln2_g, ln2_b):
    assert w_in.shape[0] == DEPTH == 1
    l = 0
    bsz, seq, _ = x_prompt.shape
    dbsz, dseq, _ = x_sample.shape
    row = lambda a: a.reshape(1, -1)
    prm = {
        "w_in": (w_in[l] * jnp.where(jnp.arange(IN_DIM) >= OFF_GA, 0.5, 1.0)).astype(_BF16),
        "conv_w": conv_w[l], "conv_b": row(conv_b[l]),
        "wg": (0.5 * _gate_blocks(_block_diag(gate_a_w[l]), _block_diag(gate_x_w[l]))).astype(_BF16),
        "bg": 0.5 * jnp.concatenate([gate_a_b[l].reshape(-1, 1, GATE_COLS),
                                     gate_x_b[l].reshape(-1, 1, GATE_COLS)], axis=2),
        "lam": row(lru_lambda[l]),
        "wao": w_attn_out[l].astype(_BF16), "wro": w_rnn_out[l].astype(_BF16), "wout": w_out[l].astype(_BF16),
        "g1": row(ln1_g[l]), "b1": row(ln1_b[l]),
        "rwt": router_w[l].T.astype(_BF16), "rb": router_b[l].reshape(-1, 1),
        "wgu": w_gate_up[l], "bgu": b_gate_up[l][:, None, :],
        "wd": w_down[l], "bd": b_down[l][:, None, :],
        "g2": row(ln2_g[l]), "b2": row(ln2_b[l]),
    }
    sinks = attn_sinks[l]

    rope_p = _rope_tables(jnp.arange(seq, dtype=jnp.int32))
    rope_s = _rope_tables(PAST_LEN + jnp.arange(dseq, dtype=jnp.int32))

    n_prompt = bsz * seq
    n_tok = n_prompt + dbsz * dseq
    zeros_kv = jnp.zeros((bsz, WINDOW, KV_DIM), _F32)
    x1_p, meta_p, gate_p, cnt_p, pk, pv, pc, ph = _mixer(
        x_prompt, rope_p, zeros_kv, zeros_kv, jnp.zeros((bsz, CONV_W - 1, D_RNN), _F32),
        jnp.zeros((bsz, 1, D_RNN), _F32), sinks, prm, has_past=False, nb=1, lt=min(MIXER_ROWS, seq), lq=CHUNK)
    x1_s, meta_s, gate_s, cnt_s, sk, sv, sc, sh = _mixer(
        x_sample, rope_s, cache_k[l].reshape(dbsz, WINDOW, KV_DIM), cache_v[l].reshape(dbsz, WINDOW, KV_DIM),
        state_conv[l], state_h[l][:, None, :], sinks, prm, has_past=True, nb=min(MIXER_ROWS // dseq, dbsz),
        lt=dseq, lq=dseq)

    tm = EXPERT_ROWS
    n_tiles = (n_tok * TOP_K + N_EXPERTS * (tm - 1)) // tm + 1
    pos, gates, tile_expert, n_used, zero_tiles = _route(meta_p, meta_s, gate_p, gate_s, cnt_p, cnt_s, tm, n_tiles)
    xs = _dispatch(x1_p, x1_s, pos, zero_tiles, n_used, tt=TOKEN_ROWS, tm=tm, n_tiles=n_tiles)
    ysort = _experts(xs, tile_expert, n_used, prm, tm=tm)
    yp, ys = _combine(x1_p, x1_s, ysort, pos, gates, prm, tt=TOKEN_ROWS)
    yp = yp.reshape(bsz, seq, D_MODEL)
    ys = ys.reshape(dbsz, dseq, D_MODEL)

    kv5 = lambda a: a.reshape(1, a.shape[0], WINDOW, N_KV_HEADS, HEAD_DIM)
    return (yp, ys, kv5(pk), kv5(pv), pc[None], ph.reshape(1, bsz, D_RNN),
            kv5(sk), kv5(sv), sc[None], sh.reshape(1, dbsz, D_RNN))
```

```python
import functools

import jax
import jax.numpy as jnp
import numpy as np
from jax import lax
from jax.experimental import pallas as pl
from jax.experimental.pallas import tpu as pltpu

D_MODEL = 1024
PAST_LEN = 1024
CHUNK = 64
N_HEADS = 16
N_KV_HEADS = 4
HEAD_DIM = 64
GROUP = N_HEADS // N_KV_HEADS
ROT_DIM = HEAD_DIM // 4
ROPE_THETA = 500000.0
WINDOW = 128
ATTN_SCALE = HEAD_DIM ** -0.5
NEG_INF = -1e30
D_RNN = 1280
RNN_HEADS = 16
RNN_HEAD_DIM = D_RNN // RNN_HEADS
CONV_W = 4
LRU_C = 8.0
N_EXPERTS = 32
TOP_K = 4
D_FF = 1024
SWIGLU_ALPHA = 1.702
SWIGLU_LIMIT = 7.0
LN_EPS = 1e-5
DEPTH = 1
DN_ALPHA = (2.0 * DEPTH) ** 0.25

Q_DIM = N_HEADS * HEAD_DIM
KV_DIM = N_KV_HEADS * HEAD_DIM
OFF_K = Q_DIM
OFF_V = OFF_K + KV_DIM
OFF_XR = OFF_V + KV_DIM
OFF_YR = OFF_XR + D_RNN
OFF_GA = OFF_YR + D_RNN
OFF_GR = OFF_GA + D_MODEL
IN_DIM = OFF_GR + D_MODEL

LANES = 128
SLAB = D_MODEL // LANES
MIXER_ROWS = 256
EXPERT_ROWS = 512
TOKEN_ROWS = 256
ATT_GROUP = 4
SCAN_S = 8
GATE_COLS = 256
GATE_K = 512
GATE_STARTS = tuple(min(max((j * GATE_COLS // RNN_HEAD_DIM) * RNN_HEAD_DIM // LANES * LANES, 0), D_RNN - GATE_K)
                    for j in range(D_RNN // GATE_COLS))
CONV_HDR = 8
VMEM_LIMIT_BYTES = 56 * 1024 * 1024

_BF16 = jnp.bfloat16
_F32 = jnp.float32


def _dot(a, b):
    return jnp.dot(a, b, preferred_element_type=_F32)


def _dot_nt(a, b):
    return lax.dot_general(a, b, (((1,), (1,)), ((), ())), preferred_element_type=_F32)


def _half_logistic(h):
    return 0.5 * jnp.tanh(h) + 0.5


def _layer_norm(x, g, b):
    mu = jnp.mean(x, axis=-1, keepdims=True)
    xc = x - mu
    var = jnp.mean(xc * xc, axis=-1, keepdims=True)
    return xc * lax.rsqrt(var + LN_EPS) * g + b


def _rope(x, cos_t, sin_up, sin_dn):
    return x * cos_t + pltpu.roll(x, LANES - ROT_DIM // 2, 1) * sin_up + pltpu.roll(x, ROT_DIM // 2, 1) * sin_dn


def _lo_hi(slab, h):
    blk = slab[:, (h // 2) * LANES:(h // 2 + 1) * LANES]
    lane = lax.broadcasted_iota(jnp.int32, blk.shape, 1)
    if h % 2 == 0:
        lo = jnp.where(lane < HEAD_DIM, blk, 0.0)
        hi = pltpu.roll(lo, HEAD_DIM, 1)
    else:
        hi = jnp.where(lane >= HEAD_DIM, blk, 0.0)
        lo = pltpu.roll(hi, HEAD_DIM, 1)
    return lo.astype(_BF16), hi.astype(_BF16)


def _mixer_kernel(has_past, nb, lt, lq,
                  x_ref, rope_ref, w_in_ref, k0_ref, v0_ref, c0_ref, h0_ref, sinkrows_ref,
                  convw_ref, convb_ref, wg_ref, bg_ref, lam_ref, wao_ref, wro_ref, wout_ref,
                  g1_ref, b1_ref, rwt_ref, rb_ref,
                  x1s_ref, meta_ref, gate_ref, cnt_ref, kn_ref, vn_ref, cn_ref, hn_ref,
                  kw_ref, vw_ref, xp_ref, xc_ref, a_ref, b_ref, hin_ref, hc_ref, oat_ref, run_ref):
    s_idx = pl.program_id(1)
    n_s = pl.num_programs(1)
    rows = nb * lt
    n_chunks = lt // lq

    @pl.when(jnp.logical_and(pl.program_id(0) == 0, s_idx == 0))
    def _init_counts():
        run_ref[...] = jnp.zeros_like(run_ref)

    @pl.when(s_idx == 0)
    def _init():
        for bi in range(nb):
            for h in range(N_KV_HEADS):
                klo, khi = _lo_hi(k0_ref[bi], h)
                vlo, vhi = _lo_hi(v0_ref[bi], h)
                kw_ref[bi, h, 0, 0:WINDOW, :] = klo
                kw_ref[bi, h, 1, 0:WINDOW, :] = khi
                vw_ref[bi, h, 0, 0:WINDOW, :] = vlo
                vw_ref[bi, h, 1, 0:WINDOW, :] = vhi
            for c in range(D_RNN // LANES):
                xp_ref[bi, c, CONV_HDR - (CONV_W - 1):CONV_HDR, :] = c0_ref[bi, :, c * LANES:(c + 1) * LANES]
            hc_ref[bi] = h0_ref[bi]

    x = x_ref[...].reshape(rows, D_MODEL)
    xb = x.astype(_BF16)

    cos_t = jnp.concatenate([rope_ref[0]] * nb, axis=0) if nb > 1 else rope_ref[0]
    sin_up = jnp.concatenate([rope_ref[1]] * nb, axis=0) if nb > 1 else rope_ref[1]
    sin_dn = jnp.concatenate([rope_ref[2]] * nb, axis=0) if nb > 1 else rope_ref[2]

    zq = _dot(xb, w_in_ref[:, 0:Q_DIM])
    q_blocks = []
    for c in range(Q_DIM // LANES):
        qr = _rope(zq[:, c * LANES:(c + 1) * LANES], cos_t, sin_up, sin_dn)
        q_blocks.append((qr * ATTN_SCALE).astype(_BF16))
    zk = _dot(xb, w_in_ref[:, OFF_K:OFF_V])
    k_rot = jnp.concatenate(
        [_rope(zk[:, c * LANES:(c + 1) * LANES], cos_t, sin_up, sin_dn) for c in range(KV_DIM // LANES)], axis=1)
    v_new = _dot(xb, w_in_ref[:, OFF_V:OFF_XR])

    for bi in range(nb):
        r0 = bi * lt
        for h in range(N_KV_HEADS):
            klo, khi = _lo_hi(k_rot[r0:r0 + lt], h)
            vlo, vhi = _lo_hi(v_new[r0:r0 + lt], h)
            kw_ref[bi, h, 0, WINDOW:WINDOW + lt, :] = klo
            kw_ref[bi, h, 1, WINDOW:WINDOW + lt, :] = khi
            vw_ref[bi, h, 0, WINDOW:WINDOW + lt, :] = vlo
            vw_ref[bi, h, 1, WINDOW:WINDOW + lt, :] = vhi

    for bi in range(nb):
        r0 = bi * lt
        if lt >= WINDOW:
            kn_ref[bi] = k_rot[r0 + lt - WINDOW:r0 + lt]
            vn_ref[bi] = v_new[r0 + lt - WINDOW:r0 + lt]
        else:
            kn_ref[bi, 0:WINDOW - lt, :] = k0_ref[bi, lt:WINDOW, :]
            kn_ref[bi, WINDOW - lt:WINDOW, :] = k_rot[r0:r0 + lt]
            vn_ref[bi, 0:WINDOW - lt, :] = v0_ref[bi, lt:WINDOW, :]
            vn_ref[bi, WINDOW - lt:WINDOW, :] = v_new[r0:r0 + lt]

    w2 = 2 * lq
    wk = 2 * WINDOW + w2
    r_all = N_KV_HEADS * 2 * lq
    is_a2 = lax.broadcasted_iota(jnp.int32, (r_all, w2), 1) < lq
    lane_lo = lax.broadcasted_iota(jnp.int32, (r_all, LANES), 1) < HEAD_DIM
    key_row = lax.broadcasted_iota(jnp.int32, (wk, LANES), 0)
    key_is_a = jnp.logical_or(key_row < WINDOW, jnp.logical_and(key_row >= 2 * WINDOW, key_row < 2 * WINDOW + lq))
    key_lane_lo = lax.broadcasted_iota(jnp.int32, (wk, LANES), 1) < HEAD_DIM
    head_ones = jnp.where(key_is_a == key_lane_lo, 1.0, 0.0).astype(_BF16)
    sink_rows, sink_a_rows, sink_b_rows = sinkrows_ref[0], sinkrows_ref[1], sinkrows_ref[2]
    first = s_idx == 0
    units = [(bi, j) for bi in range(nb) for j in range(n_chunks)]
    for g0 in range(0, len(units), ATT_GROUP):
        group = units[g0:g0 + ATT_GROUP]
        scores = []
        vmats = []
        for bi, j in group:
            p0 = j * lq
            o0 = WINDOW + j * lq
            q0 = bi * lt + j * lq
            for h in range(N_KV_HEADS):
                kmat = jnp.concatenate([kw_ref[bi, h, 0, p0:p0 + WINDOW, :], kw_ref[bi, h, 1, p0:p0 + WINDOW, :],
                                        kw_ref[bi, h, 0, o0:o0 + lq, :], kw_ref[bi, h, 1, o0:o0 + lq, :]], axis=0)
                vmat = jnp.concatenate([vw_ref[bi, h, 0, p0:p0 + WINDOW, :], vw_ref[bi, h, 1, p0:p0 + WINDOW, :],
                                        vw_ref[bi, h, 0, o0:o0 + lq, :], vw_ref[bi, h, 1, o0:o0 + lq, :]], axis=0)
                vmats.append(jnp.concatenate([vmat, head_ones], axis=1))
                qs = jnp.concatenate([q_blocks[2 * h][q0:q0 + lq], q_blocks[2 * h + 1][q0:q0 + lq]], axis=0)
                sc_h = _dot_nt(qs, kmat)
                if not has_past and j * lq < WINDOW:
                    col = lax.broadcasted_iota(jnp.int32, sc_h.shape, 1)
                    n_bad = WINDOW - j * lq
                    bad = jnp.logical_or(col < n_bad, jnp.logical_and(col >= WINDOW, col < WINDOW + n_bad))
                    sc_h = jnp.where(jnp.logical_and(first, bad), NEG_INF, sc_h)
                scores.append(sc_h)
        n_u = len(group)
        sc = jnp.concatenate(scores, axis=0)
        tile_rows = lambda a: jnp.concatenate([a] * n_u, axis=0) if n_u > 1 else a
        is_a2g, lane_log, sinks_g = tile_rows(is_a2), tile_rows(lane_lo), tile_rows(sink_rows)
        sinks_ag, sinks_bg = tile_rows(sink_a_rows), tile_rows(sink_b_rows)
        c0 = sc[:, 0:LANES]
        c1 = sc[:, LANES:2 * LANES]
        c2 = sc[:, 2 * LANES:wk]
        if w2 == LANES:
            ma = jnp.max(jnp.maximum(c0, jnp.where(is_a2g, c2, NEG_INF)).astype(_BF16), axis=1, keepdims=True)
            mb = jnp.max(jnp.maximum(c1, jnp.where(is_a2g, NEG_INF, c2)).astype(_BF16), axis=1, keepdims=True)
        else:
            ma = jnp.maximum(jnp.max(c0.astype(_BF16), axis=1, keepdims=True),
                             jnp.max(jnp.where(is_a2g, c2, NEG_INF).astype(_BF16), axis=1, keepdims=True))
            mb = jnp.maximum(jnp.max(c1.astype(_BF16), axis=1, keepdims=True),
                             jnp.max(jnp.where(is_a2g, NEG_INF, c2).astype(_BF16), axis=1, keepdims=True))
        ma = jnp.maximum(jnp.broadcast_to(ma.astype(_F32), sinks_ag.shape), sinks_ag)
        mb = jnp.maximum(jnp.broadcast_to(mb.astype(_F32), sinks_bg.shape), sinks_bg)
        probs = jnp.concatenate([jnp.exp(c0 - ma), jnp.exp(c1 - mb),
                                 jnp.exp(c2 - jnp.where(is_a2g, ma[:, 0:w2], mb[:, 0:w2]))], axis=1).astype(_BF16)
        sink_term = jnp.exp(sinks_g - jnp.where(lane_log, ma, mb))
        for u, (bi, j) in enumerate(group):
            q0 = bi * lt + j * lq
            for h in range(N_KV_HEADS):
                r0 = (u * N_KV_HEADS + h) * 2 * lq
                od = _dot(probs[r0:r0 + 2 * lq], vmats[u * N_KV_HEADS + h])
                out = od[:, 0:LANES] / (od[:, LANES:2 * LANES] + sink_term[r0:r0 + 2 * lq])
                for p in range(2):
                    oat_ref[q0:q0 + lq, (2 * h + p) * LANES:(2 * h + p + 1) * LANES] = (
                        out[p * lq:(p + 1) * lq].astype(_BF16))

    if lt >= WINDOW:
        @pl.when(s_idx + 1 < n_s)
        def _carry_kv():
            for bi in range(nb):
                for h in range(N_KV_HEADS):
                    for v in range(2):
                        kw_ref[bi, h, v, 0:WINDOW, :] = kw_ref[bi, h, v, lt:lt + WINDOW, :]
                        vw_ref[bi, h, v, 0:WINDOW, :] = vw_ref[bi, h, v, lt:lt + WINDOW, :]

    xr = _dot(xb, w_in_ref[:, OFF_XR:OFF_YR])
    n_lb = D_RNN // LANES
    n_g = lt // SCAN_S
    conv_w = convw_ref[...]
    conv_b = convb_ref[...]
    for bi in range(nb):
        r0 = bi * lt
        tail = xr[r0 + lt - (CONV_W - 1):r0 + lt]
        cn_ref[bi] = tail
        for c in range(n_lb):
            ls = slice(c * LANES, (c + 1) * LANES)
            xp_ref[bi, c, CONV_HDR:CONV_HDR + lt, :] = xr[r0:r0 + lt, ls]
            frames = [xp_ref[bi, c, pl.ds(CONV_HDR + m, n_g, stride=SCAN_S), :]
                      for m in range(-(CONV_W - 1), SCAN_S)]
            for k in range(SCAN_S):
                acc = frames[k] * conv_w[0:1, ls]
                for t in range(1, CONV_W):
                    acc = acc + frames[k + t] * conv_w[t:t + 1, ls]
                xc_ref[c, pl.ds(r0 + k, n_g, stride=SCAN_S), :] = acc + conv_b[:, ls]
            xp_ref[bi, c, CONV_HDR - (CONV_W - 1):CONV_HDR, :] = tail[:, ls]

    xc = jnp.concatenate([xc_ref[c] for c in range(n_lb)], axis=1)
    xcb = xc.astype(_BF16)
    lam = lam_ref[...]
    softplus_neg = jnp.maximum(-lam, 0.0) + jnp.log(1.0 + jnp.exp(-jnp.abs(lam)))
    for j in range(D_RNN // GATE_COLS):
        cs = slice(j * GATE_COLS, (j + 1) * GATE_COLS)
        g = _dot(xcb[:, GATE_STARTS[j]:GATE_STARTS[j] + GATE_K], wg_ref[j]) + bg_ref[j]
        i_gate = _half_logistic(g[:, GATE_COLS:2 * GATE_COLS])
        m_sp = (-0.5 * LRU_C) * softplus_neg[:, cs]
        log_a = jnp.tanh(g[:, 0:GATE_COLS]) * m_sp + m_sp
        a = jnp.exp(log_a)
        bv = jnp.sqrt(-jnp.tanh(log_a) * (1.0 + a * a)) * i_gate * xc[:, cs]
        for cc in range(GATE_COLS // LANES):
            c = j * (GATE_COLS // LANES) + cc
            a_ref[c] = a[:, cc * LANES:(cc + 1) * LANES]
            b_ref[c] = bv[:, cc * LANES:(cc + 1) * LANES]

    for bi in range(nb):
        r0 = bi * lt
        a_tot, h_tot = [], []
        for c in range(n_lb):
            at = a_ref[c, pl.ds(r0, n_g, stride=SCAN_S), :]
            ht = b_ref[c, pl.ds(r0, n_g, stride=SCAN_S), :]
            for k in range(1, SCAN_S):
                ak = a_ref[c, pl.ds(r0 + k, n_g, stride=SCAN_S), :]
                ht = ak * ht + b_ref[c, pl.ds(r0 + k, n_g, stride=SCAN_S), :]
                at = ak * at
            a_tot.append(at)
            h_tot.append(ht)
        carry = [hc_ref[bi][:, c * LANES:(c + 1) * LANES] for c in range(n_lb)]
        for g in range(n_g):
            for c in range(n_lb):
                hin_ref[c, g:g + 1, :] = carry[c]
                carry[c] = a_tot[c][g:g + 1, :] * carry[c] + h_tot[c][g:g + 1, :]
        h_last = jnp.concatenate(carry, axis=1)
        hc_ref[bi] = h_last
        hn_ref[bi] = h_last
        for c in range(n_lb):
            hcur = hin_ref[c]
            for k in range(SCAN_S):
                hcur = a_ref[c, pl.ds(r0 + k, n_g, stride=SCAN_S), :] * hcur + b_ref[c, pl.ds(r0 + k, n_g, stride=SCAN_S), :]
                b_ref[c, pl.ds(r0 + k, n_g, stride=SCAN_S), :] = hcur

    yr = _dot(xb, w_in_ref[:, OFF_YR:OFF_GA])
    rnn = jnp.concatenate([b_ref[c] for c in range(n_lb)], axis=1) * jax.nn.gelu(yr)

    ga = _dot(xb, w_in_ref[:, OFF_GA:OFF_GR])
    gr = _dot(xb, w_in_ref[:, OFF_GR:IN_DIM])
    merged = (_half_logistic(ga) * _dot(oat_ref[...], wao_ref[...])
              + _half_logistic(gr) * _dot(rnn.astype(_BF16), wro_ref[...]))
    mix = _dot(merged.astype(_BF16), wout_ref[...])
    x1 = _layer_norm(DN_ALPHA * x + mix, g1_ref[...], b1_ref[...])
    for c in range(SLAB):
        x1s_ref[pl.ds(c, rows, stride=SLAB), :] = x1[:, c * LANES:(c + 1) * LANES]

    logits = _dot_nt(rwt_ref[...], x1.astype(_BF16)) + rb_ref[...]
    e_iota = lax.broadcasted_iota(jnp.int32, logits.shape, 0)
    work = logits
    top_vals = []
    top_idx = []
    top_sel = []
    for _ in range(TOP_K):
        m = jnp.max(work, axis=0, keepdims=True)
        idx = jnp.min(jnp.where(work == m, e_iota, N_EXPERTS), axis=0, keepdims=True)
        sel = e_iota == idx
        top_vals.append(m)
        top_idx.append(idx)
        top_sel.append(sel)
        work = jnp.where(sel, -jnp.inf, work)
    exps = [jnp.exp(v - top_vals[0]) for v in top_vals]
    denom = exps[0]
    for e in exps[1:]:
        denom = denom + e

    onehot = jnp.zeros_like(logits)
    for sel in top_sel:
        onehot = onehot + jnp.where(sel, 1.0, 0.0)
    r_i = lax.broadcasted_iota(jnp.int32, (rows, rows), 0)
    c_i = lax.broadcasted_iota(jnp.int32, (rows, rows), 1)
    earlier = jnp.where(r_i < c_i, 1.0, 0.0).astype(_BF16)
    before = _dot(onehot.astype(_BF16), earlier) + run_ref[...]
    run_ref[...] = run_ref[...] + jnp.sum(onehot, axis=1, keepdims=True)
    cnt_ref[...] = run_ref[...]

    sub_m = lax.broadcasted_iota(jnp.int32, (2 * TOP_K, rows), 0)
    meta_i = jnp.zeros((2 * TOP_K, rows), jnp.int32)
    meta_g = jnp.zeros((2 * TOP_K, rows), _F32)
    for k in range(TOP_K):
        rank_k = jnp.sum(jnp.where(top_sel[k], before, 0.0), axis=0, keepdims=True).astype(jnp.int32)
        meta_i = jnp.where(sub_m == k, top_idx[k], meta_i)
        meta_i = jnp.where(sub_m == TOP_K + k, rank_k, meta_i)
        meta_g = jnp.where(sub_m == k, exps[k] / denom, meta_g)
    meta_ref[...] = meta_i
    gate_ref[...] = meta_g


def _const_spec(shape):
    nd = len(shape)
    return pl.BlockSpec(shape, lambda b, s, _nd=nd: (0,) * _nd, pipeline_mode=pl.Buffered(1))


def _mixer(x, rope, k0, v0, c0, h0, sinks, prm, *, has_past, nb, lt, lq):
    bsz, seq, _ = x.shape
    assert bsz % nb == 0 and seq % lt == 0 and lt % lq == 0 and lq % 32 == 0
    assert lt >= WINDOW or seq == lt
    n_s = seq // lt
    rows = nb * lt
    n_tok = bsz * seq
    assert lt % SCAN_S == 0
    kernel = functools.partial(_mixer_kernel, has_past, nb, lt, lq)
    r_all = N_KV_HEADS * 2 * lq
    sink_rows = jnp.broadcast_to(sinks.reshape(N_KV_HEADS * 2, 1, 2, 1), (N_KV_HEADS * 2, lq, 2, HEAD_DIM))
    sink_rows = sink_rows.reshape(r_all, LANES)
    sink_ab = jnp.broadcast_to(sinks.reshape(N_KV_HEADS * 2, 1, 2, 1), (N_KV_HEADS * 2, lq, 2, LANES))
    sink_rows = jnp.stack([sink_rows, sink_ab[:, :, 0].reshape(r_all, LANES), sink_ab[:, :, 1].reshape(r_all, LANES)])
    n_gate = D_RNN // GATE_COLS
    batch_spec = lambda shape: pl.BlockSpec((nb,) + shape, lambda b, s: (b,) + (0,) * len(shape))
    in_specs = [
        pl.BlockSpec((nb, lt, D_MODEL), lambda b, s: (b, s, 0)),
        pl.BlockSpec((3, lt, LANES), lambda b, s: (0, s, 0)),
        _const_spec((D_MODEL, IN_DIM)),
        batch_spec((WINDOW, KV_DIM)), batch_spec((WINDOW, KV_DIM)),
        batch_spec((CONV_W - 1, D_RNN)), batch_spec((1, D_RNN)),
        _const_spec((3, r_all, LANES)),
        _const_spec((CONV_W, D_RNN)), _const_spec((1, D_RNN)),
        _const_spec((n_gate, GATE_K, 2 * GATE_COLS)), _const_spec((n_gate, 1, 2 * GATE_COLS)), _const_spec((1, D_RNN)),
        _const_spec((Q_DIM, D_MODEL)), _const_spec((D_RNN, D_MODEL)), _const_spec((D_MODEL, D_MODEL)),
        _const_spec((1, D_MODEL)), _const_spec((1, D_MODEL)),
        _const_spec((N_EXPERTS, D_MODEL)), _const_spec((N_EXPERTS, 1)),
    ]
    tok_map = lambda b, s: (b * n_s + s, 0)
    tok_map_t = lambda b, s: (0, b * n_s + s)
    out_shape = (
        jax.ShapeDtypeStruct((n_tok * SLAB, LANES), _F32),
        jax.ShapeDtypeStruct((2 * TOP_K, n_tok), jnp.int32),
        jax.ShapeDtypeStruct((2 * TOP_K, n_tok), _F32),
        jax.ShapeDtypeStruct((N_EXPERTS, 1), _F32),
        jax.ShapeDtypeStruct((bsz, WINDOW, KV_DIM), _F32),
        jax.ShapeDtypeStruct((bsz, WINDOW, KV_DIM), _F32),
        jax.ShapeDtypeStruct((bsz, CONV_W - 1, D_RNN), _F32),
        jax.ShapeDtypeStruct((bsz, 1, D_RNN), _F32),
    )
    out_specs = (
        pl.BlockSpec((rows * SLAB, LANES), tok_map),
        pl.BlockSpec((2 * TOP_K, rows), tok_map_t),
        pl.BlockSpec((2 * TOP_K, rows), tok_map_t),
        pl.BlockSpec((N_EXPERTS, 1), lambda b, s: (0, 0)),
        batch_spec((WINDOW, KV_DIM)), batch_spec((WINDOW, KV_DIM)),
        batch_spec((CONV_W - 1, D_RNN)), batch_spec((1, D_RNN)),
    )
    scratch = [
        pltpu.VMEM((nb, N_KV_HEADS, 2, WINDOW + lt, LANES), _BF16),
        pltpu.VMEM((nb, N_KV_HEADS, 2, WINDOW + lt, LANES), _BF16),
        pltpu.VMEM((nb, D_RNN // LANES, CONV_HDR + lt, LANES), _F32),
        pltpu.VMEM((D_RNN // LANES, rows, LANES), _F32),
        pltpu.VMEM((D_RNN // LANES, rows, LANES), _F32),
        pltpu.VMEM((D_RNN // LANES, rows, LANES), _F32),
        pltpu.VMEM((D_RNN // LANES, lt // SCAN_S, LANES), _F32),
        pltpu.VMEM((nb, 1, D_RNN), _F32),
        pltpu.VMEM((rows, Q_DIM), _BF16),
        pltpu.VMEM((N_EXPERTS, 1), _F32),
    ]
    args = [x, rope, prm["w_in"], k0, v0, c0, h0, sink_rows, prm["conv_w"], prm["conv_b"], prm["wg"], prm["bg"],
            prm["lam"], prm["wao"], prm["wro"], prm["wout"], prm["g1"], prm["b1"], prm["rwt"], prm["rb"]]
    return pl.pallas_call(
        kernel,
        grid=(bsz // nb, n_s),
        in_specs=in_specs,
        out_specs=out_specs,
        out_shape=out_shape,
        scratch_shapes=scratch,
        compiler_params=pltpu.CompilerParams(
            dimension_semantics=("arbitrary", "arbitrary"), vmem_limit_bytes=VMEM_LIMIT_BYTES),
        name="mixer_past" if has_past else "mixer_prompt",
    )(*args)


def _slab_rows(ref, first_tok, n):
    return jnp.concatenate([ref[pl.ds(first_tok * SLAB + c, n, stride=SLAB), :] for c in range(SLAB)], axis=1)


def _dispatch_kernel(tt, tm, n_first, n_tiles, zt_ref, nu_ref, pos_ref, xa_ref, xb_ref, xs_hbm, zero_ref, sem):
    i = pl.program_id(0)

    @pl.when(i == 0)
    def _zero_padding():
        zero_ref[...] = jnp.zeros_like(zero_ref)

        def clear(tile):
            dst = xs_hbm.at[pl.ds(pl.multiple_of(tile * (tm * SLAB), SLAB), tm * SLAB)]
            cp = pltpu.make_async_copy(zero_ref, dst, sem)
            cp.start()
            cp.wait()

        for e in range(N_EXPERTS):
            pl.when(zt_ref[e] >= 0)(functools.partial(clear, zt_ref[e]))

        def clear_unused(tile, carry):
            clear(tile)
            return carry

        lax.fori_loop(nu_ref[0], n_tiles, clear_unused, 0)

    def scatter(src_ref):
        def issue(r, carry):
            src = src_ref.at[pl.ds(pl.multiple_of(r * SLAB, SLAB), SLAB)]
            for k in range(TOP_K):
                dst = xs_hbm.at[pl.ds(pl.multiple_of(pos_ref[0, 0, r * TOP_K + k] * SLAB, SLAB), SLAB)]
                pltpu.make_async_copy(src, dst, sem).start(priority=k % 2)
            return carry

        lax.fori_loop(0, tt, issue, 0)

    pl.when(i < n_first)(lambda: scatter(xa_ref))
    pl.when(i >= n_first)(lambda: scatter(xb_ref))
    n_rows = tt * TOP_K * SLAB
    pltpu.make_async_copy(xs_hbm.at[pl.ds(0, n_rows)], xs_hbm.at[pl.ds(0, n_rows)], sem).wait()


def _dispatch(x1s_a, x1s_b, pos, zero_tiles, n_used, *, tt, tm, n_tiles):
    n_a = x1s_a.shape[0] // SLAB
    n_b = x1s_b.shape[0] // SLAB
    assert n_a % tt == 0 and n_b % tt == 0 and n_tiles * tm >= tt * TOP_K
    n_first = n_a // tt
    n_steps = (n_a + n_b) // tt
    return pl.pallas_call(
        functools.partial(_dispatch_kernel, tt, tm, n_first, n_tiles),
        grid_spec=pltpu.PrefetchScalarGridSpec(
            num_scalar_prefetch=2,
            grid=(n_steps,),
            in_specs=[
                pl.BlockSpec((1, 1, tt * TOP_K), lambda i, zt, nu: (i, 0, 0), memory_space=pltpu.SMEM),
                pl.BlockSpec((tt * SLAB, LANES), lambda i, zt, nu: (jnp.minimum(i, n_first - 1), 0)),
                pl.BlockSpec((tt * SLAB, LANES), lambda i, zt, nu: (jnp.maximum(i - n_first, 0), 0)),
            ],
            out_specs=pl.BlockSpec(memory_space=pl.ANY),
            scratch_shapes=[pltpu.VMEM((tm * SLAB, LANES), _F32), pltpu.SemaphoreType.DMA],
        ),
        out_shape=jax.ShapeDtypeStruct((n_tiles * tm * SLAB, LANES), _F32),
        compiler_params=pltpu.CompilerParams(dimension_semantics=("arbitrary",)),
        name="moe_dispatch",
    )(zero_tiles, n_used, pos.reshape(n_steps, 1, tt * TOP_K), x1s_a, x1s_b)


def _expert_kernel(tm, te_ref, nu_ref, xs_ref, wgu_ref, bgu_ref, wd_ref, bd_ref, ys_ref, wgu_b, wd_b):
    i = pl.program_id(0)

    @pl.when(i < nu_ref[0])
    def _tile():
        prev_e = te_ref[jnp.maximum(i - 1, 0)]

        @pl.when(jnp.logical_or(i == 0, te_ref[i] != prev_e))
        def _new_expert():
            wgu_b[...] = wgu_ref[0].astype(_BF16)
            wd_b[...] = wd_ref[0].astype(_BF16)

        x = _slab_rows(xs_ref, 0, tm).astype(_BF16)
        hgu = _dot(x, wgu_b[...]) + bgu_ref[0]
        glu = jnp.minimum(hgu[:, 0:D_FF], SWIGLU_LIMIT)
        lin = jnp.clip(hgu[:, D_FF:2 * D_FF], -SWIGLU_LIMIT, SWIGLU_LIMIT)
        hh = glu * _half_logistic((0.5 * SWIGLU_ALPHA) * glu) * (lin + 1.0)
        y = _dot(hh.astype(_BF16), wd_b[...]) + bd_ref[0]
        for c in range(SLAB):
            ys_ref[pl.ds(c, tm, stride=SLAB), :] = y[:, c * LANES:(c + 1) * LANES]

    @pl.when(i >= nu_ref[0])
    def _unused_tile():
        ys_ref[...] = jnp.zeros_like(ys_ref)


def _experts(xs, tile_expert, n_used, prm, *, tm):
    n_tiles = xs.shape[0] // (tm * SLAB)
    row_map = lambda i, te, nu: (jnp.minimum(i, nu[0] - 1), 0)
    out_map = lambda i, te, nu: (i, 0)
    exp_map = lambda i, te, nu: (te[i], 0, 0)
    return pl.pallas_call(
        functools.partial(_expert_kernel, tm),
        grid_spec=pltpu.PrefetchScalarGridSpec(
            num_scalar_prefetch=2,
            grid=(n_tiles,),
            in_specs=[
                pl.BlockSpec((tm * SLAB, LANES), row_map),
                pl.BlockSpec((1, D_MODEL, 2 * D_FF), exp_map),
                pl.BlockSpec((1, 1, 2 * D_FF), exp_map),
                pl.BlockSpec((1, D_FF, D_MODEL), exp_map),
                pl.BlockSpec((1, 1, D_MODEL), exp_map),
            ],
            out_specs=pl.BlockSpec((tm * SLAB, LANES), out_map),
            scratch_shapes=[pltpu.VMEM((D_MODEL, 2 * D_FF), _BF16), pltpu.VMEM((D_FF, D_MODEL), _BF16)],
        ),
        out_shape=jax.ShapeDtypeStruct(xs.shape, _F32),
        compiler_params=pltpu.CompilerParams(
            dimension_semantics=("arbitrary",), vmem_limit_bytes=VMEM_LIMIT_BYTES),
        name="moe_experts",
    )(tile_expert, n_used, xs, prm["wgu"], prm["bgu"], prm["wd"], prm["bd"])


def _combine_kernel(tt, n_first, n_steps, pos_ref, pos_next_ref, gate_ref, xa_ref, xb_ref, ys_hbm, g2_ref, b2_ref,
                    ya_ref, yb_ref, ybuf, sems):
    i = pl.program_id(0)
    n_rows = tt * TOP_K * SLAB

    def gather(p_ref, slot):
        def issue(r, carry):
            for k in range(TOP_K):
                src = ys_hbm.at[pl.ds(pl.multiple_of(p_ref[0, 0, r * TOP_K + k] * SLAB, SLAB), SLAB)]
                dst = ybuf.at[slot, pl.ds(pl.multiple_of((k * tt + r) * SLAB, SLAB), SLAB)]
                pltpu.make_async_copy(src, dst, sems.at[slot]).start(priority=k % 2)
            return carry

        lax.fori_loop(0, tt, issue, 0)

    pl.when(i == 0)(lambda: gather(pos_ref, 0))

    def tile(slot):
        pl.when(i + 1 < n_steps)(lambda: gather(pos_next_ref, 1 - slot))
        pltpu.make_async_copy(ys_hbm.at[pl.ds(0, n_rows)], ybuf.at[slot], sems.at[slot]).wait()
        rows_ref = ybuf.at[slot]
        gates = gate_ref[...]
        acc = gates[:, 0:1] * _slab_rows(rows_ref, 0, tt)
        for k in range(1, TOP_K):
            acc = acc + gates[:, k:k + 1] * _slab_rows(rows_ref, k * tt, tt)

        def finish(x_ref, y_ref):
            x1 = _slab_rows(x_ref, 0, tt)
            y_ref[...] = _layer_norm(DN_ALPHA * x1 + acc, g2_ref[...], b2_ref[...])

        pl.when(i < n_first)(lambda: finish(xa_ref, ya_ref))
        pl.when(i >= n_first)(lambda: finish(xb_ref, yb_ref))

    for slot in range(2):
        pl.when(i % 2 == slot)(functools.partial(tile, slot))


def _combine(x1s_a, x1s_b, ys, pos, gates, prm, *, tt):
    n_a = x1s_a.shape[0] // SLAB
    n_b = x1s_b.shape[0] // SLAB
    assert n_a % tt == 0 and n_b % tt == 0
    n_first = n_a // tt
    n_steps = (n_a + n_b) // tt
    a_map = lambda i: (jnp.minimum(i, n_first - 1), 0)
    b_map = lambda i: (jnp.maximum(i - n_first, 0), 0)
    pos3 = pos.reshape(n_steps, 1, tt * TOP_K)
    return pl.pallas_call(
        functools.partial(_combine_kernel, tt, n_first, n_steps),
        grid=(n_steps,),
        in_specs=[
            pl.BlockSpec((1, 1, tt * TOP_K), lambda i: (i, 0, 0), memory_space=pltpu.SMEM),
            pl.BlockSpec((1, 1, tt * TOP_K), lambda i: (jnp.minimum(i + 1, n_steps - 1), 0, 0),
                         memory_space=pltpu.SMEM),
            pl.BlockSpec((tt, TOP_K), lambda i: (i, 0)),
            pl.BlockSpec((tt * SLAB, LANES), a_map),
            pl.BlockSpec((tt * SLAB, LANES), b_map),
            pl.BlockSpec(memory_space=pl.ANY),
            pl.BlockSpec((1, D_MODEL), lambda i: (0, 0)),
            pl.BlockSpec((1, D_MODEL), lambda i: (0, 0)),
        ],
        out_specs=(pl.BlockSpec((tt, D_MODEL), a_map), pl.BlockSpec((tt, D_MODEL), b_map)),
        out_shape=(jax.ShapeDtypeStruct((n_a, D_MODEL), _F32), jax.ShapeDtypeStruct((n_b, D_MODEL), _F32)),
        scratch_shapes=[pltpu.VMEM((2, TOP_K * tt * SLAB, LANES), _F32), pltpu.SemaphoreType.DMA((2,))],
        compiler_params=pltpu.CompilerParams(
            dimension_semantics=("arbitrary",), vmem_limit_bytes=VMEM_LIMIT_BYTES),
        name="moe_combine",
    )(pos3, pos3, gates, x1s_a, x1s_b, ys, prm["g2"], prm["b2"])


def _route(meta_a, meta_b, gate_a, gate_b, counts_a, counts_b, tm, n_tiles):
    meta = jnp.concatenate([meta_a, meta_b], axis=1).T
    idx = meta[:, 0:TOP_K]
    rank = meta[:, TOP_K:2 * TOP_K]
    gates = jnp.concatenate([gate_a[0:TOP_K], gate_b[0:TOP_K]], axis=1).T
    cp = counts_a.reshape(-1).astype(jnp.int32)
    cs = counts_b.reshape(-1).astype(jnp.int32)
    is_b = (jnp.arange(meta.shape[0]) >= meta_a.shape[1])[:, None]
    rank = rank + jnp.where(is_b, cp[idx], 0)
    tiles_per = (cp + cs + tm - 1) // tm
    tile_end = jnp.cumsum(tiles_per)
    tile_start = tile_end - tiles_per
    pos = tile_start[idx] * tm + rank
    n_used = tile_end[-1:]
    tile_ids = jnp.arange(n_tiles, dtype=jnp.int32)
    tile_expert = jnp.minimum(jnp.sum(tile_ids[:, None] >= tile_end[None, :], axis=1), N_EXPERTS - 1)
    last_e = tile_expert[jnp.maximum(n_used[0] - 1, 0)]
    tile_expert = jnp.where(tile_ids < n_used[0], tile_expert, last_e).astype(jnp.int32)
    zero_tiles = jnp.where(tiles_per > 0, tile_end - 1, -1).astype(jnp.int32)
    return pos.astype(jnp.int32), gates, tile_expert, n_used.astype(jnp.int32), zero_tiles


def _rope_tables(pos):
    half = ROT_DIM // 2
    inv_freq = ROPE_THETA ** (-jnp.arange(half, dtype=_F32) / half)
    ang = pos.astype(_F32)[:, None] * inv_freq[None, :]
    cos, sin = jnp.cos(ang), jnp.sin(ang)
    n = pos.shape[0]
    ones = jnp.ones((n, HEAD_DIM - ROT_DIM), _F32)
    zeros = jnp.zeros((n, HEAD_DIM - ROT_DIM), _F32)
    zh = jnp.zeros((n, half), _F32)
    cos_t = jnp.concatenate([cos, cos, ones], axis=1)
    sin_up = jnp.concatenate([-sin, zh, zeros], axis=1)
    sin_dn = jnp.concatenate([zh, sin, zeros], axis=1)
    tab = jnp.stack([cos_t, sin_up, sin_dn])
    return jnp.concatenate([tab, tab], axis=2)


def _block_diag(w):
    hd, d, _ = w.shape
    eye = jnp.eye(hd, dtype=w.dtype)
    return (eye[:, None, :, None] * w[:, :, None, :]).reshape(hd * d, hd * d)


def _gate_blocks(wa, wx):
    blocks = []
    head_of = np.arange(D_RNN) // RNN_HEAD_DIM
    for j, k0 in enumerate(GATE_STARTS):
        cols = np.arange(j * GATE_COLS, (j + 1) * GATE_COLS)
        reach = np.flatnonzero(np.isin(head_of, head_of[cols]))
        assert reach.min() >= k0 and reach.max() < k0 + GATE_K
        cs = slice(j * GATE_COLS, (j + 1) * GATE_COLS)
        blocks.append(jnp.concatenate([wa[k0:k0 + GATE_K, cs], wx[k0:k0 + GATE_K, cs]], axis=1))
    return jnp.stack(blocks)


def kernel(x_prompt, x_sample, cache_k, cache_v, state_conv, state_h, w_in, attn_sinks, w_attn_out, conv_w, conv_b, gate_a_w, gate_a_b, gate_x_w, gate_x_b, lru_lambda, w_rnn_out, w_out, ln1_g, ln1_b, router_w, router_b, w_gate_up, b_gate_up, w_down, b_down, ln2_g, ln2_b):
    assert w_in.shape[0] == DEPTH == 1
    l = 0
    bsz, seq, _ = x_prompt.shape
    dbsz, dseq, _ = x_sample.shape
    row = lambda a: a.reshape(1, -1)
    prm = {
        "w_in": (w_in[l] * jnp.where(jnp.arange(IN_DIM) >= OFF_GA, 0.5, 1.0)).astype(_BF16),
        "conv_w": conv_w[l], "conv_b": row(conv_b[l]),
        "wg": (0.5 * _gate_blocks(_block_diag(gate_a_w[l]), _block_diag(gate_x_w[l]))).astype(_BF16),
        "bg": 0.5 * jnp.concatenate([gate_a_b[l].reshape(-1, 1, GATE_COLS),
                                     gate_x_b[l].reshape(-1, 1, GATE_COLS)], axis=2),
        "lam": row(lru_lambda[l]),
        "wao": w_attn_out[l].astype(_BF16), "wro": w_rnn_out[l].astype(_BF16), "wout": w_out[l].astype(_BF16),
        "g1": row(ln1_g[l]), "b1": row(ln1_b[l]),
        "rwt": router_w[l].T.astype(_BF16), "rb": router_b[l].reshape(-1, 1),
        "wgu": w_gate_up[l], "bgu": b_gate_up[l][:, None, :],
        "wd": w_down[l], "bd": b_down[l][:, None, :],
        "g2": row(ln2_g[l]), "b2": row(ln2_b[l]),
    }
    sinks = attn_sinks[l]

    rope_p = _rope_tables(jnp.arange(seq, dtype=jnp.int32))
    rope_s = _rope_tables(PAST_LEN + jnp.arange(dseq, dtype=jnp.int32))

    n_prompt = bsz * seq
    n_tok = n_prompt + dbsz * dseq
    zeros_kv = jnp.zeros((bsz, WINDOW, KV_DIM), _F32)
    x1_p, meta_p, gate_p, cnt_p, pk, pv, pc, ph = _mixer(
        x_prompt, rope_p, zeros_kv, zeros_kv, jnp.zeros((bsz, CONV_W - 1, D_RNN), _F32),
        jnp.zeros((bsz, 1, D_RNN), _F32), sinks, prm, has_past=False, nb=1, lt=min(MIXER_ROWS, seq), lq=CHUNK)
    x1_s, meta_s, gate_s, cnt_s, sk, sv, sc, sh = _mixer(
        x_sample, rope_s, cache_k[l].reshape(dbsz, WINDOW, KV_DIM), cache_v[l].reshape(dbsz, WINDOW, KV_DIM),
        state_conv[l], state_h[l][:, None, :], sinks, prm, has_past=True, nb=min(MIXER_ROWS // dseq, dbsz),
        lt=dseq, lq=dseq)

    tm = EXPERT_ROWS
    n_tiles = (n_tok * TOP_K + N_EXPERTS * (tm - 1)) // tm + 1
    pos, gates, tile_expert, n_used, zero_tiles = _route(meta_p, meta_s, gate_p, gate_s, cnt_p, cnt_s, tm, n_tiles)
    xs = _dispatch(x1_p, x1_s, pos, zero_tiles, n_used, tt=TOKEN_ROWS, tm=tm, n_tiles=n_tiles)
    ysort = _experts(xs, tile_expert, n_used, prm, tm=tm)
    yp, ys = _combine(x1_p, x1_s, ysort, pos, gates, prm, tt=TOKEN_ROWS)
    yp = yp.reshape(bsz, seq, D_MODEL)
    ys = ys.reshape(dbsz, dseq, D_MODEL)

    kv5 = lambda a: a.reshape(1, a.shape[0], WINDOW, N_KV_HEADS, HEAD_DIM)
    return (yp, ys, kv5(pk), kv5(pv), pc[None], ph.reshape(1, bsz, D_RNN),
            kv5(sk), kv5(sv), sc[None], sh.reshape(1, dbsz, D_RNN))
```

```python
import functools

import jax
import jax.numpy as jnp
import numpy as np
from jax import lax
from jax.experimental import pallas as pl
from jax.experimental.pallas import tpu as pltpu

D_MODEL = 1024
PAST_LEN = 1024
CHUNK = 64
N_HEADS = 16
N_KV_HEADS = 4
HEAD_DIM = 64
GROUP = N_HEADS // N_KV_HEADS
ROT_DIM = HEAD_DIM // 4
ROPE_THETA = 500000.0
WINDOW = 128
ATTN_SCALE = HEAD_DIM ** -0.5
NEG_INF = -1e30
D_RNN = 1280
RNN_HEADS = 16
RNN_HEAD_DIM = D_RNN // RNN_HEADS
CONV_W = 4
LRU_C = 8.0
N_EXPERTS = 32
TOP_K = 4
D_FF = 1024
SWIGLU_ALPHA = 1.702
SWIGLU_LIMIT = 7.0
LN_EPS = 1e-5
DEPTH = 1
DN_ALPHA = (2.0 * DEPTH) ** 0.25

Q_DIM = N_HEADS * HEAD_DIM
KV_DIM = N_KV_HEADS * HEAD_DIM
OFF_K = Q_DIM
OFF_V = OFF_K + KV_DIM
OFF_XR = OFF_V + KV_DIM
OFF_YR = OFF_XR + D_RNN
OFF_GA = OFF_YR + D_RNN
OFF_GR = OFF_GA + D_MODEL
IN_DIM = OFF_GR + D_MODEL

LANES = 128
SLAB = D_MODEL // LANES
MIXER_ROWS = 256
EXPERT_ROWS = 512
TOKEN_ROWS = 256
GELU_C0 = float(np.sqrt(2.0 / np.pi))
GELU_C1 = 0.044715 * GELU_C0
ATT_GROUP = 4
SCAN_S = 4
GATE_COLS = 256
GATE_K = 512
GATE_STARTS = tuple(min(max((j * GATE_COLS // RNN_HEAD_DIM) * RNN_HEAD_DIM // LANES * LANES, 0), D_RNN - GATE_K)
                    for j in range(D_RNN // GATE_COLS))
CONV_HDR = 8
VMEM_LIMIT_BYTES = 56 * 1024 * 1024

_BF16 = jnp.bfloat16
_F32 = jnp.float32


def _dot(a, b):
    return jnp.dot(a, b, preferred_element_type=_F32)


def _dot_nt(a, b):
    return lax.dot_general(a, b, (((1,), (1,)), ((), ())), preferred_element_type=_F32)


def _half_logistic(h):
    return 0.5 * jnp.tanh(h) + 0.5


def _layer_norm(x, g, b):
    mu = jnp.mean(x, axis=-1, keepdims=True)
    xc = x - mu
    var = jnp.mean(xc * xc, axis=-1, keepdims=True)
    return xc * lax.rsqrt(var + LN_EPS) * g + b


def _rope(x, cos_t, sin_up, sin_dn):
    return x * cos_t + pltpu.roll(x, LANES - ROT_DIM // 2, 1) * sin_up + pltpu.roll(x, ROT_DIM // 2, 1) * sin_dn


def _lo_hi(slab, h):
    blk = slab[:, (h // 2) * LANES:(h // 2 + 1) * LANES]
    lane = lax.broadcasted_iota(jnp.int32, blk.shape, 1)
    if h % 2 == 0:
        lo = jnp.where(lane < HEAD_DIM, blk, 0.0)
        hi = pltpu.roll(lo, HEAD_DIM, 1)
    else:
        hi = jnp.where(lane >= HEAD_DIM, blk, 0.0)
        lo = pltpu.roll(hi, HEAD_DIM, 1)
    return lo.astype(_BF16), hi.astype(_BF16)


def _mixer_kernel(has_past, nb, lt, lq,
                  x_ref, rope_ref, w_in_ref, k0_ref, v0_ref, c0_ref, h0_ref, sinkrows_ref,
                  convw_ref, convb_ref, wg_ref, bg_ref, lam_ref, wao_ref, wro_ref, wout_ref,
                  g1_ref, b1_ref, rwt_ref, rb_ref,
                  x1s_ref, meta_ref, gate_ref, cnt_ref, kn_ref, vn_ref, cn_ref, hn_ref,
                  kw_ref, vw_ref, xp_ref, xc_ref, a_ref, b_ref, hin_ref, hc_ref, oat_ref, run_ref):
    s_idx = pl.program_id(1)
    n_s = pl.num_programs(1)
    rows = nb * lt
    n_chunks = lt // lq

    @pl.when(jnp.logical_and(pl.program_id(0) == 0, s_idx == 0))
    def _init_counts():
        run_ref[...] = jnp.zeros_like(run_ref)

    @pl.when(s_idx == 0)
    def _init():
        for bi in range(nb):
            for h in range(N_KV_HEADS):
                klo, khi = _lo_hi(k0_ref[bi], h)
                vlo, vhi = _lo_hi(v0_ref[bi], h)
                kw_ref[bi, h, 0, 0:WINDOW, :] = klo
                kw_ref[bi, h, 1, 0:WINDOW, :] = khi
                vw_ref[bi, h, 0, 0:WINDOW, :] = vlo
                vw_ref[bi, h, 1, 0:WINDOW, :] = vhi
            for c in range(D_RNN // LANES):
                xp_ref[bi, c, CONV_HDR - (CONV_W - 1):CONV_HDR, :] = c0_ref[bi, :, c * LANES:(c + 1) * LANES]
            hc_ref[bi] = h0_ref[bi]

    x = x_ref[...].reshape(rows, D_MODEL)
    xb = x.astype(_BF16)

    cos_t = jnp.concatenate([rope_ref[0]] * nb, axis=0) if nb > 1 else rope_ref[0]
    sin_up = jnp.concatenate([rope_ref[1]] * nb, axis=0) if nb > 1 else rope_ref[1]
    sin_dn = jnp.concatenate([rope_ref[2]] * nb, axis=0) if nb > 1 else rope_ref[2]

    zq = _dot(xb, w_in_ref[:, 0:Q_DIM])
    q_blocks = []
    for c in range(Q_DIM // LANES):
        qr = _rope(zq[:, c * LANES:(c + 1) * LANES], cos_t, sin_up, sin_dn)
        q_blocks.append((qr * ATTN_SCALE).astype(_BF16))
    zk = _dot(xb, w_in_ref[:, OFF_K:OFF_V])
    k_rot = jnp.concatenate(
        [_rope(zk[:, c * LANES:(c + 1) * LANES], cos_t, sin_up, sin_dn) for c in range(KV_DIM // LANES)], axis=1)
    v_new = _dot(xb, w_in_ref[:, OFF_V:OFF_XR])

    for bi in range(nb):
        r0 = bi * lt
        for h in range(N_KV_HEADS):
            klo, khi = _lo_hi(k_rot[r0:r0 + lt], h)
            vlo, vhi = _lo_hi(v_new[r0:r0 + lt], h)
            kw_ref[bi, h, 0, WINDOW:WINDOW + lt, :] = klo
            kw_ref[bi, h, 1, WINDOW:WINDOW + lt, :] = khi
            vw_ref[bi, h, 0, WINDOW:WINDOW + lt, :] = vlo
            vw_ref[bi, h, 1, WINDOW:WINDOW + lt, :] = vhi

    for bi in range(nb):
        r0 = bi * lt
        if lt >= WINDOW:
            kn_ref[bi] = k_rot[r0 + lt - WINDOW:r0 + lt]
            vn_ref[bi] = v_new[r0 + lt - WINDOW:r0 + lt]
        else:
            kn_ref[bi, 0:WINDOW - lt, :] = k0_ref[bi, lt:WINDOW, :]
            kn_ref[bi, WINDOW - lt:WINDOW, :] = k_rot[r0:r0 + lt]
            vn_ref[bi, 0:WINDOW - lt, :] = v0_ref[bi, lt:WINDOW, :]
            vn_ref[bi, WINDOW - lt:WINDOW, :] = v_new[r0:r0 + lt]

    w2 = 2 * lq
    wk = 2 * WINDOW + w2
    r_all = N_KV_HEADS * 2 * lq
    is_a2 = lax.broadcasted_iota(jnp.int32, (r_all, w2), 1) < lq
    lane_lo = lax.broadcasted_iota(jnp.int32, (r_all, LANES), 1) < HEAD_DIM
    key_row = lax.broadcasted_iota(jnp.int32, (wk, LANES), 0)
    key_is_a = jnp.logical_or(key_row < WINDOW, jnp.logical_and(key_row >= 2 * WINDOW, key_row < 2 * WINDOW + lq))
    key_lane_lo = lax.broadcasted_iota(jnp.int32, (wk, LANES), 1) < HEAD_DIM
    head_ones = jnp.where(key_is_a == key_lane_lo, 1.0, 0.0).astype(_BF16)
    sink_rows, sink_a_rows, sink_b_rows = sinkrows_ref[0], sinkrows_ref[1], sinkrows_ref[2]
    first = s_idx == 0
    units = [(bi, j) for bi in range(nb) for j in range(n_chunks)]
    for g0 in range(0, len(units), ATT_GROUP):
        group = units[g0:g0 + ATT_GROUP]
        scores = []
        vmats = []
        for bi, j in group:
            p0 = j * lq
            o0 = WINDOW + j * lq
            q0 = bi * lt + j * lq
            for h in range(N_KV_HEADS):
                kmat = jnp.concatenate([kw_ref[bi, h, 0, p0:p0 + WINDOW, :], kw_ref[bi, h, 1, p0:p0 + WINDOW, :],
                                        kw_ref[bi, h, 0, o0:o0 + lq, :], kw_ref[bi, h, 1, o0:o0 + lq, :]], axis=0)
                vmat = jnp.concatenate([vw_ref[bi, h, 0, p0:p0 + WINDOW, :], vw_ref[bi, h, 1, p0:p0 + WINDOW, :],
                                        vw_ref[bi, h, 0, o0:o0 + lq, :], vw_ref[bi, h, 1, o0:o0 + lq, :]], axis=0)
                vmats.append(jnp.concatenate([vmat, head_ones], axis=1))
                qs = jnp.concatenate([q_blocks[2 * h][q0:q0 + lq], q_blocks[2 * h + 1][q0:q0 + lq]], axis=0)
                sc_h = _dot_nt(qs, kmat)
                if not has_past and j * lq < WINDOW:
                    col = lax.broadcasted_iota(jnp.int32, sc_h.shape, 1)
                    n_bad = WINDOW - j * lq
                    bad = jnp.logical_or(col < n_bad, jnp.logical_and(col >= WINDOW, col < WINDOW + n_bad))
                    sc_h = jnp.where(jnp.logical_and(first, bad), NEG_INF, sc_h)
                scores.append(sc_h)
        n_u = len(group)
        sc = jnp.concatenate(scores, axis=0)
        tile_rows = lambda a: jnp.concatenate([a] * n_u, axis=0) if n_u > 1 else a
        is_a2g, lane_log, sinks_g = tile_rows(is_a2), tile_rows(lane_lo), tile_rows(sink_rows)
        sinks_ag, sinks_bg = tile_rows(sink_a_rows), tile_rows(sink_b_rows)
        c0 = sc[:, 0:LANES]
        c1 = sc[:, LANES:2 * LANES]
        c2 = sc[:, 2 * LANES:wk]
        if w2 == LANES:
            ma = jnp.max(jnp.maximum(c0, jnp.where(is_a2g, c2, NEG_INF)).astype(_BF16), axis=1, keepdims=True)
            mb = jnp.max(jnp.maximum(c1, jnp.where(is_a2g, NEG_INF, c2)).astype(_BF16), axis=1, keepdims=True)
        else:
            ma = jnp.maximum(jnp.max(c0.astype(_BF16), axis=1, keepdims=True),
                             jnp.max(jnp.where(is_a2g, c2, NEG_INF).astype(_BF16), axis=1, keepdims=True))
            mb = jnp.maximum(jnp.max(c1.astype(_BF16), axis=1, keepdims=True),
                             jnp.max(jnp.where(is_a2g, NEG_INF, c2).astype(_BF16), axis=1, keepdims=True))
        ma = jnp.maximum(jnp.broadcast_to(ma.astype(_F32), sinks_ag.shape), sinks_ag)
        mb = jnp.maximum(jnp.broadcast_to(mb.astype(_F32), sinks_bg.shape), sinks_bg)
        probs = jnp.concatenate([jnp.exp(c0 - ma), jnp.exp(c1 - mb),
                                 jnp.exp(c2 - jnp.where(is_a2g, ma[:, 0:w2], mb[:, 0:w2]))], axis=1).astype(_BF16)
        sink_term = jnp.exp(sinks_g - jnp.where(lane_log, ma, mb))
        for u, (bi, j) in enumerate(group):
            q0 = bi * lt + j * lq
            for h in range(N_KV_HEADS):
                r0 = (u * N_KV_HEADS + h) * 2 * lq
                od = _dot(probs[r0:r0 + 2 * lq], vmats[u * N_KV_HEADS + h])
                out = od[:, 0:LANES] / (od[:, LANES:2 * LANES] + sink_term[r0:r0 + 2 * lq])
                for p in range(2):
                    oat_ref[q0:q0 + lq, (2 * h + p) * LANES:(2 * h + p + 1) * LANES] = (
                        out[p * lq:(p + 1) * lq].astype(_BF16))

    if lt >= WINDOW:
        @pl.when(s_idx + 1 < n_s)
        def _carry_kv():
            for bi in range(nb):
                for h in range(N_KV_HEADS):
                    for v in range(2):
                        kw_ref[bi, h, v, 0:WINDOW, :] = kw_ref[bi, h, v, lt:lt + WINDOW, :]
                        vw_ref[bi, h, v, 0:WINDOW, :] = vw_ref[bi, h, v, lt:lt + WINDOW, :]

    xr = _dot(xb, w_in_ref[:, OFF_XR:OFF_YR])
    n_lb = D_RNN // LANES
    n_g = lt // SCAN_S
    conv_w = convw_ref[...]
    conv_b = convb_ref[...]
    for bi in range(nb):
        r0 = bi * lt
        tail = xr[r0 + lt - (CONV_W - 1):r0 + lt]
        cn_ref[bi] = tail
        for c in range(n_lb):
            ls = slice(c * LANES, (c + 1) * LANES)
            xp_ref[bi, c, CONV_HDR:CONV_HDR + lt, :] = xr[r0:r0 + lt, ls]
            frames = [xp_ref[bi, c, pl.ds(CONV_HDR + m, n_g, stride=SCAN_S), :]
                      for m in range(-(CONV_W - 1), SCAN_S)]
            for k in range(SCAN_S):
                acc = frames[k] * conv_w[0:1, ls]
                for t in range(1, CONV_W):
                    acc = acc + frames[k + t] * conv_w[t:t + 1, ls]
                xc_ref[c, pl.ds(r0 + k, n_g, stride=SCAN_S), :] = acc + conv_b[:, ls]
            xp_ref[bi, c, CONV_HDR - (CONV_W - 1):CONV_HDR, :] = tail[:, ls]

    yr = _dot(xb, w_in_ref[:, OFF_YR:OFF_GA])
    ga = _dot(xb, w_in_ref[:, OFF_GA:OFF_GR])
    gr = _dot(xb, w_in_ref[:, OFF_GR:IN_DIM])
    xc = jnp.concatenate([xc_ref[c] for c in range(n_lb)], axis=1)
    xcb = xc.astype(_BF16)
    lam = lam_ref[...]
    softplus_neg = jnp.maximum(-lam, 0.0) + jnp.log(1.0 + jnp.exp(-jnp.abs(lam)))
    for j in range(D_RNN // GATE_COLS):
        cs = slice(j * GATE_COLS, (j + 1) * GATE_COLS)
        g = _dot(xcb[:, GATE_STARTS[j]:GATE_STARTS[j] + GATE_K], wg_ref[j]) + bg_ref[j]
        i_gate = _half_logistic(g[:, GATE_COLS:2 * GATE_COLS])
        m_sp = (-0.5 * LRU_C) * softplus_neg[:, cs]
        log_a = jnp.tanh(g[:, 0:GATE_COLS]) * m_sp + m_sp
        a = jnp.exp(log_a)
        z = -jnp.tanh(log_a) * (1.0 + a * a)
        bv = jnp.where(z > 0.0, z * lax.rsqrt(z), 0.0) * i_gate * xc[:, cs]
        for cc in range(GATE_COLS // LANES):
            c = j * (GATE_COLS // LANES) + cc
            a_ref[c] = a[:, cc * LANES:(cc + 1) * LANES]
            b_ref[c] = bv[:, cc * LANES:(cc + 1) * LANES]

    for bi in range(nb):
        r0 = bi * lt
        a_tot, h_tot = [], []
        for c in range(n_lb):
            at = a_ref[c, pl.ds(r0, n_g, stride=SCAN_S), :]
            ht = b_ref[c, pl.ds(r0, n_g, stride=SCAN_S), :]
            for k in range(1, SCAN_S):
                ak = a_ref[c, pl.ds(r0 + k, n_g, stride=SCAN_S), :]
                ht = ak * ht + b_ref[c, pl.ds(r0 + k, n_g, stride=SCAN_S), :]
                at = ak * at
            a_tot.append(at)
            h_tot.append(ht)
        carry = [hc_ref[bi][:, c * LANES:(c + 1) * LANES] for c in range(n_lb)]
        for g in range(n_g):
            for c in range(n_lb):
                hin_ref[c, g:g + 1, :] = carry[c]
                carry[c] = a_tot[c][g:g + 1, :] * carry[c] + h_tot[c][g:g + 1, :]
        h_last = jnp.concatenate(carry, axis=1)
        hc_ref[bi] = h_last
        hn_ref[bi] = h_last
        for c in range(n_lb):
            hcur = hin_ref[c]
            for k in range(SCAN_S):
                hcur = a_ref[c, pl.ds(r0 + k, n_g, stride=SCAN_S), :] * hcur + b_ref[c, pl.ds(r0 + k, n_g, stride=SCAN_S), :]
                b_ref[c, pl.ds(r0 + k, n_g, stride=SCAN_S), :] = hcur

    gelu_t = jnp.tanh(yr * (GELU_C0 + GELU_C1 * (yr * yr)))
    half_h = jnp.concatenate([b_ref[c] for c in range(n_lb)], axis=1) * (0.5 * yr)
    rnn = half_h * gelu_t + half_h

    merged2 = ((jnp.tanh(ga) + 1.0) * _dot(oat_ref[...], wao_ref[...])
               + (jnp.tanh(gr) + 1.0) * _dot(rnn.astype(_BF16), wro_ref[...]))
    mix = _dot(merged2.astype(_BF16), wout_ref[...])
    x1 = _layer_norm(DN_ALPHA * x + mix, g1_ref[...], b1_ref[...])
    for c in range(SLAB):
        x1s_ref[pl.ds(c, rows, stride=SLAB), :] = x1[:, c * LANES:(c + 1) * LANES]

    logits = _dot_nt(rwt_ref[...], x1.astype(_BF16)) + rb_ref[...]
    e_iota = lax.broadcasted_iota(jnp.int32, logits.shape, 0)
    work = logits
    top_vals = []
    top_idx = []
    top_sel = []
    for _ in range(TOP_K):
        m = jnp.max(work, axis=0, keepdims=True)
        idx = jnp.min(jnp.where(work == m, e_iota, N_EXPERTS), axis=0, keepdims=True)
        sel = e_iota == idx
        top_vals.append(m)
        top_idx.append(idx)
        top_sel.append(sel)
        work = jnp.where(sel, -jnp.inf, work)
    exps = [jnp.exp(v - top_vals[0]) for v in top_vals]
    denom = exps[0]
    for e in exps[1:]:
        denom = denom + e

    onehot = jnp.zeros_like(logits)
    for sel in top_sel:
        onehot = onehot + jnp.where(sel, 1.0, 0.0)
    r_i = lax.broadcasted_iota(jnp.int32, (rows, rows), 0)
    c_i = lax.broadcasted_iota(jnp.int32, (rows, rows), 1)
    earlier = jnp.where(r_i < c_i, 1.0, 0.0).astype(_BF16)
    before = _dot(onehot.astype(_BF16), earlier) + run_ref[...]
    run_ref[...] = run_ref[...] + jnp.sum(onehot, axis=1, keepdims=True)
    cnt_ref[...] = run_ref[...]

    sub_m = lax.broadcasted_iota(jnp.int32, (2 * TOP_K, rows), 0)
    meta_i = jnp.zeros((2 * TOP_K, rows), jnp.int32)
    meta_g = jnp.zeros((2 * TOP_K, rows), _F32)
    for k in range(TOP_K):
        rank_k = jnp.sum(jnp.where(top_sel[k], before, 0.0), axis=0, keepdims=True).astype(jnp.int32)
        meta_i = jnp.where(sub_m == k, top_idx[k], meta_i)
        meta_i = jnp.where(sub_m == TOP_K + k, rank_k, meta_i)
        meta_g = jnp.where(sub_m == k, exps[k] / denom, meta_g)
    meta_ref[...] = meta_i
    gate_ref[...] = meta_g


def _const_spec(shape):
    nd = len(shape)
    return pl.BlockSpec(shape, lambda b, s, _nd=nd: (0,) * _nd, pipeline_mode=pl.Buffered(1))


def _mixer(x, rope, k0, v0, c0, h0, sinks, prm, *, has_past, nb, lt, lq):
    bsz, seq, _ = x.shape
    assert bsz % nb == 0 and seq % lt == 0 and lt % lq == 0 and lq % 32 == 0
    assert lt >= WINDOW or seq == lt
    n_s = seq // lt
    rows = nb * lt
    n_tok = bsz * seq
    assert lt % SCAN_S == 0
    kernel = functools.partial(_mixer_kernel, has_past, nb, lt, lq)
    r_all = N_KV_HEADS * 2 * lq
    sink_rows = jnp.broadcast_to(sinks.reshape(N_KV_HEADS * 2, 1, 2, 1), (N_KV_HEADS * 2, lq, 2, HEAD_DIM))
    sink_rows = sink_rows.reshape(r_all, LANES)
    sink_ab = jnp.broadcast_to(sinks.reshape(N_KV_HEADS * 2, 1, 2, 1), (N_KV_HEADS * 2, lq, 2, LANES))
    sink_rows = jnp.stack([sink_rows, sink_ab[:, :, 0].reshape(r_all, LANES), sink_ab[:, :, 1].reshape(r_all, LANES)])
    n_gate = D_RNN // GATE_COLS
    batch_spec = lambda shape: pl.BlockSpec((nb,) + shape, lambda b, s: (b,) + (0,) * len(shape))
    in_specs = [
        pl.BlockSpec((nb, lt, D_MODEL), lambda b, s: (b, s, 0)),
        pl.BlockSpec((3, lt, LANES), lambda b, s: (0, s, 0)),
        _const_spec((D_MODEL, IN_DIM)),
        batch_spec((WINDOW, KV_DIM)), batch_spec((WINDOW, KV_DIM)),
        batch_spec((CONV_W - 1, D_RNN)), batch_spec((1, D_RNN)),
        _const_spec((3, r_all, LANES)),
        _const_spec((CONV_W, D_RNN)), _const_spec((1, D_RNN)),
        _const_spec((n_gate, GATE_K, 2 * GATE_COLS)), _const_spec((n_gate, 1, 2 * GATE_COLS)), _const_spec((1, D_RNN)),
        _const_spec((Q_DIM, D_MODEL)), _const_spec((D_RNN, D_MODEL)), _const_spec((D_MODEL, D_MODEL)),
        _const_spec((1, D_MODEL)), _const_spec((1, D_MODEL)),
        _const_spec((N_EXPERTS, D_MODEL)), _const_spec((N_EXPERTS, 1)),
    ]
    tok_map = lambda b, s: (b * n_s + s, 0)
    tok_map_t = lambda b, s: (0, b * n_s + s)
    out_shape = (
        jax.ShapeDtypeStruct((n_tok * SLAB, LANES), _F32),
        jax.ShapeDtypeStruct((2 * TOP_K, n_tok), jnp.int32),
        jax.ShapeDtypeStruct((2 * TOP_K, n_tok), _F32),
        jax.ShapeDtypeStruct((N_EXPERTS, 1), _F32),
        jax.ShapeDtypeStruct((bsz, WINDOW, KV_DIM), _F32),
        jax.ShapeDtypeStruct((bsz, WINDOW, KV_DIM), _F32),
        jax.ShapeDtypeStruct((bsz, CONV_W - 1, D_RNN), _F32),
        jax.ShapeDtypeStruct((bsz, 1, D_RNN), _F32),
    )
    out_specs = (
        pl.BlockSpec((rows * SLAB, LANES), tok_map),
        pl.BlockSpec((2 * TOP_K, rows), tok_map_t),
        pl.BlockSpec((2 * TOP_K, rows), tok_map_t),
        pl.BlockSpec((N_EXPERTS, 1), lambda b, s: (0, 0)),
        batch_spec((WINDOW, KV_DIM)), batch_spec((WINDOW, KV_DIM)),
        batch_spec((CONV_W - 1, D_RNN)), batch_spec((1, D_RNN)),
    )
    scratch = [
        pltpu.VMEM((nb, N_KV_HEADS, 2, WINDOW + lt, LANES), _BF16),
        pltpu.VMEM((nb, N_KV_HEADS, 2, WINDOW + lt, LANES), _BF16),
        pltpu.VMEM((nb, D_RNN // LANES, CONV_HDR + lt, LANES), _F32),
        pltpu.VMEM((D_RNN // LANES, rows, LANES), _F32),
        pltpu.VMEM((D_RNN // LANES, rows, LANES), _F32),
        pltpu.VMEM((D_RNN // LANES, rows, LANES), _F32),
        pltpu.VMEM((D_RNN // LANES, lt // SCAN_S, LANES), _F32),
        pltpu.VMEM((nb, 1, D_RNN), _F32),
        pltpu.VMEM((rows, Q_DIM), _BF16),
        pltpu.VMEM((N_EXPERTS, 1), _F32),
    ]
    args = [x, rope, prm["w_in"], k0, v0, c0, h0, sink_rows, prm["conv_w"], prm["conv_b"], prm["wg"], prm["bg"],
            prm["lam"], prm["wao"], prm["wro"], prm["wout"], prm["g1"], prm["b1"], prm["rwt"], prm["rb"]]
    return pl.pallas_call(
        kernel,
        grid=(bsz // nb, n_s),
        in_specs=in_specs,
        out_specs=out_specs,
        out_shape=out_shape,
        scratch_shapes=scratch,
        compiler_params=pltpu.CompilerParams(
            dimension_semantics=("arbitrary", "arbitrary"), vmem_limit_bytes=VMEM_LIMIT_BYTES),
        name="mixer_past" if has_past else "mixer_prompt",
    )(*args)


def _slab_rows(ref, first_tok, n):
    return jnp.concatenate([ref[pl.ds(first_tok * SLAB + c, n, stride=SLAB), :] for c in range(SLAB)], axis=1)


def _dispatch_kernel(tt, tm, n_first, n_tiles, zt_ref, nu_ref, pos_ref, xa_ref, xb_ref, xs_hbm, zero_ref, sem):
    i = pl.program_id(0)

    @pl.when(i == 0)
    def _zero_padding():
        zero_ref[...] = jnp.zeros_like(zero_ref)

        def clear(tile):
            dst = xs_hbm.at[pl.ds(pl.multiple_of(tile * (tm * SLAB), SLAB), tm * SLAB)]
            cp = pltpu.make_async_copy(zero_ref, dst, sem)
            cp.start()
            cp.wait()

        for e in range(N_EXPERTS):
            pl.when(zt_ref[e] >= 0)(functools.partial(clear, zt_ref[e]))

        def clear_unused(tile, carry):
            clear(tile)
            return carry

        lax.fori_loop(nu_ref[0], n_tiles, clear_unused, 0)

    def scatter(src_ref):
        def issue(r, carry):
            src = src_ref.at[pl.ds(pl.multiple_of(r * SLAB, SLAB), SLAB)]
            for k in range(TOP_K):
                dst = xs_hbm.at[pl.ds(pl.multiple_of(pos_ref[0, 0, r * TOP_K + k] * SLAB, SLAB), SLAB)]
                pltpu.make_async_copy(src, dst, sem).start(priority=k % 2)
            return carry

        lax.fori_loop(0, tt, issue, 0)

    pl.when(i < n_first)(lambda: scatter(xa_ref))
    pl.when(i >= n_first)(lambda: scatter(xb_ref))
    n_rows = tt * TOP_K * SLAB
    pltpu.make_async_copy(xs_hbm.at[pl.ds(0, n_rows)], xs_hbm.at[pl.ds(0, n_rows)], sem).wait()


def _dispatch(x1s_a, x1s_b, pos, zero_tiles, n_used, *, tt, tm, n_tiles):
    n_a = x1s_a.shape[0] // SLAB
    n_b = x1s_b.shape[0] // SLAB
    assert n_a % tt == 0 and n_b % tt == 0 and n_tiles * tm >= tt * TOP_K
    n_first = n_a // tt
    n_steps = (n_a + n_b) // tt
    return pl.pallas_call(
        functools.partial(_dispatch_kernel, tt, tm, n_first, n_tiles),
        grid_spec=pltpu.PrefetchScalarGridSpec(
            num_scalar_prefetch=2,
            grid=(n_steps,),
            in_specs=[
                pl.BlockSpec((1, 1, tt * TOP_K), lambda i, zt, nu: (i, 0, 0), memory_space=pltpu.SMEM),
                pl.BlockSpec((tt * SLAB, LANES), lambda i, zt, nu: (jnp.minimum(i, n_first - 1), 0)),
                pl.BlockSpec((tt * SLAB, LANES), lambda i, zt, nu: (jnp.maximum(i - n_first, 0), 0)),
            ],
            out_specs=pl.BlockSpec(memory_space=pl.ANY),
            scratch_shapes=[pltpu.VMEM((tm * SLAB, LANES), _F32), pltpu.SemaphoreType.DMA],
        ),
        out_shape=jax.ShapeDtypeStruct((n_tiles * tm * SLAB, LANES), _F32),
        compiler_params=pltpu.CompilerParams(dimension_semantics=("arbitrary",)),
        name="moe_dispatch",
    )(zero_tiles, n_used, pos.reshape(n_steps, 1, tt * TOP_K), x1s_a, x1s_b)


def _expert_kernel(tm, te_ref, nu_ref, xs_ref, wgu_ref, bgu_ref, wd_ref, bd_ref, ys_ref, wgu_b, wd_b):
    i = pl.program_id(0)

    @pl.when(i < nu_ref[0])
    def _tile():
        prev_e = te_ref[jnp.maximum(i - 1, 0)]

        @pl.when(jnp.logical_or(i == 0, te_ref[i] != prev_e))
        def _new_expert():
            wgu_b[...] = wgu_ref[0].astype(_BF16)
            wd_b[...] = wd_ref[0].astype(_BF16)

        x = _slab_rows(xs_ref, 0, tm).astype(_BF16)
        hgu = _dot(x, wgu_b[...]) + bgu_ref[0]
        glu = jnp.minimum(hgu[:, 0:D_FF], SWIGLU_LIMIT)
        lin = jnp.clip(hgu[:, D_FF:2 * D_FF], -SWIGLU_LIMIT, SWIGLU_LIMIT)
        hh = glu * _half_logistic((0.5 * SWIGLU_ALPHA) * glu) * (lin + 1.0)
        y = _dot(hh.astype(_BF16), wd_b[...]) + bd_ref[0]
        for c in range(SLAB):
            ys_ref[pl.ds(c, tm, stride=SLAB), :] = y[:, c * LANES:(c + 1) * LANES]

    @pl.when(i >= nu_ref[0])
    def _unused_tile():
        ys_ref[...] = jnp.zeros_like(ys_ref)


def _experts(xs, tile_expert, n_used, prm, *, tm):
    n_tiles = xs.shape[0] // (tm * SLAB)
    row_map = lambda i, te, nu: (jnp.minimum(i, nu[0] - 1), 0)
    out_map = lambda i, te, nu: (i, 0)
    exp_map = lambda i, te, nu: (te[i], 0, 0)
    return pl.pallas_call(
        functools.partial(_expert_kernel, tm),
        grid_spec=pltpu.PrefetchScalarGridSpec(
            num_scalar_prefetch=2,
            grid=(n_tiles,),
            in_specs=[
                pl.BlockSpec((tm * SLAB, LANES), row_map),
                pl.BlockSpec((1, D_MODEL, 2 * D_FF), exp_map),
                pl.BlockSpec((1, 1, 2 * D_FF), exp_map),
                pl.BlockSpec((1, D_FF, D_MODEL), exp_map),
                pl.BlockSpec((1, 1, D_MODEL), exp_map),
            ],
            out_specs=pl.BlockSpec((tm * SLAB, LANES), out_map),
            scratch_shapes=[pltpu.VMEM((D_MODEL, 2 * D_FF), _BF16), pltpu.VMEM((D_FF, D_MODEL), _BF16)],
        ),
        out_shape=jax.ShapeDtypeStruct(xs.shape, _F32),
        compiler_params=pltpu.CompilerParams(
            dimension_semantics=("arbitrary",), vmem_limit_bytes=VMEM_LIMIT_BYTES),
        name="moe_experts",
    )(tile_expert, n_used, xs, prm["wgu"], prm["bgu"], prm["wd"], prm["bd"])


def _combine_kernel(tt, n_first, n_steps, pos_ref, pos_next_ref, gate_ref, xa_ref, xb_ref, ys_hbm, g2_ref, b2_ref,
                    ya_ref, yb_ref, ybuf, sems):
    i = pl.program_id(0)
    n_rows = tt * TOP_K * SLAB

    def gather(p_ref, slot):
        def issue(r, carry):
            for k in range(TOP_K):
                src = ys_hbm.at[pl.ds(pl.multiple_of(p_ref[0, 0, r * TOP_K + k] * SLAB, SLAB), SLAB)]
                dst = ybuf.at[slot, pl.ds(pl.multiple_of((k * tt + r) * SLAB, SLAB), SLAB)]
                pltpu.make_async_copy(src, dst, sems.at[slot]).start(priority=k % 2)
            return carry

        lax.fori_loop(0, tt, issue, 0)

    pl.when(i == 0)(lambda: gather(pos_ref, 0))

    def tile(slot):
        pl.when(i + 1 < n_steps)(lambda: gather(pos_next_ref, 1 - slot))
        pltpu.make_async_copy(ys_hbm.at[pl.ds(0, n_rows)], ybuf.at[slot], sems.at[slot]).wait()
        rows_ref = ybuf.at[slot]
        gates = gate_ref[...]
        acc = gates[:, 0:1] * _slab_rows(rows_ref, 0, tt)
        for k in range(1, TOP_K):
            acc = acc + gates[:, k:k + 1] * _slab_rows(rows_ref, k * tt, tt)

        def finish(x_ref, y_ref):
            x1 = _slab_rows(x_ref, 0, tt)
            y_ref[...] = _layer_norm(DN_ALPHA * x1 + acc, g2_ref[...], b2_ref[...])

        pl.when(i < n_first)(lambda: finish(xa_ref, ya_ref))
        pl.when(i >= n_first)(lambda: finish(xb_ref, yb_ref))

    for slot in range(2):
        pl.when(i % 2 == slot)(functools.partial(tile, slot))


def _combine(x1s_a, x1s_b, ys, pos, gates, prm, *, tt):
    n_a = x1s_a.shape[0] // SLAB
    n_b = x1s_b.shape[0] // SLAB
    assert n_a % tt == 0 and n_b % tt == 0
    n_first = n_a // tt
    n_steps = (n_a + n_b) // tt
    a_map = lambda i: (jnp.minimum(i, n_first - 1), 0)
    b_map = lambda i: (jnp.maximum(i - n_first, 0), 0)
    pos3 = pos.reshape(n_steps, 1, tt * TOP_K)
    return pl.pallas_call(
        functools.partial(_combine_kernel, tt, n_first, n_steps),
        grid=(n_steps,),
        in_specs=[
            pl.BlockSpec((1, 1, tt * TOP_K), lambda i: (i, 0, 0), memory_space=pltpu.SMEM),
            pl.BlockSpec((1, 1, tt * TOP_K), lambda i: (jnp.minimum(i + 1, n_steps - 1), 0, 0),
                         memory_space=pltpu.SMEM),
            pl.BlockSpec((tt, TOP_K), lambda i: (i, 0)),
            pl.BlockSpec((tt * SLAB, LANES), a_map),
            pl.BlockSpec((tt * SLAB, LANES), b_map),
            pl.BlockSpec(memory_space=pl.ANY),
            pl.BlockSpec((1, D_MODEL), lambda i: (0, 0)),
            pl.BlockSpec((1, D_MODEL), lambda i: (0, 0)),
        ],
        out_specs=(pl.BlockSpec((tt, D_MODEL), a_map), pl.BlockSpec((tt, D_MODEL), b_map)),
        out_shape=(jax.ShapeDtypeStruct((n_a, D_MODEL), _F32), jax.ShapeDtypeStruct((n_b, D_MODEL), _F32)),
        scratch_shapes=[pltpu.VMEM((2, TOP_K * tt * SLAB, LANES), _F32), pltpu.SemaphoreType.DMA((2,))],
        compiler_params=pltpu.CompilerParams(
            dimension_semantics=("arbitrary",), vmem_limit_bytes=VMEM_LIMIT_BYTES),
        name="moe_combine",
    )(pos3, pos3, gates, x1s_a, x1s_b, ys, prm["g2"], prm["b2"])


def _route(meta_a, meta_b, gate_a, gate_b, counts_a, counts_b, tm, n_tiles):
    meta = jnp.concatenate([meta_a, meta_b], axis=1).T
    idx = meta[:, 0:TOP_K]
    rank = meta[:, TOP_K:2 * TOP_K]
    gates = jnp.concatenate([gate_a[0:TOP_K], gate_b[0:TOP_K]], axis=1).T
    cp = counts_a.reshape(-1).astype(jnp.int32)
    cs = counts_b.reshape(-1).astype(jnp.int32)
    is_b = (jnp.arange(meta.shape[0]) >= meta_a.shape[1])[:, None]
    rank = rank + jnp.where(is_b, cp[idx], 0)
    tiles_per = (cp + cs + tm - 1) // tm
    tile_end = jnp.cumsum(tiles_per)
    tile_start = tile_end - tiles_per
    pos = tile_start[idx] * tm + rank
    n_used = tile_end[-1:]
    tile_ids = jnp.arange(n_tiles, dtype=jnp.int32)
    tile_expert = jnp.minimum(jnp.sum(tile_ids[:, None] >= tile_end[None, :], axis=1), N_EXPERTS - 1)
    last_e = tile_expert[jnp.maximum(n_used[0] - 1, 0)]
    tile_expert = jnp.where(tile_ids < n_used[0], tile_expert, last_e).astype(jnp.int32)
    zero_tiles = jnp.where(tiles_per > 0, tile_end - 1, -1).astype(jnp.int32)
    return pos.astype(jnp.int32), gates, tile_expert, n_used.astype(jnp.int32), zero_tiles


def _rope_tables(pos):
    half = ROT_DIM // 2
    inv_freq = ROPE_THETA ** (-jnp.arange(half, dtype=_F32) / half)
    ang = pos.astype(_F32)[:, None] * inv_freq[None, :]
    cos, sin = jnp.cos(ang), jnp.sin(ang)
    n = pos.shape[0]
    ones = jnp.ones((n, HEAD_DIM - ROT_DIM), _F32)
    zeros = jnp.zeros((n, HEAD_DIM - ROT_DIM), _F32)
    zh = jnp.zeros((n, half), _F32)
    cos_t = jnp.concatenate([cos, cos, ones], axis=1)
    sin_up = jnp.concatenate([-sin, zh, zeros], axis=1)
    sin_dn = jnp.concatenate([zh, sin, zeros], axis=1)
    tab = jnp.stack([cos_t, sin_up, sin_dn])
    return jnp.concatenate([tab, tab], axis=2)


def _block_diag(w):
    hd, d, _ = w.shape
    eye = jnp.eye(hd, dtype=w.dtype)
    return (eye[:, None, :, None] * w[:, :, None, :]).reshape(hd * d, hd * d)


def _gate_blocks(wa, wx):
    blocks = []
    head_of = np.arange(D_RNN) // RNN_HEAD_DIM
    for j, k0 in enumerate(GATE_STARTS):
        cols = np.arange(j * GATE_COLS, (j + 1) * GATE_COLS)
        reach = np.flatnonzero(np.isin(head_of, head_of[cols]))
        assert reach.min() >= k0 and reach.max() < k0 + GATE_K
        cs = slice(j * GATE_COLS, (j + 1) * GATE_COLS)
        blocks.append(jnp.concatenate([wa[k0:k0 + GATE_K, cs], wx[k0:k0 + GATE_K, cs]], axis=1))
    return jnp.stack(blocks)


def kernel(x_prompt, x_sample, cache_k, cache_v, state_conv, state_h, w_in, attn_sinks, w_attn_out, conv_w, conv_b, gate_a_w, gate_a_b, gate_x_w, gate_x_b, lru_lambda, w_rnn_out, w_out, ln1_g, ln1_b, router_w, router_b, w_gate_up, b_gate_up, w_down, b_down, ln2_g, ln2_b):
    assert w_in.shape[0] == DEPTH == 1
    l = 0
    bsz, seq, _ = x_prompt.shape
    dbsz, dseq, _ = x_sample.shape
    row = lambda a: a.reshape(1, -1)
    prm = {
        "w_in": (w_in[l] * jnp.where(jnp.arange(IN_DIM) >= OFF_GA, 0.5, 1.0)).astype(_BF16),
        "conv_w": conv_w[l], "conv_b": row(conv_b[l]),
        "wg": (0.5 * _gate_blocks(_block_diag(gate_a_w[l]), _block_diag(gate_x_w[l]))).astype(_BF16),
        "bg": 0.5 * jnp.concatenate([gate_a_b[l].reshape(-1, 1, GATE_COLS),
                                     gate_x_b[l].reshape(-1, 1, GATE_COLS)], axis=2),
        "lam": row(lru_lambda[l]),
        "wao": w_attn_out[l].astype(_BF16), "wro": w_rnn_out[l].astype(_BF16), "wout": (0.5 * w_out[l]).astype(_BF16),
        "g1": row(ln1_g[l]), "b1": row(ln1_b[l]),
        "rwt": router_w[l].T.astype(_BF16), "rb": router_b[l].reshape(-1, 1),
        "wgu": w_gate_up[l], "bgu": b_gate_up[l][:, None, :],
        "wd": w_down[l], "bd": b_down[l][:, None, :],
        "g2": row(ln2_g[l]), "b2": row(ln2_b[l]),
    }
    sinks = attn_sinks[l]

    rope_p = _rope_tables(jnp.arange(seq, dtype=jnp.int32))
    rope_s = _rope_tables(PAST_LEN + jnp.arange(dseq, dtype=jnp.int32))

    n_prompt = bsz * seq
    n_tok = n_prompt + dbsz * dseq
    zeros_kv = jnp.zeros((bsz, WINDOW, KV_DIM), _F32)
    x1_p, meta_p, gate_p, cnt_p, pk, pv, pc, ph = _mixer(
        x_prompt, rope_p, zeros_kv, zeros_kv, jnp.zeros((bsz, CONV_W - 1, D_RNN), _F32),
        jnp.zeros((bsz, 1, D_RNN), _F32), sinks, prm, has_past=False, nb=1, lt=min(MIXER_ROWS, seq), lq=CHUNK)
    x1_s, meta_s, gate_s, cnt_s, sk, sv, sc, sh = _mixer(
        x_sample, rope_s, cache_k[l].reshape(dbsz, WINDOW, KV_DIM), cache_v[l].reshape(dbsz, WINDOW, KV_DIM),
        state_conv[l], state_h[l][:, None, :], sinks, prm, has_past=True, nb=min(MIXER_ROWS // dseq, dbsz),
        lt=dseq, lq=dseq)

    tm = EXPERT_ROWS
    n_tiles = (n_tok * TOP_K + N_EXPERTS * (tm - 1)) // tm + 1
    pos, gates, tile_expert, n_used, zero_tiles = _route(meta_p, meta_s, gate_p, gate_s, cnt_p, cnt_s, tm, n_tiles)
    xs = _dispatch(x1_p, x1_s, pos, zero_tiles, n_used, tt=TOKEN_ROWS, tm=tm, n_tiles=n_tiles)
    ysort = _experts(xs, tile_expert, n_used, prm, tm=tm)
    yp, ys = _combine(x1_p, x1_s, ysort, pos, gates, prm, tt=TOKEN_ROWS)
    yp = yp.reshape(bsz, seq, D_MODEL)
    ys = ys.reshape(dbsz, dseq, D_MODEL)

    kv5 = lambda a: a.reshape(1, a.shape[0], WINDOW, N_KV_HEADS, HEAD_DIM)
    return (yp, ys, kv5(pk), kv5(pv), pc[None], ph.reshape(1, bsz, D_RNN),
            kv5(sk), kv5(sv), sc[None], sh.reshape(1, dbsz, D_RNN))
```

```python
import functools

import jax
import jax.numpy as jnp
import numpy as np
from jax import lax
from jax.experimental import pallas as pl
from jax.experimental.pallas import tpu as pltpu

D_MODEL = 1024
PAST_LEN = 1024
CHUNK = 64
N_HEADS = 16
N_KV_HEADS = 4
HEAD_DIM = 64
GROUP = N_HEADS // N_KV_HEADS
ROT_DIM = HEAD_DIM // 4
ROPE_THETA = 500000.0
WINDOW = 128
ATTN_SCALE = HEAD_DIM ** -0.5
NEG_INF = -1e30
D_RNN = 1280
RNN_HEADS = 16
RNN_HEAD_DIM = D_RNN // RNN_HEADS
CONV_W = 4
LRU_C = 8.0
N_EXPERTS = 32
TOP_K = 4
D_FF = 1024
SWIGLU_ALPHA = 1.702
SWIGLU_LIMIT = 7.0
LN_EPS = 1e-5
DEPTH = 1
DN_ALPHA = (2.0 * DEPTH) ** 0.25

Q_DIM = N_HEADS * HEAD_DIM
KV_DIM = N_KV_HEADS * HEAD_DIM
OFF_K = Q_DIM
OFF_V = OFF_K + KV_DIM
OFF_XR = OFF_V + KV_DIM
OFF_YR = OFF_XR + D_RNN
OFF_GA = OFF_YR + D_RNN
OFF_GR = OFF_GA + D_MODEL
IN_DIM = OFF_GR + D_MODEL

LANES = 128
SLAB = D_MODEL // LANES
MIXER_ROWS = 256
PAST_ROWS = 256
EXPERT_ROWS = 512
TOKEN_ROWS = 256
GELU_C0 = float(np.sqrt(2.0 / np.pi))
GELU_C1 = 0.044715 * GELU_C0
ATT_GROUP = 4
SCAN_S = 4
GATE_COLS = 256
GATE_K = 512
GATE_STARTS = tuple(min(max((j * GATE_COLS // RNN_HEAD_DIM) * RNN_HEAD_DIM // LANES * LANES, 0), D_RNN - GATE_K)
                    for j in range(D_RNN // GATE_COLS))
CONV_HDR = 8
VMEM_LIMIT_BYTES = 56 * 1024 * 1024

_BF16 = jnp.bfloat16
_F32 = jnp.float32


def _dot(a, b):
    return jnp.dot(a, b, preferred_element_type=_F32)


def _dot_nt(a, b):
    return lax.dot_general(a, b, (((1,), (1,)), ((), ())), preferred_element_type=_F32)


def _half_logistic(h):
    return 0.5 * jnp.tanh(h) + 0.5


def _layer_norm(x, g, b):
    mu = jnp.mean(x, axis=-1, keepdims=True)
    xc = x - mu
    var = jnp.mean(xc * xc, axis=-1, keepdims=True)
    return xc * lax.rsqrt(var + LN_EPS) * g + b


def _rope(x, cos_t, sin_up, sin_dn):
    return x * cos_t + pltpu.roll(x, LANES - ROT_DIM // 2, 1) * sin_up + pltpu.roll(x, ROT_DIM // 2, 1) * sin_dn


def _lo_hi(slab, h):
    blk = slab[:, (h // 2) * LANES:(h // 2 + 1) * LANES]
    lane = lax.broadcasted_iota(jnp.int32, blk.shape, 1)
    if h % 2 == 0:
        lo = jnp.where(lane < HEAD_DIM, blk, 0.0)
        hi = pltpu.roll(lo, HEAD_DIM, 1)
    else:
        hi = jnp.where(lane >= HEAD_DIM, blk, 0.0)
        lo = pltpu.roll(hi, HEAD_DIM, 1)
    return lo.astype(_BF16), hi.astype(_BF16)


def _mixer_kernel(has_past, nb, lt, lq,
                  x_ref, rope_ref, w_in_ref, k0_ref, v0_ref, c0_ref, h0_ref, sinkrows_ref,
                  convw_ref, convb_ref, wg_ref, bg_ref, lam_ref, wao_ref, wro_ref, wout_ref,
                  g1_ref, b1_ref, rwt_ref, rb_ref,
                  x1s_ref, meta_ref, gate_ref, cnt_ref, kn_ref, vn_ref, cn_ref, hn_ref,
                  kw_ref, vw_ref, xp_ref, xc_ref, a_ref, b_ref, hin_ref, hc_ref, oat_ref, run_ref):
    s_idx = pl.program_id(1)
    n_s = pl.num_programs(1)
    rows = nb * lt
    n_chunks = lt // lq

    @pl.when(jnp.logical_and(pl.program_id(0) == 0, s_idx == 0))
    def _init_counts():
        run_ref[...] = jnp.zeros_like(run_ref)

    @pl.when(s_idx == 0)
    def _init():
        for bi in range(nb):
            for h in range(N_KV_HEADS):
                klo, khi = _lo_hi(k0_ref[bi], h)
                vlo, vhi = _lo_hi(v0_ref[bi], h)
                kw_ref[bi, h, 0, 0:WINDOW, :] = klo
                kw_ref[bi, h, 1, 0:WINDOW, :] = khi
                vw_ref[bi, h, 0, 0:WINDOW, :] = vlo
                vw_ref[bi, h, 1, 0:WINDOW, :] = vhi
            for c in range(D_RNN // LANES):
                xp_ref[bi, c, CONV_HDR - (CONV_W - 1):CONV_HDR, :] = c0_ref[bi, :, c * LANES:(c + 1) * LANES]
            hc_ref[bi] = h0_ref[bi]

    x = x_ref[...].reshape(rows, D_MODEL)
    xb = x.astype(_BF16)

    cos_t = jnp.concatenate([rope_ref[0]] * nb, axis=0) if nb > 1 else rope_ref[0]
    sin_up = jnp.concatenate([rope_ref[1]] * nb, axis=0) if nb > 1 else rope_ref[1]
    sin_dn = jnp.concatenate([rope_ref[2]] * nb, axis=0) if nb > 1 else rope_ref[2]

    zq = _dot(xb, w_in_ref[:, 0:Q_DIM])
    q_blocks = []
    for c in range(Q_DIM // LANES):
        qr = _rope(zq[:, c * LANES:(c + 1) * LANES], cos_t, sin_up, sin_dn)
        q_blocks.append((qr * ATTN_SCALE).astype(_BF16))
    zk = _dot(xb, w_in_ref[:, OFF_K:OFF_V])
    k_rot = jnp.concatenate(
        [_rope(zk[:, c * LANES:(c + 1) * LANES], cos_t, sin_up, sin_dn) for c in range(KV_DIM // LANES)], axis=1)
    v_new = _dot(xb, w_in_ref[:, OFF_V:OFF_XR])

    for bi in range(nb):
        r0 = bi * lt
        for h in range(N_KV_HEADS):
            klo, khi = _lo_hi(k_rot[r0:r0 + lt], h)
            vlo, vhi = _lo_hi(v_new[r0:r0 + lt], h)
            kw_ref[bi, h, 0, WINDOW:WINDOW + lt, :] = klo
            kw_ref[bi, h, 1, WINDOW:WINDOW + lt, :] = khi
            vw_ref[bi, h, 0, WINDOW:WINDOW + lt, :] = vlo
            vw_ref[bi, h, 1, WINDOW:WINDOW + lt, :] = vhi

    for bi in range(nb):
        r0 = bi * lt
        if lt >= WINDOW:
            kn_ref[bi] = k_rot[r0 + lt - WINDOW:r0 + lt]
            vn_ref[bi] = v_new[r0 + lt - WINDOW:r0 + lt]
        else:
            kn_ref[bi, 0:WINDOW - lt, :] = k0_ref[bi, lt:WINDOW, :]
            kn_ref[bi, WINDOW - lt:WINDOW, :] = k_rot[r0:r0 + lt]
            vn_ref[bi, 0:WINDOW - lt, :] = v0_ref[bi, lt:WINDOW, :]
            vn_ref[bi, WINDOW - lt:WINDOW, :] = v_new[r0:r0 + lt]

    w2 = 2 * lq
    wk = 2 * WINDOW + w2
    r_all = N_KV_HEADS * 2 * lq
    is_a2 = lax.broadcasted_iota(jnp.int32, (r_all, w2), 1) < lq
    lane_lo = lax.broadcasted_iota(jnp.int32, (r_all, LANES), 1) < HEAD_DIM
    key_row = lax.broadcasted_iota(jnp.int32, (wk, LANES), 0)
    key_is_a = jnp.logical_or(key_row < WINDOW, jnp.logical_and(key_row >= 2 * WINDOW, key_row < 2 * WINDOW + lq))
    key_lane_lo = lax.broadcasted_iota(jnp.int32, (wk, LANES), 1) < HEAD_DIM
    head_ones = jnp.where(key_is_a == key_lane_lo, 1.0, 0.0).astype(_BF16)
    sink_rows, sink_a_rows, sink_b_rows = sinkrows_ref[0], sinkrows_ref[1], sinkrows_ref[2]
    first = s_idx == 0
    units = [(bi, j) for bi in range(nb) for j in range(n_chunks)]
    for g0 in range(0, len(units), ATT_GROUP):
        group = units[g0:g0 + ATT_GROUP]
        scores = []
        vmats = []
        for bi, j in group:
            p0 = j * lq
            o0 = WINDOW + j * lq
            q0 = bi * lt + j * lq
            for h in range(N_KV_HEADS):
                kmat = jnp.concatenate([kw_ref[bi, h, 0, p0:p0 + WINDOW, :], kw_ref[bi, h, 1, p0:p0 + WINDOW, :],
                                        kw_ref[bi, h, 0, o0:o0 + lq, :], kw_ref[bi, h, 1, o0:o0 + lq, :]], axis=0)
                vmat = jnp.concatenate([vw_ref[bi, h, 0, p0:p0 + WINDOW, :], vw_ref[bi, h, 1, p0:p0 + WINDOW, :],
                                        vw_ref[bi, h, 0, o0:o0 + lq, :], vw_ref[bi, h, 1, o0:o0 + lq, :]], axis=0)
                vmats.append(jnp.concatenate([vmat, head_ones], axis=1))
                qs = jnp.concatenate([q_blocks[2 * h][q0:q0 + lq], q_blocks[2 * h + 1][q0:q0 + lq]], axis=0)
                sc_h = _dot_nt(qs, kmat)
                if not has_past and j * lq < WINDOW:
                    col = lax.broadcasted_iota(jnp.int32, sc_h.shape, 1)
                    n_bad = WINDOW - j * lq
                    bad = jnp.logical_or(col < n_bad, jnp.logical_and(col >= WINDOW, col < WINDOW + n_bad))
                    sc_h = jnp.where(jnp.logical_and(first, bad), NEG_INF, sc_h)
                scores.append(sc_h)
        n_u = len(group)
        sc = jnp.concatenate(scores, axis=0)
        if g0 == 0:
            ga = _dot(xb, w_in_ref[:, OFF_GA:OFF_GR])
            gr = _dot(xb, w_in_ref[:, OFF_GR:IN_DIM])
        tile_rows = lambda a: jnp.concatenate([a] * n_u, axis=0) if n_u > 1 else a
        is_a2g, lane_log, sinks_g = tile_rows(is_a2), tile_rows(lane_lo), tile_rows(sink_rows)
        sinks_ag, sinks_bg = tile_rows(sink_a_rows), tile_rows(sink_b_rows)
        c0 = sc[:, 0:LANES]
        c1 = sc[:, LANES:2 * LANES]
        c2 = sc[:, 2 * LANES:wk]
        if w2 == LANES:
            ma = jnp.max(jnp.maximum(c0, jnp.where(is_a2g, c2, NEG_INF)).astype(_BF16), axis=1, keepdims=True)
            mb = jnp.max(jnp.maximum(c1, jnp.where(is_a2g, NEG_INF, c2)).astype(_BF16), axis=1, keepdims=True)
        else:
            ma = jnp.maximum(jnp.max(c0.astype(_BF16), axis=1, keepdims=True),
                             jnp.max(jnp.where(is_a2g, c2, NEG_INF).astype(_BF16), axis=1, keepdims=True))
            mb = jnp.maximum(jnp.max(c1.astype(_BF16), axis=1, keepdims=True),
                             jnp.max(jnp.where(is_a2g, NEG_INF, c2).astype(_BF16), axis=1, keepdims=True))
        ma = jnp.maximum(jnp.broadcast_to(ma.astype(_F32), sinks_ag.shape), sinks_ag)
        mb = jnp.maximum(jnp.broadcast_to(mb.astype(_F32), sinks_bg.shape), sinks_bg)
        probs = jnp.concatenate([jnp.exp(c0 - ma), jnp.exp(c1 - mb),
                                 jnp.exp(c2 - jnp.where(is_a2g, ma[:, 0:w2], mb[:, 0:w2]))], axis=1).astype(_BF16)
        sink_term = jnp.exp(sinks_g - jnp.where(lane_log, ma, mb))
        for u, (bi, j) in enumerate(group):
            q0 = bi * lt + j * lq
            for h in range(N_KV_HEADS):
                r0 = (u * N_KV_HEADS + h) * 2 * lq
                od = _dot(probs[r0:r0 + 2 * lq], vmats[u * N_KV_HEADS + h])
                out = od[:, 0:LANES] / (od[:, LANES:2 * LANES] + sink_term[r0:r0 + 2 * lq])
                for p in range(2):
                    oat_ref[q0:q0 + lq, (2 * h + p) * LANES:(2 * h + p + 1) * LANES] = (
                        out[p * lq:(p + 1) * lq].astype(_BF16))

    if lt >= WINDOW:
        @pl.when(s_idx + 1 < n_s)
        def _carry_kv():
            for bi in range(nb):
                for h in range(N_KV_HEADS):
                    for v in range(2):
                        kw_ref[bi, h, v, 0:WINDOW, :] = kw_ref[bi, h, v, lt:lt + WINDOW, :]
                        vw_ref[bi, h, v, 0:WINDOW, :] = vw_ref[bi, h, v, lt:lt + WINDOW, :]

    xr = _dot(xb, w_in_ref[:, OFF_XR:OFF_YR])
    n_lb = D_RNN // LANES
    n_g = lt // SCAN_S
    conv_w = convw_ref[...]
    conv_b = convb_ref[...]
    for bi in range(nb):
        r0 = bi * lt
        tail = xr[r0 + lt - (CONV_W - 1):r0 + lt]
        cn_ref[bi] = tail
        for c in range(n_lb):
            ls = slice(c * LANES, (c + 1) * LANES)
            xp_ref[bi, c, CONV_HDR:CONV_HDR + lt, :] = xr[r0:r0 + lt, ls]
            frames = [xp_ref[bi, c, pl.ds(CONV_HDR + m, n_g, stride=SCAN_S), :]
                      for m in range(-(CONV_W - 1), SCAN_S)]
            for k in range(SCAN_S):
                acc = frames[k] * conv_w[0:1, ls]
                for t in range(1, CONV_W):
                    acc = acc + frames[k + t] * conv_w[t:t + 1, ls]
                xc_ref[c, pl.ds(r0 + k, n_g, stride=SCAN_S), :] = acc + conv_b[:, ls]
            xp_ref[bi, c, CONV_HDR - (CONV_W - 1):CONV_HDR, :] = tail[:, ls]

    yr = _dot(xb, w_in_ref[:, OFF_YR:OFF_GA])
    xc = jnp.concatenate([xc_ref[c] for c in range(n_lb)], axis=1)
    xcb = xc.astype(_BF16)
    lam = lam_ref[...]
    softplus_neg = jnp.maximum(-lam, 0.0) + jnp.log(1.0 + jnp.exp(-jnp.abs(lam)))
    for j in range(D_RNN // GATE_COLS):
        cs = slice(j * GATE_COLS, (j + 1) * GATE_COLS)
        g = _dot(xcb[:, GATE_STARTS[j]:GATE_STARTS[j] + GATE_K], wg_ref[j]) + bg_ref[j]
        i_gate = _half_logistic(g[:, GATE_COLS:2 * GATE_COLS])
        m_sp = (-0.5 * LRU_C) * softplus_neg[:, cs]
        log_a = jnp.tanh(g[:, 0:GATE_COLS]) * m_sp + m_sp
        a = jnp.exp(log_a)
        z = -jnp.tanh(log_a) * (1.0 + a * a)
        bv = jnp.where(z > 0.0, z * lax.rsqrt(z), 0.0) * i_gate * xc[:, cs]
        for cc in range(GATE_COLS // LANES):
            c = j * (GATE_COLS // LANES) + cc
            a_ref[c] = a[:, cc * LANES:(cc + 1) * LANES]
            b_ref[c] = bv[:, cc * LANES:(cc + 1) * LANES]

    for bi in range(nb):
        r0 = bi * lt
        a_tot, h_tot = [], []
        for c in range(n_lb):
            at = a_ref[c, pl.ds(r0, n_g, stride=SCAN_S), :]
            ht = b_ref[c, pl.ds(r0, n_g, stride=SCAN_S), :]
            for k in range(1, SCAN_S):
                ak = a_ref[c, pl.ds(r0 + k, n_g, stride=SCAN_S), :]
                ht = ak * ht + b_ref[c, pl.ds(r0 + k, n_g, stride=SCAN_S), :]
                at = ak * at
            a_tot.append(at)
            h_tot.append(ht)
        carry = [hc_ref[bi][:, c * LANES:(c + 1) * LANES] for c in range(n_lb)]
        for g in range(n_g):
            for c in range(n_lb):
                hin_ref[c, g:g + 1, :] = carry[c]
                carry[c] = a_tot[c][g:g + 1, :] * carry[c] + h_tot[c][g:g + 1, :]
        h_last = jnp.concatenate(carry, axis=1)
        hc_ref[bi] = h_last
        hn_ref[bi] = h_last
        for c in range(n_lb):
            hcur = hin_ref[c]
            for k in range(SCAN_S):
                hcur = a_ref[c, pl.ds(r0 + k, n_g, stride=SCAN_S), :] * hcur + b_ref[c, pl.ds(r0 + k, n_g, stride=SCAN_S), :]
                b_ref[c, pl.ds(r0 + k, n_g, stride=SCAN_S), :] = hcur

    gelu_t = jnp.tanh(yr * (GELU_C0 + GELU_C1 * (yr * yr)))
    half_h = jnp.concatenate([b_ref[c] for c in range(n_lb)], axis=1) * (0.5 * yr)
    rnn = half_h * gelu_t + half_h

    merged2 = ((jnp.tanh(ga) + 1.0) * _dot(oat_ref[...], wao_ref[...])
               + (jnp.tanh(gr) + 1.0) * _dot(rnn.astype(_BF16), wro_ref[...]))
    mix = _dot(merged2.astype(_BF16), wout_ref[...])
    x1 = _layer_norm(DN_ALPHA * x + mix, g1_ref[...], b1_ref[...])
    for c in range(SLAB):
        x1s_ref[pl.ds(c, rows, stride=SLAB), :] = x1[:, c * LANES:(c + 1) * LANES]

    logits = _dot_nt(rwt_ref[...], x1.astype(_BF16)) + rb_ref[...]
    e_iota = lax.broadcasted_iota(jnp.int32, logits.shape, 0)
    work = logits
    top_vals = []
    top_idx = []
    top_sel = []
    for _ in range(TOP_K):
        m = jnp.max(work, axis=0, keepdims=True)
        idx = jnp.min(jnp.where(work == m, e_iota, N_EXPERTS), axis=0, keepdims=True)
        sel = e_iota == idx
        top_vals.append(m)
        top_idx.append(idx)
        top_sel.append(sel)
        work = jnp.where(sel, -jnp.inf, work)
    exps = [jnp.exp(v - top_vals[0]) for v in top_vals]
    denom = exps[0]
    for e in exps[1:]:
        denom = denom + e

    onehot = jnp.zeros_like(logits)
    for sel in top_sel:
        onehot = onehot + jnp.where(sel, 1.0, 0.0)
    r_i = lax.broadcasted_iota(jnp.int32, (rows, rows), 0)
    c_i = lax.broadcasted_iota(jnp.int32, (rows, rows), 1)
    earlier = jnp.where(r_i < c_i, 1.0, 0.0).astype(_BF16)
    before = _dot(onehot.astype(_BF16), earlier) + run_ref[...]
    run_ref[...] = run_ref[...] + jnp.sum(onehot, axis=1, keepdims=True)
    cnt_ref[...] = run_ref[...]

    sub_m = lax.broadcasted_iota(jnp.int32, (2 * TOP_K, rows), 0)
    meta_i = jnp.zeros((2 * TOP_K, rows), jnp.int32)
    meta_g = jnp.zeros((2 * TOP_K, rows), _F32)
    for k in range(TOP_K):
        rank_k = jnp.sum(jnp.where(top_sel[k], before, 0.0), axis=0, keepdims=True).astype(jnp.int32)
        meta_i = jnp.where(sub_m == k, top_idx[k], meta_i)
        meta_i = jnp.where(sub_m == TOP_K + k, rank_k, meta_i)
        meta_g = jnp.where(sub_m == k, exps[k] / denom, meta_g)
    meta_ref[...] = meta_i
    gate_ref[...] = meta_g


def _const_spec(shape):
    nd = len(shape)
    return pl.BlockSpec(shape, lambda b, s, _nd=nd: (0,) * _nd, pipeline_mode=pl.Buffered(1))


def _mixer(x, rope, k0, v0, c0, h0, sinks, prm, *, has_past, nb, lt, lq):
    bsz, seq, _ = x.shape
    assert bsz % nb == 0 and seq % lt == 0 and lt % lq == 0 and lq % 32 == 0
    assert lt >= WINDOW or seq == lt
    n_s = seq // lt
    rows = nb * lt
    n_tok = bsz * seq
    assert lt % SCAN_S == 0
    kernel = functools.partial(_mixer_kernel, has_past, nb, lt, lq)
    r_all = N_KV_HEADS * 2 * lq
    sink_rows = jnp.broadcast_to(sinks.reshape(N_KV_HEADS * 2, 1, 2, 1), (N_KV_HEADS * 2, lq, 2, HEAD_DIM))
    sink_rows = sink_rows.reshape(r_all, LANES)
    sink_ab = jnp.broadcast_to(sinks.reshape(N_KV_HEADS * 2, 1, 2, 1), (N_KV_HEADS * 2, lq, 2, LANES))
    sink_rows = jnp.stack([sink_rows, sink_ab[:, :, 0].reshape(r_all, LANES), sink_ab[:, :, 1].reshape(r_all, LANES)])
    n_gate = D_RNN // GATE_COLS
    batch_spec = lambda shape: pl.BlockSpec((nb,) + shape, lambda b, s: (b,) + (0,) * len(shape))
    in_specs = [
        pl.BlockSpec((nb, lt, D_MODEL), lambda b, s: (b, s, 0)),
        pl.BlockSpec((3, lt, LANES), lambda b, s: (0, s, 0)),
        _const_spec((D_MODEL, IN_DIM)),
        batch_spec((WINDOW, KV_DIM)), batch_spec((WINDOW, KV_DIM)),
        batch_spec((CONV_W - 1, D_RNN)), batch_spec((1, D_RNN)),
        _const_spec((3, r_all, LANES)),
        _const_spec((CONV_W, D_RNN)), _const_spec((1, D_RNN)),
        _const_spec((n_gate, GATE_K, 2 * GATE_COLS)), _const_spec((n_gate, 1, 2 * GATE_COLS)), _const_spec((1, D_RNN)),
        _const_spec((Q_DIM, D_MODEL)), _const_spec((D_RNN, D_MODEL)), _const_spec((D_MODEL, D_MODEL)),
        _const_spec((1, D_MODEL)), _const_spec((1, D_MODEL)),
        _const_spec((N_EXPERTS, D_MODEL)), _const_spec((N_EXPERTS, 1)),
    ]
    tok_map = lambda b, s: (b * n_s + s, 0)
    tok_map_t = lambda b, s: (0, b * n_s + s)
    out_shape = (
        jax.ShapeDtypeStruct((n_tok * SLAB, LANES), _F32),
        jax.ShapeDtypeStruct((2 * TOP_K, n_tok), jnp.int32),
        jax.ShapeDtypeStruct((2 * TOP_K, n_tok), _F32),
        jax.ShapeDtypeStruct((N_EXPERTS, 1), _F32),
        jax.ShapeDtypeStruct((bsz, WINDOW, KV_DIM), _F32),
        jax.ShapeDtypeStruct((bsz, WINDOW, KV_DIM), _F32),
        jax.ShapeDtypeStruct((bsz, CONV_W - 1, D_RNN), _F32),
        jax.ShapeDtypeStruct((bsz, 1, D_RNN), _F32),
    )
    out_specs = (
        pl.BlockSpec((rows * SLAB, LANES), tok_map),
        pl.BlockSpec((2 * TOP_K, rows), tok_map_t),
        pl.BlockSpec((2 * TOP_K, rows), tok_map_t),
        pl.BlockSpec((N_EXPERTS, 1), lambda b, s: (0, 0)),
        batch_spec((WINDOW, KV_DIM)), batch_spec((WINDOW, KV_DIM)),
        batch_spec((CONV_W - 1, D_RNN)), batch_spec((1, D_RNN)),
    )
    scratch = [
        pltpu.VMEM((nb, N_KV_HEADS, 2, WINDOW + lt, LANES), _BF16),
        pltpu.VMEM((nb, N_KV_HEADS, 2, WINDOW + lt, LANES), _BF16),
        pltpu.VMEM((nb, D_RNN // LANES, CONV_HDR + lt, LANES), _F32),
        pltpu.VMEM((D_RNN // LANES, rows, LANES), _F32),
        pltpu.VMEM((D_RNN // LANES, rows, LANES), _F32),
        pltpu.VMEM((D_RNN // LANES, rows, LANES), _F32),
        pltpu.VMEM((D_RNN // LANES, lt // SCAN_S, LANES), _F32),
        pltpu.VMEM((nb, 1, D_RNN), _F32),
        pltpu.VMEM((rows, Q_DIM), _BF16),
        pltpu.VMEM((N_EXPERTS, 1), _F32),
    ]
    args = [x, rope, prm["w_in"], k0, v0, c0, h0, sink_rows, prm["conv_w"], prm["conv_b"], prm["wg"], prm["bg"],
            prm["lam"], prm["wao"], prm["wro"], prm["wout"], prm["g1"], prm["b1"], prm["rwt"], prm["rb"]]
    return pl.pallas_call(
        kernel,
        grid=(bsz // nb, n_s),
        in_specs=in_specs,
        out_specs=out_specs,
        out_shape=out_shape,
        scratch_shapes=scratch,
        compiler_params=pltpu.CompilerParams(
            dimension_semantics=("arbitrary", "arbitrary"), vmem_limit_bytes=VMEM_LIMIT_BYTES),
        name="mixer_past" if has_past else "mixer_prompt",
    )(*args)


def _slab_rows(ref, first_tok, n):
    return jnp.concatenate([ref[pl.ds(first_tok * SLAB + c, n, stride=SLAB), :] for c in range(SLAB)], axis=1)


def _dispatch_kernel(tt, tm, n_first, n_tiles, zt_ref, nu_ref, pos_ref, xa_ref, xb_ref, xs_hbm, zero_ref, sem):
    i = pl.program_id(0)

    @pl.when(i == 0)
    def _zero_padding():
        zero_ref[...] = jnp.zeros_like(zero_ref)

        def clear(tile):
            dst = xs_hbm.at[pl.ds(pl.multiple_of(tile * (tm * SLAB), SLAB), tm * SLAB)]
            cp = pltpu.make_async_copy(zero_ref, dst, sem)
            cp.start()
            cp.wait()

        for e in range(N_EXPERTS):
            pl.when(zt_ref[e] >= 0)(functools.partial(clear, zt_ref[e]))

        def clear_unused(tile, carry):
            clear(tile)
            return carry

        lax.fori_loop(nu_ref[0], n_tiles, clear_unused, 0)

    def scatter(src_ref):
        def issue(r, carry):
            src = src_ref.at[pl.ds(pl.multiple_of(r * SLAB, SLAB), SLAB)]
            for k in range(TOP_K):
                dst = xs_hbm.at[pl.ds(pl.multiple_of(pos_ref[0, 0, r * TOP_K + k] * SLAB, SLAB), SLAB)]
                pltpu.make_async_copy(src, dst, sem).start(priority=k % 2)
            return carry

        lax.fori_loop(0, tt, issue, 0)

    pl.when(i < n_first)(lambda: scatter(xa_ref))
    pl.when(i >= n_first)(lambda: scatter(xb_ref))
    n_rows = tt * TOP_K * SLAB
    pltpu.make_async_copy(xs_hbm.at[pl.ds(0, n_rows)], xs_hbm.at[pl.ds(0, n_rows)], sem).wait()


def _dispatch(x1s_a, x1s_b, pos, zero_tiles, n_used, *, tt, tm, n_tiles):
    n_a = x1s_a.shape[0] // SLAB
    n_b = x1s_b.shape[0] // SLAB
    assert n_a % tt == 0 and n_b % tt == 0 and n_tiles * tm >= tt * TOP_K
    n_first = n_a // tt
    n_steps = (n_a + n_b) // tt
    return pl.pallas_call(
        functools.partial(_dispatch_kernel, tt, tm, n_first, n_tiles),
        grid_spec=pltpu.PrefetchScalarGridSpec(
            num_scalar_prefetch=2,
            grid=(n_steps,),
            in_specs=[
                pl.BlockSpec((1, 1, tt * TOP_K), lambda i, zt, nu: (i, 0, 0), memory_space=pltpu.SMEM),
                pl.BlockSpec((tt * SLAB, LANES), lambda i, zt, nu: (jnp.minimum(i, n_first - 1), 0)),
                pl.BlockSpec((tt * SLAB, LANES), lambda i, zt, nu: (jnp.maximum(i - n_first, 0), 0)),
            ],
            out_specs=pl.BlockSpec(memory_space=pl.ANY),
            scratch_shapes=[pltpu.VMEM((tm * SLAB, LANES), _F32), pltpu.SemaphoreType.DMA],
        ),
        out_shape=jax.ShapeDtypeStruct((n_tiles * tm * SLAB, LANES), _F32),
        compiler_params=pltpu.CompilerParams(dimension_semantics=("arbitrary",)),
        name="moe_dispatch",
    )(zero_tiles, n_used, pos.reshape(n_steps, 1, tt * TOP_K), x1s_a, x1s_b)


def _expert_kernel(tm, te_ref, nu_ref, xs_ref, wgu_ref, bgu_ref, wd_ref, bd_ref, ys_ref, wgu_b, wd_b):
    i = pl.program_id(0)

    @pl.when(i < nu_ref[0])
    def _tile():
        prev_e = te_ref[jnp.maximum(i - 1, 0)]

        @pl.when(jnp.logical_or(i == 0, te_ref[i] != prev_e))
        def _new_expert():
            wgu_b[...] = wgu_ref[0].astype(_BF16)
            wd_b[...] = wd_ref[0].astype(_BF16)

        x = _slab_rows(xs_ref, 0, tm).astype(_BF16)
        hgu = _dot(x, wgu_b[...]) + bgu_ref[0]
        glu = jnp.minimum(hgu[:, 0:D_FF], SWIGLU_LIMIT)
        lin = jnp.clip(hgu[:, D_FF:2 * D_FF], -SWIGLU_LIMIT, SWIGLU_LIMIT)
        hh = glu * _half_logistic((0.5 * SWIGLU_ALPHA) * glu) * (lin + 1.0)
        y = _dot(hh.astype(_BF16), wd_b[...]) + bd_ref[0]
        for c in range(SLAB):
            ys_ref[pl.ds(c, tm, stride=SLAB), :] = y[:, c * LANES:(c + 1) * LANES]

    @pl.when(i >= nu_ref[0])
    def _unused_tile():
        ys_ref[...] = jnp.zeros_like(ys_ref)


def _experts(xs, tile_expert, n_used, prm, *, tm):
    n_tiles = xs.shape[0] // (tm * SLAB)
    row_map = lambda i, te, nu: (jnp.minimum(i, nu[0] - 1), 0)
    out_map = lambda i, te, nu: (i, 0)
    exp_map = lambda i, te, nu: (te[i], 0, 0)
    return pl.pallas_call(
        functools.partial(_expert_kernel, tm),
        grid_spec=pltpu.PrefetchScalarGridSpec(
            num_scalar_prefetch=2,
            grid=(n_tiles,),
            in_specs=[
                pl.BlockSpec((tm * SLAB, LANES), row_map),
                pl.BlockSpec((1, D_MODEL, 2 * D_FF), exp_map),
                pl.BlockSpec((1, 1, 2 * D_FF), exp_map),
                pl.BlockSpec((1, D_FF, D_MODEL), exp_map),
                pl.BlockSpec((1, 1, D_MODEL), exp_map),
            ],
            out_specs=pl.BlockSpec((tm * SLAB, LANES), out_map),
            scratch_shapes=[pltpu.VMEM((D_MODEL, 2 * D_FF), _BF16), pltpu.VMEM((D_FF, D_MODEL), _BF16)],
        ),
        out_shape=jax.ShapeDtypeStruct(xs.shape, _F32),
        compiler_params=pltpu.CompilerParams(
            dimension_semantics=("arbitrary",), vmem_limit_bytes=VMEM_LIMIT_BYTES),
        name="moe_experts",
    )(tile_expert, n_used, xs, prm["wgu"], prm["bgu"], prm["wd"], prm["bd"])


def _combine_kernel(tt, n_first, n_steps, pos_ref, pos_next_ref, gate_ref, xa_ref, xb_ref, ys_hbm, g2_ref, b2_ref,
                    ya_ref, yb_ref, ybuf, sems):
    i = pl.program_id(0)
    n_rows = tt * TOP_K * SLAB

    def gather(p_ref, slot):
        def issue(r, carry):
            for k in range(TOP_K):
                src = ys_hbm.at[pl.ds(pl.multiple_of(p_ref[0, 0, r * TOP_K + k] * SLAB, SLAB), SLAB)]
                dst = ybuf.at[slot, pl.ds(pl.multiple_of((k * tt + r) * SLAB, SLAB), SLAB)]
                pltpu.make_async_copy(src, dst, sems.at[slot]).start(priority=k % 2)
            return carry

        lax.fori_loop(0, tt, issue, 0)

    pl.when(i == 0)(lambda: gather(pos_ref, 0))

    def tile(slot):
        pl.when(i + 1 < n_steps)(lambda: gather(pos_next_ref, 1 - slot))
        pltpu.make_async_copy(ys_hbm.at[pl.ds(0, n_rows)], ybuf.at[slot], sems.at[slot]).wait()
        rows_ref = ybuf.at[slot]
        gates = gate_ref[...]
        acc = gates[:, 0:1] * _slab_rows(rows_ref, 0, tt)
        for k in range(1, TOP_K):
            acc = acc + gates[:, k:k + 1] * _slab_rows(rows_ref, k * tt, tt)

        def finish(x_ref, y_ref):
            x1 = _slab_rows(x_ref, 0, tt)
            y_ref[...] = _layer_norm(DN_ALPHA * x1 + acc, g2_ref[...], b2_ref[...])

        pl.when(i < n_first)(lambda: finish(xa_ref, ya_ref))
        pl.when(i >= n_first)(lambda: finish(xb_ref, yb_ref))

    for slot in range(2):
        pl.when(i % 2 == slot)(functools.partial(tile, slot))


def _combine(x1s_a, x1s_b, ys, pos, gates, prm, *, tt):
    n_a = x1s_a.shape[0] // SLAB
    n_b = x1s_b.shape[0] // SLAB
    assert n_a % tt == 0 and n_b % tt == 0
    n_first = n_a // tt
    n_steps = (n_a + n_b) // tt
    a_map = lambda i: (jnp.minimum(i, n_first - 1), 0)
    b_map = lambda i: (jnp.maximum(i - n_first, 0), 0)
    pos3 = pos.reshape(n_steps, 1, tt * TOP_K)
    return pl.pallas_call(
        functools.partial(_combine_kernel, tt, n_first, n_steps),
        grid=(n_steps,),
        in_specs=[
            pl.BlockSpec((1, 1, tt * TOP_K), lambda i: (i, 0, 0), memory_space=pltpu.SMEM),
            pl.BlockSpec((1, 1, tt * TOP_K), lambda i: (jnp.minimum(i + 1, n_steps - 1), 0, 0),
                         memory_space=pltpu.SMEM),
            pl.BlockSpec((tt, TOP_K), lambda i: (i, 0)),
            pl.BlockSpec((tt * SLAB, LANES), a_map),
            pl.BlockSpec((tt * SLAB, LANES), b_map),
            pl.BlockSpec(memory_space=pl.ANY),
            pl.BlockSpec((1, D_MODEL), lambda i: (0, 0)),
            pl.BlockSpec((1, D_MODEL), lambda i: (0, 0)),
        ],
        out_specs=(pl.BlockSpec((tt, D_MODEL), a_map), pl.BlockSpec((tt, D_MODEL), b_map)),
        out_shape=(jax.ShapeDtypeStruct((n_a, D_MODEL), _F32), jax.ShapeDtypeStruct((n_b, D_MODEL), _F32)),
        scratch_shapes=[pltpu.VMEM((2, TOP_K * tt * SLAB, LANES), _F32), pltpu.SemaphoreType.DMA((2,))],
        compiler_params=pltpu.CompilerParams(
            dimension_semantics=("arbitrary",), vmem_limit_bytes=VMEM_LIMIT_BYTES),
        name="moe_combine",
    )(pos3, pos3, gates, x1s_a, x1s_b, ys, prm["g2"], prm["b2"])


def _route(meta_a, meta_b, gate_a, gate_b, counts_a, counts_b, tm, n_tiles):
    meta = jnp.concatenate([meta_a, meta_b], axis=1).T
    idx = meta[:, 0:TOP_K]
    rank = meta[:, TOP_K:2 * TOP_K]
    gates = jnp.concatenate([gate_a[0:TOP_K], gate_b[0:TOP_K]], axis=1).T
    cp = counts_a.reshape(-1).astype(jnp.int32)
    cs = counts_b.reshape(-1).astype(jnp.int32)
    is_b = (jnp.arange(meta.shape[0]) >= meta_a.shape[1])[:, None]
    is_e = idx[:, :, None] == jnp.arange(N_EXPERTS, dtype=idx.dtype)
    lookup = lambda table: jnp.sum(jnp.where(is_e, table, 0), axis=-1)
    rank = rank + jnp.where(is_b, lookup(cp), 0)
    tiles_per = (cp + cs + tm - 1) // tm
    tile_end = jnp.cumsum(tiles_per)
    tile_start = tile_end - tiles_per
    pos = lookup(tile_start) * tm + rank
    n_used = tile_end[-1:]
    tile_ids = jnp.arange(n_tiles, dtype=jnp.int32)
    tile_expert = jnp.minimum(jnp.sum(tile_ids[:, None] >= tile_end[None, :], axis=1), N_EXPERTS - 1)
    last_e = tile_expert[jnp.maximum(n_used[0] - 1, 0)]
    tile_expert = jnp.where(tile_ids < n_used[0], tile_expert, last_e).astype(jnp.int32)
    zero_tiles = jnp.where(tiles_per > 0, tile_end - 1, -1).astype(jnp.int32)
    return pos.astype(jnp.int32), gates, tile_expert, n_used.astype(jnp.int32), zero_tiles


def _rope_tables(pos):
    half = ROT_DIM // 2
    inv_freq = ROPE_THETA ** (-jnp.arange(half, dtype=_F32) / half)
    ang = pos.astype(_F32)[:, None] * inv_freq[None, :]
    cos, sin = jnp.cos(ang), jnp.sin(ang)
    n = pos.shape[0]
    ones = jnp.ones((n, HEAD_DIM - ROT_DIM), _F32)
    zeros = jnp.zeros((n, HEAD_DIM - ROT_DIM), _F32)
    zh = jnp.zeros((n, half), _F32)
    cos_t = jnp.concatenate([cos, cos, ones], axis=1)
    sin_up = jnp.concatenate([-sin, zh, zeros], axis=1)
    sin_dn = jnp.concatenate([zh, sin, zeros], axis=1)
    tab = jnp.stack([cos_t, sin_up, sin_dn])
    return jnp.concatenate([tab, tab], axis=2)


def _block_diag(w):
    hd, d, _ = w.shape
    eye = jnp.eye(hd, dtype=w.dtype)
    return (eye[:, None, :, None] * w[:, :, None, :]).reshape(hd * d, hd * d)


def _gate_blocks(wa, wx):
    blocks = []
    head_of = np.arange(D_RNN) // RNN_HEAD_DIM
    for j, k0 in enumerate(GATE_STARTS):
        cols = np.arange(j * GATE_COLS, (j + 1) * GATE_COLS)
        reach = np.flatnonzero(np.isin(head_of, head_of[cols]))
        assert reach.min() >= k0 and reach.max() < k0 + GATE_K
        cs = slice(j * GATE_COLS, (j + 1) * GATE_COLS)
        blocks.append(jnp.concatenate([wa[k0:k0 + GATE_K, cs], wx[k0:k0 + GATE_K, cs]], axis=1))
    return jnp.stack(blocks)


def kernel(x_prompt, x_sample, cache_k, cache_v, state_conv, state_h, w_in, attn_sinks, w_attn_out, conv_w, conv_b, gate_a_w, gate_a_b, gate_x_w, gate_x_b, lru_lambda, w_rnn_out, w_out, ln1_g, ln1_b, router_w, router_b, w_gate_up, b_gate_up, w_down, b_down, ln2_g, ln2_b):
    assert w_in.shape[0] == DEPTH == 1
    l = 0
    bsz, seq, _ = x_prompt.shape
    dbsz, dseq, _ = x_sample.shape
    row = lambda a: a.reshape(1, -1)
    prm = {
        "w_in": (w_in[l] * jnp.where(jnp.arange(IN_DIM) >= OFF_GA, 0.5, 1.0)).astype(_BF16),
        "conv_w": conv_w[l], "conv_b": row(conv_b[l]),
        "wg": (0.5 * _gate_blocks(_block_diag(gate_a_w[l]), _block_diag(gate_x_w[l]))).astype(_BF16),
        "bg": 0.5 * jnp.concatenate([gate_a_b[l].reshape(-1, 1, GATE_COLS),
                                     gate_x_b[l].reshape(-1, 1, GATE_COLS)], axis=2),
        "lam": row(lru_lambda[l]),
        "wao": w_attn_out[l].astype(_BF16), "wro": w_rnn_out[l].astype(_BF16), "wout": (0.5 * w_out[l]).astype(_BF16),
        "g1": row(ln1_g[l]), "b1": row(ln1_b[l]),
        "rwt": router_w[l].T.astype(_BF16), "rb": router_b[l].reshape(-1, 1),
        "wgu": w_gate_up[l], "bgu": b_gate_up[l][:, None, :],
        "wd": w_down[l], "bd": b_down[l][:, None, :],
        "g2": row(ln2_g[l]), "b2": row(ln2_b[l]),
    }
    sinks = attn_sinks[l]

    rope_p = _rope_tables(jnp.arange(seq, dtype=jnp.int32))
    rope_s = _rope_tables(PAST_LEN + jnp.arange(dseq, dtype=jnp.int32))

    n_prompt = bsz * seq
    n_tok = n_prompt + dbsz * dseq
    zeros_kv = jnp.zeros((bsz, WINDOW, KV_DIM), _F32)
    x1_p, meta_p, gate_p, cnt_p, pk, pv, pc, ph = _mixer(
        x_prompt, rope_p, zeros_kv, zeros_kv, jnp.zeros((bsz, CONV_W - 1, D_RNN), _F32),
        jnp.zeros((bsz, 1, D_RNN), _F32), sinks, prm, has_past=False, nb=1, lt=min(MIXER_ROWS, seq), lq=CHUNK)
    x1_s, meta_s, gate_s, cnt_s, sk, sv, sc, sh = _mixer(
        x_sample, rope_s, cache_k[l].reshape(dbsz, WINDOW, KV_DIM), cache_v[l].reshape(dbsz, WINDOW, KV_DIM),
        state_conv[l], state_h[l][:, None, :], sinks, prm, has_past=True, nb=min(PAST_ROWS // dseq, dbsz),
        lt=dseq, lq=dseq)

    tm = EXPERT_ROWS
    n_tiles = (n_tok * TOP_K + N_EXPERTS * (tm - 1)) // tm + 1
    pos, gates, tile_expert, n_used, zero_tiles = _route(meta_p, meta_s, gate_p, gate_s, cnt_p, cnt_s, tm, n_tiles)
    xs = _dispatch(x1_p, x1_s, pos, zero_tiles, n_used, tt=TOKEN_ROWS, tm=tm, n_tiles=n_tiles)
    ysort = _experts(xs, tile_expert, n_used, prm, tm=tm)
    yp, ys = _combine(x1_p, x1_s, ysort, pos, gates, prm, tt=TOKEN_ROWS)
    yp = yp.reshape(bsz, seq, D_MODEL)
    ys = ys.reshape(dbsz, dseq, D_MODEL)

    kv5 = lambda a: a.reshape(1, a.shape[0], WINDOW, N_KV_HEADS, HEAD_DIM)
    return (yp, ys, kv5(pk), kv5(pv), pc[None], ph.reshape(1, bsz, D_RNN),
            kv5(sk), kv5(sv), sc[None], sh.reshape(1, dbsz, D_RNN))
```

```python
import functools

import jax
import jax.numpy as jnp
import numpy as np
from jax import lax
from jax.experimental import pallas as pl
from jax.experimental.pallas import tpu as pltpu

D_MODEL = 1024
PAST_LEN = 1024
CHUNK = 64
N_HEADS = 16
N_KV_HEADS = 4
HEAD_DIM = 64
GROUP = N_HEADS // N_KV_HEADS
ROT_DIM = HEAD_DIM // 4
ROPE_THETA = 500000.0
WINDOW = 128
ATTN_SCALE = HEAD_DIM ** -0.5
NEG_INF = -1e30
D_RNN = 1280
RNN_HEADS = 16
RNN_HEAD_DIM = D_RNN // RNN_HEADS
CONV_W = 4
LRU_C = 8.0
N_EXPERTS = 32
TOP_K = 4
D_FF = 1024
SWIGLU_ALPHA = 1.702
SWIGLU_LIMIT = 7.0
LN_EPS = 1e-5
DEPTH = 1
DN_ALPHA = (2.0 * DEPTH) ** 0.25

Q_DIM = N_HEADS * HEAD_DIM
KV_DIM = N_KV_HEADS * HEAD_DIM
OFF_K = Q_DIM
OFF_V = OFF_K + KV_DIM
OFF_XR = OFF_V + KV_DIM
OFF_YR = OFF_XR + D_RNN
OFF_GA = OFF_YR + D_RNN
OFF_GR = OFF_GA + D_MODEL
IN_DIM = OFF_GR + D_MODEL

LANES = 128
SLAB = D_MODEL // LANES
MIXER_ROWS = 256
PAST_ROWS = 256
EXPERT_ROWS = 512
TOKEN_ROWS = 512
GELU_C0 = float(np.sqrt(2.0 / np.pi))
GELU_C1 = 0.044715 * GELU_C0
ATT_GROUP = 4
SCAN_S = 4
GATE_COLS = 256
GATE_K = 512
GATE_STARTS = tuple(min(max((j * GATE_COLS // RNN_HEAD_DIM) * RNN_HEAD_DIM // LANES * LANES, 0), D_RNN - GATE_K)
                    for j in range(D_RNN // GATE_COLS))
CONV_HDR = 8
VMEM_LIMIT_BYTES = 56 * 1024 * 1024

_BF16 = jnp.bfloat16
_F32 = jnp.float32


def _dot(a, b):
    return jnp.dot(a, b, preferred_element_type=_F32)


def _dot_nt(a, b):
    return lax.dot_general(a, b, (((1,), (1,)), ((), ())), preferred_element_type=_F32)


def _half_logistic(h):
    return 0.5 * jnp.tanh(h) + 0.5


def _layer_norm(x, g, b):
    mu = jnp.mean(x, axis=-1, keepdims=True)
    xc = x - mu
    var = jnp.mean(xc * xc, axis=-1, keepdims=True)
    return xc * lax.rsqrt(var + LN_EPS) * g + b


def _rope(x, cos_t, sin_up, sin_dn):
    return x * cos_t + pltpu.roll(x, LANES - ROT_DIM // 2, 1) * sin_up + pltpu.roll(x, ROT_DIM // 2, 1) * sin_dn


def _lo_hi(slab, h):
    blk = slab[:, (h // 2) * LANES:(h // 2 + 1) * LANES]
    lane = lax.broadcasted_iota(jnp.int32, blk.shape, 1)
    if h % 2 == 0:
        lo = jnp.where(lane < HEAD_DIM, blk, 0.0)
        hi = pltpu.roll(lo, HEAD_DIM, 1)
    else:
        hi = jnp.where(lane >= HEAD_DIM, blk, 0.0)
        lo = pltpu.roll(hi, HEAD_DIM, 1)
    return lo.astype(_BF16), hi.astype(_BF16)


def _mixer_kernel(has_past, nb, lt, lq,
                  x_ref, rope_ref, w_in_ref, k0_ref, v0_ref, c0_ref, h0_ref, sinkrows_ref,
                  convw_ref, convb_ref, wg_ref, bg_ref, lam_ref, wao_ref, wro_ref, wout_ref,
                  g1_ref, b1_ref, rwt_ref, rb_ref,
                  x1s_ref, meta_ref, gate_ref, cnt_ref, kn_ref, vn_ref, cn_ref, hn_ref,
                  kw_ref, vw_ref, xp_ref, xc_ref, a_ref, b_ref, hin_ref, hc_ref, oat_ref, run_ref):
    s_idx = pl.program_id(1)
    n_s = pl.num_programs(1)
    rows = nb * lt
    n_chunks = lt // lq

    @pl.when(jnp.logical_and(pl.program_id(0) == 0, s_idx == 0))
    def _init_counts():
        run_ref[...] = jnp.zeros_like(run_ref)

    @pl.when(s_idx == 0)
    def _init():
        for bi in range(nb):
            for h in range(N_KV_HEADS):
                klo, khi = _lo_hi(k0_ref[bi], h)
                vlo, vhi = _lo_hi(v0_ref[bi], h)
                kw_ref[bi, h, 0, 0:WINDOW, :] = klo
                kw_ref[bi, h, 1, 0:WINDOW, :] = khi
                vw_ref[bi, h, 0, 0:WINDOW, :] = vlo
                vw_ref[bi, h, 1, 0:WINDOW, :] = vhi
            for c in range(D_RNN // LANES):
                xp_ref[bi, c, CONV_HDR - (CONV_W - 1):CONV_HDR, :] = c0_ref[bi, :, c * LANES:(c + 1) * LANES]
            hc_ref[bi] = h0_ref[bi]

    x = x_ref[...].reshape(rows, D_MODEL)
    xb = x.astype(_BF16)

    cos_t = jnp.concatenate([rope_ref[0]] * nb, axis=0) if nb > 1 else rope_ref[0]
    sin_up = jnp.concatenate([rope_ref[1]] * nb, axis=0) if nb > 1 else rope_ref[1]
    sin_dn = jnp.concatenate([rope_ref[2]] * nb, axis=0) if nb > 1 else rope_ref[2]

    zq = _dot(xb, w_in_ref[:, 0:Q_DIM])
    q_blocks = []
    for c in range(Q_DIM // LANES):
        qr = _rope(zq[:, c * LANES:(c + 1) * LANES], cos_t, sin_up, sin_dn)
        q_blocks.append((qr * ATTN_SCALE).astype(_BF16))
    zk = _dot(xb, w_in_ref[:, OFF_K:OFF_V])
    k_rot = jnp.concatenate(
        [_rope(zk[:, c * LANES:(c + 1) * LANES], cos_t, sin_up, sin_dn) for c in range(KV_DIM // LANES)], axis=1)
    v_new = _dot(xb, w_in_ref[:, OFF_V:OFF_XR])

    for bi in range(nb):
        r0 = bi * lt
        for h in range(N_KV_HEADS):
            klo, khi = _lo_hi(k_rot[r0:r0 + lt], h)
            vlo, vhi = _lo_hi(v_new[r0:r0 + lt], h)
            kw_ref[bi, h, 0, WINDOW:WINDOW + lt, :] = klo
            kw_ref[bi, h, 1, WINDOW:WINDOW + lt, :] = khi
            vw_ref[bi, h, 0, WINDOW:WINDOW + lt, :] = vlo
            vw_ref[bi, h, 1, WINDOW:WINDOW + lt, :] = vhi

    for bi in range(nb):
        r0 = bi * lt
        if lt >= WINDOW:
            kn_ref[bi] = k_rot[r0 + lt - WINDOW:r0 + lt]
            vn_ref[bi] = v_new[r0 + lt - WINDOW:r0 + lt]
        else:
            kn_ref[bi, 0:WINDOW - lt, :] = k0_ref[bi, lt:WINDOW, :]
            kn_ref[bi, WINDOW - lt:WINDOW, :] = k_rot[r0:r0 + lt]
            vn_ref[bi, 0:WINDOW - lt, :] = v0_ref[bi, lt:WINDOW, :]
            vn_ref[bi, WINDOW - lt:WINDOW, :] = v_new[r0:r0 + lt]

    w2 = 2 * lq
    wk = 2 * WINDOW + w2
    r_all = N_KV_HEADS * 2 * lq
    is_a2 = lax.broadcasted_iota(jnp.int32, (r_all, w2), 1) < lq
    lane_lo = lax.broadcasted_iota(jnp.int32, (r_all, LANES), 1) < HEAD_DIM
    key_row = lax.broadcasted_iota(jnp.int32, (wk, LANES), 0)
    key_is_a = jnp.logical_or(key_row < WINDOW, jnp.logical_and(key_row >= 2 * WINDOW, key_row < 2 * WINDOW + lq))
    key_lane_lo = lax.broadcasted_iota(jnp.int32, (wk, LANES), 1) < HEAD_DIM
    head_ones = jnp.where(key_is_a == key_lane_lo, 1.0, 0.0).astype(_BF16)
    sink_rows, sink_a_rows, sink_b_rows = sinkrows_ref[0], sinkrows_ref[1], sinkrows_ref[2]
    first = s_idx == 0
    units = [(bi, j) for bi in range(nb) for j in range(n_chunks)]
    for g0 in range(0, len(units), ATT_GROUP):
        group = units[g0:g0 + ATT_GROUP]
        scores = []
        vmats = []
        for bi, j in group:
            p0 = j * lq
            o0 = WINDOW + j * lq
            q0 = bi * lt + j * lq
            for h in range(N_KV_HEADS):
                kmat = jnp.concatenate([kw_ref[bi, h, 0, p0:p0 + WINDOW, :], kw_ref[bi, h, 1, p0:p0 + WINDOW, :],
                                        kw_ref[bi, h, 0, o0:o0 + lq, :], kw_ref[bi, h, 1, o0:o0 + lq, :]], axis=0)
                vmat = jnp.concatenate([vw_ref[bi, h, 0, p0:p0 + WINDOW, :], vw_ref[bi, h, 1, p0:p0 + WINDOW, :],
                                        vw_ref[bi, h, 0, o0:o0 + lq, :], vw_ref[bi, h, 1, o0:o0 + lq, :]], axis=0)
                vmats.append(jnp.concatenate([vmat, head_ones], axis=1))
                qs = jnp.concatenate([q_blocks[2 * h][q0:q0 + lq], q_blocks[2 * h + 1][q0:q0 + lq]], axis=0)
                sc_h = _dot_nt(qs, kmat)
                if not has_past and j * lq < WINDOW:
                    col = lax.broadcasted_iota(jnp.int32, sc_h.shape, 1)
                    n_bad = WINDOW - j * lq
                    bad = jnp.logical_or(col < n_bad, jnp.logical_and(col >= WINDOW, col < WINDOW + n_bad))
                    sc_h = jnp.where(jnp.logical_and(first, bad), NEG_INF, sc_h)
                scores.append(sc_h)
        n_u = len(group)
        sc = jnp.concatenate(scores, axis=0)
        if g0 == 0:
            ga = _dot(xb, w_in_ref[:, OFF_GA:OFF_GR])
            gr = _dot(xb, w_in_ref[:, OFF_GR:IN_DIM])
        tile_rows = lambda a: jnp.concatenate([a] * n_u, axis=0) if n_u > 1 else a
        is_a2g, lane_log, sinks_g = tile_rows(is_a2), tile_rows(lane_lo), tile_rows(sink_rows)
        sinks_ag, sinks_bg = tile_rows(sink_a_rows), tile_rows(sink_b_rows)
        c0 = sc[:, 0:LANES]
        c1 = sc[:, LANES:2 * LANES]
        c2 = sc[:, 2 * LANES:wk]
        if w2 == LANES:
            ma = jnp.max(jnp.maximum(c0, jnp.where(is_a2g, c2, NEG_INF)).astype(_BF16), axis=1, keepdims=True)
            mb = jnp.max(jnp.maximum(c1, jnp.where(is_a2g, NEG_INF, c2)).astype(_BF16), axis=1, keepdims=True)
        else:
            ma = jnp.maximum(jnp.max(c0.astype(_BF16), axis=1, keepdims=True),
                             jnp.max(jnp.where(is_a2g, c2, NEG_INF).astype(_BF16), axis=1, keepdims=True))
            mb = jnp.maximum(jnp.max(c1.astype(_BF16), axis=1, keepdims=True),
                             jnp.max(jnp.where(is_a2g, NEG_INF, c2).astype(_BF16), axis=1, keepdims=True))
        ma = jnp.maximum(jnp.broadcast_to(ma.astype(_F32), sinks_ag.shape), sinks_ag)
        mb = jnp.maximum(jnp.broadcast_to(mb.astype(_F32), sinks_bg.shape), sinks_bg)
        probs = jnp.concatenate([jnp.exp(c0 - ma), jnp.exp(c1 - mb),
                                 jnp.exp(c2 - jnp.where(is_a2g, ma[:, 0:w2], mb[:, 0:w2]))], axis=1).astype(_BF16)
        sink_term = jnp.exp(sinks_g - jnp.where(lane_log, ma, mb))
        for u, (bi, j) in enumerate(group):
            q0 = bi * lt + j * lq
            for h in range(N_KV_HEADS):
                r0 = (u * N_KV_HEADS + h) * 2 * lq
                od = _dot(probs[r0:r0 + 2 * lq], vmats[u * N_KV_HEADS + h])
                out = od[:, 0:LANES] / (od[:, LANES:2 * LANES] + sink_term[r0:r0 + 2 * lq])
                for p in range(2):
                    oat_ref[q0:q0 + lq, (2 * h + p) * LANES:(2 * h + p + 1) * LANES] = (
                        out[p * lq:(p + 1) * lq].astype(_BF16))

    if lt >= WINDOW:
        @pl.when(s_idx + 1 < n_s)
        def _carry_kv():
            for bi in range(nb):
                for h in range(N_KV_HEADS):
                    for v in range(2):
                        kw_ref[bi, h, v, 0:WINDOW, :] = kw_ref[bi, h, v, lt:lt + WINDOW, :]
                        vw_ref[bi, h, v, 0:WINDOW, :] = vw_ref[bi, h, v, lt:lt + WINDOW, :]

    xr = _dot(xb, w_in_ref[:, OFF_XR:OFF_YR])
    n_lb = D_RNN // LANES
    n_g = lt // SCAN_S
    conv_w = convw_ref[...]
    conv_b = convb_ref[...]
    for bi in range(nb):
        r0 = bi * lt
        tail = xr[r0 + lt - (CONV_W - 1):r0 + lt]
        cn_ref[bi] = tail
        for c in range(n_lb):
            ls = slice(c * LANES, (c + 1) * LANES)
            xp_ref[bi, c, CONV_HDR:CONV_HDR + lt, :] = xr[r0:r0 + lt, ls]
            frames = [xp_ref[bi, c, pl.ds(CONV_HDR + m, n_g, stride=SCAN_S), :]
                      for m in range(-(CONV_W - 1), SCAN_S)]
            for k in range(SCAN_S):
                acc = frames[k] * conv_w[0:1, ls]
                for t in range(1, CONV_W):
                    acc = acc + frames[k + t] * conv_w[t:t + 1, ls]
                xc_ref[c, pl.ds(r0 + k, n_g, stride=SCAN_S), :] = acc + conv_b[:, ls]
            xp_ref[bi, c, CONV_HDR - (CONV_W - 1):CONV_HDR, :] = tail[:, ls]

    yr = _dot(xb, w_in_ref[:, OFF_YR:OFF_GA])
    xc = jnp.concatenate([xc_ref[c] for c in range(n_lb)], axis=1)
    xcb = xc.astype(_BF16)
    lam = lam_ref[...]
    softplus_neg = jnp.maximum(-lam, 0.0) + jnp.log(1.0 + jnp.exp(-jnp.abs(lam)))
    for j in range(D_RNN // GATE_COLS):
        cs = slice(j * GATE_COLS, (j + 1) * GATE_COLS)
        g = _dot(xcb[:, GATE_STARTS[j]:GATE_STARTS[j] + GATE_K], wg_ref[j]) + bg_ref[j]
        i_gate = _half_logistic(g[:, GATE_COLS:2 * GATE_COLS])
        m_sp = (-0.5 * LRU_C) * softplus_neg[:, cs]
        log_a = jnp.tanh(g[:, 0:GATE_COLS]) * m_sp + m_sp
        a = jnp.exp(log_a)
        z = -jnp.tanh(log_a) * (1.0 + a * a)
        bv = jnp.where(z > 0.0, z * lax.rsqrt(z), 0.0) * i_gate * xc[:, cs]
        for cc in range(GATE_COLS // LANES):
            c = j * (GATE_COLS // LANES) + cc
            a_ref[c] = a[:, cc * LANES:(cc + 1) * LANES]
            b_ref[c] = bv[:, cc * LANES:(cc + 1) * LANES]

    for bi in range(nb):
        r0 = bi * lt
        a_tot, h_tot = [], []
        for c in range(n_lb):
            at = a_ref[c, pl.ds(r0, n_g, stride=SCAN_S), :]
            ht = b_ref[c, pl.ds(r0, n_g, stride=SCAN_S), :]
            for k in range(1, SCAN_S):
                ak = a_ref[c, pl.ds(r0 + k, n_g, stride=SCAN_S), :]
                ht = ak * ht + b_ref[c, pl.ds(r0 + k, n_g, stride=SCAN_S), :]
                at = ak * at
            a_tot.append(at)
            h_tot.append(ht)
        carry = [hc_ref[bi][:, c * LANES:(c + 1) * LANES] for c in range(n_lb)]
        for g in range(n_g):
            for c in range(n_lb):
                hin_ref[c, g:g + 1, :] = carry[c]
                carry[c] = a_tot[c][g:g + 1, :] * carry[c] + h_tot[c][g:g + 1, :]
        h_last = jnp.concatenate(carry, axis=1)
        hc_ref[bi] = h_last
        hn_ref[bi] = h_last
        for c in range(n_lb):
            hcur = hin_ref[c]
            for k in range(SCAN_S):
                hcur = a_ref[c, pl.ds(r0 + k, n_g, stride=SCAN_S), :] * hcur + b_ref[c, pl.ds(r0 + k, n_g, stride=SCAN_S), :]
                b_ref[c, pl.ds(r0 + k, n_g, stride=SCAN_S), :] = hcur

    gelu_t = jnp.tanh(yr * (GELU_C0 + GELU_C1 * (yr * yr)))
    half_h = jnp.concatenate([b_ref[c] for c in range(n_lb)], axis=1) * (0.5 * yr)
    rnn = half_h * gelu_t + half_h

    merged2 = ((jnp.tanh(ga) + 1.0) * _dot(oat_ref[...], wao_ref[...])
               + (jnp.tanh(gr) + 1.0) * _dot(rnn.astype(_BF16), wro_ref[...]))
    mix = _dot(merged2.astype(_BF16), wout_ref[...])
    x1 = _layer_norm(DN_ALPHA * x + mix, g1_ref[...], b1_ref[...])
    for c in range(SLAB):
        x1s_ref[pl.ds(c, rows, stride=SLAB), :] = x1[:, c * LANES:(c + 1) * LANES]

    logits = _dot_nt(rwt_ref[...], x1.astype(_BF16)) + rb_ref[...]
    e_iota = lax.broadcasted_iota(jnp.int32, logits.shape, 0)
    work = logits
    top_vals = []
    top_idx = []
    top_sel = []
    for _ in range(TOP_K):
        m = jnp.max(work, axis=0, keepdims=True)
        idx = jnp.min(jnp.where(work == m, e_iota, N_EXPERTS), axis=0, keepdims=True)
        sel = e_iota == idx
        top_vals.append(m)
        top_idx.append(idx)
        top_sel.append(sel)
        work = jnp.where(sel, -jnp.inf, work)
    exps = [jnp.exp(v - top_vals[0]) for v in top_vals]
    denom = exps[0]
    for e in exps[1:]:
        denom = denom + e

    onehot = jnp.zeros_like(logits)
    for sel in top_sel:
        onehot = onehot + jnp.where(sel, 1.0, 0.0)
    r_i = lax.broadcasted_iota(jnp.int32, (rows, rows), 0)
    c_i = lax.broadcasted_iota(jnp.int32, (rows, rows), 1)
    earlier = jnp.where(r_i < c_i, 1.0, 0.0).astype(_BF16)
    before = _dot(onehot.astype(_BF16), earlier) + run_ref[...]
    run_ref[...] = run_ref[...] + jnp.sum(onehot, axis=1, keepdims=True)
    cnt_ref[...] = run_ref[...]

    sub_m = lax.broadcasted_iota(jnp.int32, (2 * TOP_K, rows), 0)
    meta_i = jnp.zeros((2 * TOP_K, rows), jnp.int32)
    meta_g = jnp.zeros((2 * TOP_K, rows), _F32)
    for k in range(TOP_K):
        rank_k = jnp.sum(jnp.where(top_sel[k], before, 0.0), axis=0, keepdims=True).astype(jnp.int32)
        meta_i = jnp.where(sub_m == k, top_idx[k], meta_i)
        meta_i = jnp.where(sub_m == TOP_K + k, rank_k, meta_i)
        meta_g = jnp.where(sub_m == k, exps[k] / denom, meta_g)
    meta_ref[...] = meta_i
    gate_ref[...] = meta_g


def _const_spec(shape):
    nd = len(shape)
    return pl.BlockSpec(shape, lambda b, s, _nd=nd: (0,) * _nd, pipeline_mode=pl.Buffered(1))


def _mixer(x, rope, k0, v0, c0, h0, sinks, prm, *, has_past, nb, lt, lq):
    bsz, seq, _ = x.shape
    assert bsz % nb == 0 and seq % lt == 0 and lt % lq == 0 and lq % 32 == 0
    assert lt >= WINDOW or seq == lt
    n_s = seq // lt
    rows = nb * lt
    n_tok = bsz * seq
    assert lt % SCAN_S == 0
    kernel = functools.partial(_mixer_kernel, has_past, nb, lt, lq)
    r_all = N_KV_HEADS * 2 * lq
    sink_rows = jnp.broadcast_to(sinks.reshape(N_KV_HEADS * 2, 1, 2, 1), (N_KV_HEADS * 2, lq, 2, HEAD_DIM))
    sink_rows = sink_rows.reshape(r_all, LANES)
    sink_ab = jnp.broadcast_to(sinks.reshape(N_KV_HEADS * 2, 1, 2, 1), (N_KV_HEADS * 2, lq, 2, LANES))
    sink_rows = jnp.stack([sink_rows, sink_ab[:, :, 0].reshape(r_all, LANES), sink_ab[:, :, 1].reshape(r_all, LANES)])
    n_gate = D_RNN // GATE_COLS
    batch_spec = lambda shape: pl.BlockSpec((nb,) + shape, lambda b, s: (b,) + (0,) * len(shape))
    in_specs = [
        pl.BlockSpec((nb, lt, D_MODEL), lambda b, s: (b, s, 0)),
        pl.BlockSpec((3, lt, LANES), lambda b, s: (0, s, 0)),
        _const_spec((D_MODEL, IN_DIM)),
        batch_spec((WINDOW, KV_DIM)), batch_spec((WINDOW, KV_DIM)),
        batch_spec((CONV_W - 1, D_RNN)), batch_spec((1, D_RNN)),
        _const_spec((3, r_all, LANES)),
        _const_spec((CONV_W, D_RNN)), _const_spec((1, D_RNN)),
        _const_spec((n_gate, GATE_K, 2 * GATE_COLS)), _const_spec((n_gate, 1, 2 * GATE_COLS)), _const_spec((1, D_RNN)),
        _const_spec((Q_DIM, D_MODEL)), _const_spec((D_RNN, D_MODEL)), _const_spec((D_MODEL, D_MODEL)),
        _const_spec((1, D_MODEL)), _const_spec((1, D_MODEL)),
        _const_spec((N_EXPERTS, D_MODEL)), _const_spec((N_EXPERTS, 1)),
    ]
    tok_map = lambda b, s: (b * n_s + s, 0)
    tok_map_t = lambda b, s: (0, b * n_s + s)
    out_shape = (
        jax.ShapeDtypeStruct((n_tok * SLAB, LANES), _F32),
        jax.ShapeDtypeStruct((2 * TOP_K, n_tok), jnp.int32),
        jax.ShapeDtypeStruct((2 * TOP_K, n_tok), _F32),
        jax.ShapeDtypeStruct((N_EXPERTS, 1), _F32),
        jax.ShapeDtypeStruct((bsz, WINDOW, KV_DIM), _F32),
        jax.ShapeDtypeStruct((bsz, WINDOW, KV_DIM), _F32),
        jax.ShapeDtypeStruct((bsz, CONV_W - 1, D_RNN), _F32),
        jax.ShapeDtypeStruct((bsz, 1, D_RNN), _F32),
    )
    out_specs = (
        pl.BlockSpec((rows * SLAB, LANES), tok_map),
        pl.BlockSpec((2 * TOP_K, rows), tok_map_t),
        pl.BlockSpec((2 * TOP_K, rows), tok_map_t),
        pl.BlockSpec((N_EXPERTS, 1), lambda b, s: (0, 0)),
        batch_spec((WINDOW, KV_DIM)), batch_spec((WINDOW, KV_DIM)),
        batch_spec((CONV_W - 1, D_RNN)), batch_spec((1, D_RNN)),
    )
    scratch = [
        pltpu.VMEM((nb, N_KV_HEADS, 2, WINDOW + lt, LANES), _BF16),
        pltpu.VMEM((nb, N_KV_HEADS, 2, WINDOW + lt, LANES), _BF16),
        pltpu.VMEM((nb, D_RNN // LANES, CONV_HDR + lt, LANES), _F32),
        pltpu.VMEM((D_RNN // LANES, rows, LANES), _F32),
        pltpu.VMEM((D_RNN // LANES, rows, LANES), _F32),
        pltpu.VMEM((D_RNN // LANES, rows, LANES), _F32),
        pltpu.VMEM((D_RNN // LANES, lt // SCAN_S, LANES), _F32),
        pltpu.VMEM((nb, 1, D_RNN), _F32),
        pltpu.VMEM((rows, Q_DIM), _BF16),
        pltpu.VMEM((N_EXPERTS, 1), _F32),
    ]
    args = [x, rope, prm["w_in"], k0, v0, c0, h0, sink_rows, prm["conv_w"], prm["conv_b"], prm["wg"], prm["bg"],
            prm["lam"], prm["wao"], prm["wro"], prm["wout"], prm["g1"], prm["b1"], prm["rwt"], prm["rb"]]
    return pl.pallas_call(
        kernel,
        grid=(bsz // nb, n_s),
        in_specs=in_specs,
        out_specs=out_specs,
        out_shape=out_shape,
        scratch_shapes=scratch,
        compiler_params=pltpu.CompilerParams(
            dimension_semantics=("arbitrary", "arbitrary"), vmem_limit_bytes=VMEM_LIMIT_BYTES),
        name="mixer_past" if has_past else "mixer_prompt",
    )(*args)


def _slab_rows(ref, first_tok, n):
    return jnp.concatenate([ref[pl.ds(first_tok * SLAB + c, n, stride=SLAB), :] for c in range(SLAB)], axis=1)


def _dispatch_kernel(tt, tm, n_first, n_tiles, zt_ref, nu_ref, pos_ref, xa_ref, xb_ref, xs_hbm, zero_ref, sem):
    i = pl.program_id(0)

    @pl.when(i == 0)
    def _zero_padding():
        zero_ref[...] = jnp.zeros_like(zero_ref)

        def clear(tile):
            dst = xs_hbm.at[pl.ds(pl.multiple_of(tile * (tm * SLAB), SLAB), tm * SLAB)]
            cp = pltpu.make_async_copy(zero_ref, dst, sem)
            cp.start()
            cp.wait()

        for e in range(N_EXPERTS):
            pl.when(zt_ref[e] >= 0)(functools.partial(clear, zt_ref[e]))

        def clear_unused(tile, carry):
            clear(tile)
            return carry

        lax.fori_loop(nu_ref[0], n_tiles, clear_unused, 0)

    def scatter(src_ref):
        def issue(r, carry):
            src = src_ref.at[pl.ds(pl.multiple_of(r * SLAB, SLAB), SLAB)]
            for k in range(TOP_K):
                dst = xs_hbm.at[pl.ds(pl.multiple_of(pos_ref[0, 0, r * TOP_K + k] * SLAB, SLAB), SLAB)]
                pltpu.make_async_copy(src, dst, sem).start(priority=k % 2)
            return carry

        lax.fori_loop(0, tt, issue, 0)

    pl.when(i < n_first)(lambda: scatter(xa_ref))
    pl.when(i >= n_first)(lambda: scatter(xb_ref))
    n_rows = tt * TOP_K * SLAB
    pltpu.make_async_copy(xs_hbm.at[pl.ds(0, n_rows)], xs_hbm.at[pl.ds(0, n_rows)], sem).wait()


def _dispatch(x1s_a, x1s_b, pos, zero_tiles, n_used, *, tt, tm, n_tiles):
    n_a = x1s_a.shape[0] // SLAB
    n_b = x1s_b.shape[0] // SLAB
    assert n_a % tt == 0 and n_b % tt == 0 and n_tiles * tm >= tt * TOP_K
    n_first = n_a // tt
    n_steps = (n_a + n_b) // tt
    return pl.pallas_call(
        functools.partial(_dispatch_kernel, tt, tm, n_first, n_tiles),
        grid_spec=pltpu.PrefetchScalarGridSpec(
            num_scalar_prefetch=2,
            grid=(n_steps,),
            in_specs=[
                pl.BlockSpec((1, 1, tt * TOP_K), lambda i, zt, nu: (i, 0, 0), memory_space=pltpu.SMEM),
                pl.BlockSpec((tt * SLAB, LANES), lambda i, zt, nu: (jnp.minimum(i, n_first - 1), 0)),
                pl.BlockSpec((tt * SLAB, LANES), lambda i, zt, nu: (jnp.maximum(i - n_first, 0), 0)),
            ],
            out_specs=pl.BlockSpec(memory_space=pl.ANY),
            scratch_shapes=[pltpu.VMEM((tm * SLAB, LANES), _F32), pltpu.SemaphoreType.DMA],
        ),
        out_shape=jax.ShapeDtypeStruct((n_tiles * tm * SLAB, LANES), _F32),
        compiler_params=pltpu.CompilerParams(dimension_semantics=("arbitrary",)),
        name="moe_dispatch",
    )(zero_tiles, n_used, pos.reshape(n_steps, 1, tt * TOP_K), x1s_a, x1s_b)


def _expert_kernel(tm, te_ref, nu_ref, xs_ref, wgu_ref, bgu_ref, wd_ref, bd_ref, ys_ref, wgu_b, wd_b):
    i = pl.program_id(0)

    @pl.when(i < nu_ref[0])
    def _tile():
        prev_e = te_ref[jnp.maximum(i - 1, 0)]

        @pl.when(jnp.logical_or(i == 0, te_ref[i] != prev_e))
        def _new_expert():
            wgu_b[...] = wgu_ref[0].astype(_BF16)
            wd_b[...] = wd_ref[0].astype(_BF16)

        x = _slab_rows(xs_ref, 0, tm).astype(_BF16)
        hgu = _dot(x, wgu_b[...]) + bgu_ref[0]
        glu = jnp.minimum(hgu[:, 0:D_FF], SWIGLU_LIMIT)
        lin = jnp.clip(hgu[:, D_FF:2 * D_FF], -SWIGLU_LIMIT, SWIGLU_LIMIT)
        hh = glu * _half_logistic((0.5 * SWIGLU_ALPHA) * glu) * (lin + 1.0)
        y = _dot(hh.astype(_BF16), wd_b[...]) + bd_ref[0]
        for c in range(SLAB):
            ys_ref[pl.ds(c, tm, stride=SLAB), :] = y[:, c * LANES:(c + 1) * LANES]

    @pl.when(i >= nu_ref[0])
    def _unused_tile():
        ys_ref[...] = jnp.zeros_like(ys_ref)


def _experts(xs, tile_expert, n_used, prm, *, tm):
    n_tiles = xs.shape[0] // (tm * SLAB)
    row_map = lambda i, te, nu: (jnp.minimum(i, nu[0] - 1), 0)
    out_map = lambda i, te, nu: (i, 0)
    exp_map = lambda i, te, nu: (te[i], 0, 0)
    return pl.pallas_call(
        functools.partial(_expert_kernel, tm),
        grid_spec=pltpu.PrefetchScalarGridSpec(
            num_scalar_prefetch=2,
            grid=(n_tiles,),
            in_specs=[
                pl.BlockSpec((tm * SLAB, LANES), row_map),
                pl.BlockSpec((1, D_MODEL, 2 * D_FF), exp_map),
                pl.BlockSpec((1, 1, 2 * D_FF), exp_map),
                pl.BlockSpec((1, D_FF, D_MODEL), exp_map),
                pl.BlockSpec((1, 1, D_MODEL), exp_map),
            ],
            out_specs=pl.BlockSpec((tm * SLAB, LANES), out_map),
            scratch_shapes=[pltpu.VMEM((D_MODEL, 2 * D_FF), _BF16), pltpu.VMEM((D_FF, D_MODEL), _BF16)],
        ),
        out_shape=jax.ShapeDtypeStruct(xs.shape, _F32),
        compiler_params=pltpu.CompilerParams(
            dimension_semantics=("arbitrary",), vmem_limit_bytes=VMEM_LIMIT_BYTES),
        name="moe_experts",
    )(tile_expert, n_used, xs, prm["wgu"], prm["bgu"], prm["wd"], prm["bd"])


def _combine_kernel(tt, n_first, n_steps, pos_ref, pos_next_ref, gate_ref, xa_ref, xb_ref, ys_hbm, g2_ref, b2_ref,
                    ya_ref, yb_ref, ybuf, sems):
    i = pl.program_id(0)
    n_rows = tt * TOP_K * SLAB

    def gather(p_ref, slot):
        def issue(r, carry):
            for k in range(TOP_K):
                src = ys_hbm.at[pl.ds(pl.multiple_of(p_ref[0, 0, r * TOP_K + k] * SLAB, SLAB), SLAB)]
                dst = ybuf.at[slot, pl.ds(pl.multiple_of((k * tt + r) * SLAB, SLAB), SLAB)]
                pltpu.make_async_copy(src, dst, sems.at[slot]).start(priority=k % 2)
            return carry

        lax.fori_loop(0, tt, issue, 0)

    pl.when(i == 0)(lambda: gather(pos_ref, 0))

    def tile(slot):
        pl.when(i + 1 < n_steps)(lambda: gather(pos_next_ref, 1 - slot))
        pltpu.make_async_copy(ys_hbm.at[pl.ds(0, n_rows)], ybuf.at[slot], sems.at[slot]).wait()
        rows_ref = ybuf.at[slot]
        gates = gate_ref[...]
        acc = gates[:, 0:1] * _slab_rows(rows_ref, 0, tt)
        for k in range(1, TOP_K):
            acc = acc + gates[:, k:k + 1] * _slab_rows(rows_ref, k * tt, tt)

        def finish(x_ref, y_ref):
            x1 = _slab_rows(x_ref, 0, tt)
            y_ref[...] = _layer_norm(DN_ALPHA * x1 + acc, g2_ref[...], b2_ref[...])

        pl.when(i < n_first)(lambda: finish(xa_ref, ya_ref))
        pl.when(i >= n_first)(lambda: finish(xb_ref, yb_ref))

    for slot in range(2):
        pl.when(i % 2 == slot)(functools.partial(tile, slot))


def _combine(x1s_a, x1s_b, ys, pos, gates, prm, *, tt):
    n_a = x1s_a.shape[0] // SLAB
    n_b = x1s_b.shape[0] // SLAB
    assert n_a % tt == 0 and n_b % tt == 0
    n_first = n_a // tt
    n_steps = (n_a + n_b) // tt
    a_map = lambda i: (jnp.minimum(i, n_first - 1), 0)
    b_map = lambda i: (jnp.maximum(i - n_first, 0), 0)
    pos3 = pos.reshape(n_steps, 1, tt * TOP_K)
    return pl.pallas_call(
        functools.partial(_combine_kernel, tt, n_first, n_steps),
        grid=(n_steps,),
        in_specs=[
            pl.BlockSpec((1, 1, tt * TOP_K), lambda i: (i, 0, 0), memory_space=pltpu.SMEM),
            pl.BlockSpec((1, 1, tt * TOP_K), lambda i: (jnp.minimum(i + 1, n_steps - 1), 0, 0),
                         memory_space=pltpu.SMEM),
            pl.BlockSpec((tt, TOP_K), lambda i: (i, 0)),
            pl.BlockSpec((tt * SLAB, LANES), a_map),
            pl.BlockSpec((tt * SLAB, LANES), b_map),
            pl.BlockSpec(memory_space=pl.ANY),
            pl.BlockSpec((1, D_MODEL), lambda i: (0, 0)),
            pl.BlockSpec((1, D_MODEL), lambda i: (0, 0)),
        ],
        out_specs=(pl.BlockSpec((tt, D_MODEL), a_map), pl.BlockSpec((tt, D_MODEL), b_map)),
        out_shape=(jax.ShapeDtypeStruct((n_a, D_MODEL), _F32), jax.ShapeDtypeStruct((n_b, D_MODEL), _F32)),
        scratch_shapes=[pltpu.VMEM((2, TOP_K * tt * SLAB, LANES), _F32), pltpu.SemaphoreType.DMA((2,))],
        compiler_params=pltpu.CompilerParams(
            dimension_semantics=("arbitrary",), vmem_limit_bytes=VMEM_LIMIT_BYTES),
        name="moe_combine",
    )(pos3, pos3, gates, x1s_a, x1s_b, ys, prm["g2"], prm["b2"])


def _route(meta_a, meta_b, gate_a, gate_b, counts_a, counts_b, tm, n_tiles):
    meta = jnp.concatenate([meta_a, meta_b], axis=1).T
    idx = meta[:, 0:TOP_K]
    rank = meta[:, TOP_K:2 * TOP_K]
    gates = jnp.concatenate([gate_a[0:TOP_K], gate_b[0:TOP_K]], axis=1).T
    cp = counts_a.reshape(-1).astype(jnp.int32)
    cs = counts_b.reshape(-1).astype(jnp.int32)
    is_b = (jnp.arange(meta.shape[0]) >= meta_a.shape[1])[:, None]
    is_e = idx[:, :, None] == jnp.arange(N_EXPERTS, dtype=idx.dtype)
    lookup = lambda table: jnp.sum(jnp.where(is_e, table, 0), axis=-1)
    rank = rank + jnp.where(is_b, lookup(cp), 0)
    tiles_per = (cp + cs + tm - 1) // tm
    tile_end = jnp.cumsum(tiles_per)
    tile_start = tile_end - tiles_per
    pos = lookup(tile_start) * tm + rank
    n_used = tile_end[-1:]
    tile_ids = jnp.arange(n_tiles, dtype=jnp.int32)
    tile_expert = jnp.minimum(jnp.sum(tile_ids[:, None] >= tile_end[None, :], axis=1), N_EXPERTS - 1)
    last_e = tile_expert[jnp.maximum(n_used[0] - 1, 0)]
    tile_expert = jnp.where(tile_ids < n_used[0], tile_expert, last_e).astype(jnp.int32)
    zero_tiles = jnp.where(tiles_per > 0, tile_end - 1, -1).astype(jnp.int32)
    return pos.astype(jnp.int32), gates, tile_expert, n_used.astype(jnp.int32), zero_tiles


def _rope_tables(pos):
    half = ROT_DIM // 2
    inv_freq = ROPE_THETA ** (-jnp.arange(half, dtype=_F32) / half)
    ang = pos.astype(_F32)[:, None] * inv_freq[None, :]
    cos, sin = jnp.cos(ang), jnp.sin(ang)
    n = pos.shape[0]
    ones = jnp.ones((n, HEAD_DIM - ROT_DIM), _F32)
    zeros = jnp.zeros((n, HEAD_DIM - ROT_DIM), _F32)
    zh = jnp.zeros((n, half), _F32)
    cos_t = jnp.concatenate([cos, cos, ones], axis=1)
    sin_up = jnp.concatenate([-sin, zh, zeros], axis=1)
    sin_dn = jnp.concatenate([zh, sin, zeros], axis=1)
    tab = jnp.stack([cos_t, sin_up, sin_dn])
    return jnp.concatenate([tab, tab], axis=2)


def _block_diag(w):
    hd, d, _ = w.shape
    eye = jnp.eye(hd, dtype=w.dtype)
    return (eye[:, None, :, None] * w[:, :, None, :]).reshape(hd * d, hd * d)


def _gate_blocks(wa, wx):
    blocks = []
    head_of = np.arange(D_RNN) // RNN_HEAD_DIM
    for j, k0 in enumerate(GATE_STARTS):
        cols = np.arange(j * GATE_COLS, (j + 1) * GATE_COLS)
        reach = np.flatnonzero(np.isin(head_of, head_of[cols]))
        assert reach.min() >= k0 and reach.max() < k0 + GATE_K
        cs = slice(j * GATE_COLS, (j + 1) * GATE_COLS)
        blocks.append(jnp.concatenate([wa[k0:k0 + GATE_K, cs], wx[k0:k0 + GATE_K, cs]], axis=1))
    return jnp.stack(blocks)


def kernel(x_prompt, x_sample, cache_k, cache_v, state_conv, state_h, w_in, attn_sinks, w_attn_out, conv_w, conv_b, gate_a_w, gate_a_b, gate_x_w, gate_x_b, lru_lambda, w_rnn_out, w_out, ln1_g, ln1_b, router_w, router_b, w_gate_up, b_gate_up, w_down, b_down, ln2_g, ln2_b):
    assert w_in.shape[0] == DEPTH == 1
    l = 0
    bsz, seq, _ = x_prompt.shape
    dbsz, dseq, _ = x_sample.shape
    row = lambda a: a.reshape(1, -1)
    prm = {
        "w_in": (w_in[l] * jnp.where(jnp.arange(IN_DIM) >= OFF_GA, 0.5, 1.0)).astype(_BF16),
        "conv_w": conv_w[l], "conv_b": row(conv_b[l]),
        "wg": (0.5 * _gate_blocks(_block_diag(gate_a_w[l]), _block_diag(gate_x_w[l]))).astype(_BF16),
        "bg": 0.5 * jnp.concatenate([gate_a_b[l].reshape(-1, 1, GATE_COLS),
                                     gate_x_b[l].reshape(-1, 1, GATE_COLS)], axis=2),
        "lam": row(lru_lambda[l]),
        "wao": w_attn_out[l].astype(_BF16), "wro": w_rnn_out[l].astype(_BF16), "wout": (0.5 * w_out[l]).astype(_BF16),
        "g1": row(ln1_g[l]), "b1": row(ln1_b[l]),
        "rwt": router_w[l].T.astype(_BF16), "rb": router_b[l].reshape(-1, 1),
        "wgu": w_gate_up[l], "bgu": b_gate_up[l][:, None, :],
        "wd": w_down[l], "bd": b_down[l][:, None, :],
        "g2": row(ln2_g[l]), "b2": row(ln2_b[l]),
    }
    sinks = attn_sinks[l]

    rope_p = _rope_tables(jnp.arange(seq, dtype=jnp.int32))
    rope_s = _rope_tables(PAST_LEN + jnp.arange(dseq, dtype=jnp.int32))

    n_prompt = bsz * seq
    n_tok = n_prompt + dbsz * dseq
    zeros_kv = jnp.zeros((bsz, WINDOW, KV_DIM), _F32)
    x1_p, meta_p, gate_p, cnt_p, pk, pv, pc, ph = _mixer(
        x_prompt, rope_p, zeros_kv, zeros_kv, jnp.zeros((bsz, CONV_W - 1, D_RNN), _F32),
        jnp.zeros((bsz, 1, D_RNN), _F32), sinks, prm, has_past=False, nb=1, lt=min(MIXER_ROWS, seq), lq=CHUNK)
    x1_s, meta_s, gate_s, cnt_s, sk, sv, sc, sh = _mixer(
        x_sample, rope_s, cache_k[l].reshape(dbsz, WINDOW, KV_DIM), cache_v[l].reshape(dbsz, WINDOW, KV_DIM),
        state_conv[l], state_h[l][:, None, :], sinks, prm, has_past=True, nb=min(PAST_ROWS // dseq, dbsz),
        lt=dseq, lq=dseq)

    tm = EXPERT_ROWS
    n_tiles = (n_tok * TOP_K + N_EXPERTS * (tm - 1)) // tm + 1
    pos, gates, tile_expert, n_used, zero_tiles = _route(meta_p, meta_s, gate_p, gate_s, cnt_p, cnt_s, tm, n_tiles)
    xs = _dispatch(x1_p, x1_s, pos, zero_tiles, n_used, tt=TOKEN_ROWS, tm=tm, n_tiles=n_tiles)
    ysort = _experts(xs, tile_expert, n_used, prm, tm=tm)
    yp, ys = _combine(x1_p, x1_s, ysort, pos, gates, prm, tt=TOKEN_ROWS)
    yp = yp.reshape(bsz, seq, D_MODEL)
    ys = ys.reshape(dbsz, dseq, D_MODEL)

    kv5 = lambda a: a.reshape(1, a.shape[0], WINDOW, N_KV_HEADS, HEAD_DIM)
    return (yp, ys, kv5(pk), kv5(pv), pc[None], ph.reshape(1, bsz, D_RNN),
            kv5(sk), kv5(sv), sc[None], sh.reshape(1, dbsz, D_RNN))
```

```python
import functools

import jax
import jax.numpy as jnp
import numpy as np
from jax import lax
from jax.experimental import pallas as pl
from jax.experimental.pallas import tpu as pltpu

D_MODEL = 1024
PAST_LEN = 1024
CHUNK = 64
N_HEADS = 16
N_KV_HEADS = 4
HEAD_DIM = 64
GROUP = N_HEADS // N_KV_HEADS
ROT_DIM = HEAD_DIM // 4
ROPE_THETA = 500000.0
WINDOW = 128
ATTN_SCALE = HEAD_DIM ** -0.5
NEG_INF = -1e30
D_RNN = 1280
RNN_HEADS = 16
RNN_HEAD_DIM = D_RNN // RNN_HEADS
CONV_W = 4
LRU_C = 8.0
N_EXPERTS = 32
TOP_K = 4
D_FF = 1024
SWIGLU_ALPHA = 1.702
SWIGLU_LIMIT = 7.0
LN_EPS = 1e-5
DEPTH = 1
DN_ALPHA = (2.0 * DEPTH) ** 0.25

Q_DIM = N_HEADS * HEAD_DIM
KV_DIM = N_KV_HEADS * HEAD_DIM
OFF_K = Q_DIM
OFF_V = OFF_K + KV_DIM
OFF_XR = OFF_V + KV_DIM
OFF_YR = OFF_XR + D_RNN
OFF_GA = OFF_YR + D_RNN
OFF_GR = OFF_GA + D_MODEL
IN_DIM = OFF_GR + D_MODEL

LANES = 128
SLAB = D_MODEL // LANES
MIXER_ROWS = 256
PAST_ROWS = 256
EXPERT_ROWS = 512
TOKEN_ROWS = 512
GELU_C0 = float(np.sqrt(2.0 / np.pi))
GELU_C1 = 0.044715 * GELU_C0
ATT_GROUP = 4
SCAN_S = 4
GATE_COLS = 256
GATE_K = 512
GATE_STARTS = tuple(min(max((j * GATE_COLS // RNN_HEAD_DIM) * RNN_HEAD_DIM // LANES * LANES, 0), D_RNN - GATE_K)
                    for j in range(D_RNN // GATE_COLS))
CONV_HDR = 8
VMEM_LIMIT_BYTES = 56 * 1024 * 1024

_BF16 = jnp.bfloat16
_F32 = jnp.float32


def _dot(a, b):
    return jnp.dot(a, b, preferred_element_type=_F32)


def _dot_nt(a, b):
    return lax.dot_general(a, b, (((1,), (1,)), ((), ())), preferred_element_type=_F32)


def _half_logistic(h):
    return 0.5 * jnp.tanh(h) + 0.5


def _layer_norm(x, g, b):
    mu = jnp.mean(x, axis=-1, keepdims=True)
    xc = x - mu
    var = jnp.mean(xc * xc, axis=-1, keepdims=True)
    return xc * lax.rsqrt(var + LN_EPS) * g + b


def _rope(x, cos_t, sin_up, sin_dn):
    return x * cos_t + pltpu.roll(x, LANES - ROT_DIM // 2, 1) * sin_up + pltpu.roll(x, ROT_DIM // 2, 1) * sin_dn


def _lo_hi(slab, h):
    blk = slab[:, (h // 2) * LANES:(h // 2 + 1) * LANES]
    lane = lax.broadcasted_iota(jnp.int32, blk.shape, 1)
    if h % 2 == 0:
        lo = jnp.where(lane < HEAD_DIM, blk, 0.0)
        hi = pltpu.roll(lo, HEAD_DIM, 1)
    else:
        hi = jnp.where(lane >= HEAD_DIM, blk, 0.0)
        lo = pltpu.roll(hi, HEAD_DIM, 1)
    return lo.astype(_BF16), hi.astype(_BF16)


def _mixer_kernel(has_past, nb, lt, lq,
                  x_ref, rope_ref, w_in_ref, k0_ref, v0_ref, c0_ref, h0_ref, sinkrows_ref,
                  convw_ref, convb_ref, wg_ref, bg_ref, lam_ref, wao_ref, wro_ref, wout_ref,
                  g1_ref, b1_ref, rwt_ref, rb_ref,
                  x1s_ref, meta_ref, gate_ref, cnt_ref, kn_ref, vn_ref, cn_ref, hn_ref,
                  kw_ref, vw_ref, xp_ref, xc_ref, a_ref, b_ref, hin_ref, hc_ref, oat_ref, run_ref):
    s_idx = pl.program_id(1)
    n_s = pl.num_programs(1)
    rows = nb * lt
    n_chunks = lt // lq

    @pl.when(jnp.logical_and(pl.program_id(0) == 0, s_idx == 0))
    def _init_counts():
        run_ref[...] = jnp.zeros_like(run_ref)

    @pl.when(s_idx == 0)
    def _init():
        for bi in range(nb):
            for h in range(N_KV_HEADS):
                klo, khi = _lo_hi(k0_ref[bi], h)
                vlo, vhi = _lo_hi(v0_ref[bi], h)
                kw_ref[bi, h, 0, 0:WINDOW, :] = klo
                kw_ref[bi, h, 1, 0:WINDOW, :] = khi
                vw_ref[bi, h, 0, 0:WINDOW, :] = vlo
                vw_ref[bi, h, 1, 0:WINDOW, :] = vhi
            for c in range(D_RNN // LANES):
                xp_ref[bi, c, CONV_HDR - (CONV_W - 1):CONV_HDR, :] = c0_ref[bi, :, c * LANES:(c + 1) * LANES]
            hc_ref[bi] = h0_ref[bi]

    x = x_ref[...].reshape(rows, D_MODEL)
    xb = x.astype(_BF16)

    cos_t = jnp.concatenate([rope_ref[0]] * nb, axis=0) if nb > 1 else rope_ref[0]
    sin_up = jnp.concatenate([rope_ref[1]] * nb, axis=0) if nb > 1 else rope_ref[1]
    sin_dn = jnp.concatenate([rope_ref[2]] * nb, axis=0) if nb > 1 else rope_ref[2]

    zq = _dot(xb, w_in_ref[:, 0:Q_DIM])
    q_blocks = []
    for c in range(Q_DIM // LANES):
        qr = _rope(zq[:, c * LANES:(c + 1) * LANES], cos_t, sin_up, sin_dn)
        q_blocks.append((qr * ATTN_SCALE).astype(_BF16))
    zk = _dot(xb, w_in_ref[:, OFF_K:OFF_V])
    k_rot = jnp.concatenate(
        [_rope(zk[:, c * LANES:(c + 1) * LANES], cos_t, sin_up, sin_dn) for c in range(KV_DIM // LANES)], axis=1)
    v_new = _dot(xb, w_in_ref[:, OFF_V:OFF_XR])

    for bi in range(nb):
        r0 = bi * lt
        for h in range(N_KV_HEADS):
            klo, khi = _lo_hi(k_rot[r0:r0 + lt], h)
            vlo, vhi = _lo_hi(v_new[r0:r0 + lt], h)
            kw_ref[bi, h, 0, WINDOW:WINDOW + lt, :] = klo
            kw_ref[bi, h, 1, WINDOW:WINDOW + lt, :] = khi
            vw_ref[bi, h, 0, WINDOW:WINDOW + lt, :] = vlo
            vw_ref[bi, h, 1, WINDOW:WINDOW + lt, :] = vhi

    for bi in range(nb):
        r0 = bi * lt
        if lt >= WINDOW:
            kn_ref[bi] = k_rot[r0 + lt - WINDOW:r0 + lt]
            vn_ref[bi] = v_new[r0 + lt - WINDOW:r0 + lt]
        else:
            kn_ref[bi, 0:WINDOW - lt, :] = k0_ref[bi, lt:WINDOW, :]
            kn_ref[bi, WINDOW - lt:WINDOW, :] = k_rot[r0:r0 + lt]
            vn_ref[bi, 0:WINDOW - lt, :] = v0_ref[bi, lt:WINDOW, :]
            vn_ref[bi, WINDOW - lt:WINDOW, :] = v_new[r0:r0 + lt]

    w2 = 2 * lq
    wk = 2 * WINDOW + w2
    r_all = N_KV_HEADS * 2 * lq
    is_a2 = lax.broadcasted_iota(jnp.int32, (r_all, w2), 1) < lq
    lane_lo = lax.broadcasted_iota(jnp.int32, (r_all, LANES), 1) < HEAD_DIM
    key_row = lax.broadcasted_iota(jnp.int32, (wk, LANES), 0)
    key_is_a = jnp.logical_or(key_row < WINDOW, jnp.logical_and(key_row >= 2 * WINDOW, key_row < 2 * WINDOW + lq))
    key_lane_lo = lax.broadcasted_iota(jnp.int32, (wk, LANES), 1) < HEAD_DIM
    head_ones = jnp.where(key_is_a == key_lane_lo, 1.0, 0.0).astype(_BF16)
    sink_rows, sink_a_rows, sink_b_rows = sinkrows_ref[0], sinkrows_ref[1], sinkrows_ref[2]
    first = s_idx == 0
    units = [(bi, j) for bi in range(nb) for j in range(n_chunks)]
    for g0 in range(0, len(units), ATT_GROUP):
        group = units[g0:g0 + ATT_GROUP]
        scores = []
        vmats = []
        for bi, j in group:
            p0 = j * lq
            o0 = WINDOW + j * lq
            q0 = bi * lt + j * lq
            for h in range(N_KV_HEADS):
                kmat = jnp.concatenate([kw_ref[bi, h, 0, p0:p0 + WINDOW, :], kw_ref[bi, h, 1, p0:p0 + WINDOW, :],
                                        kw_ref[bi, h, 0, o0:o0 + lq, :], kw_ref[bi, h, 1, o0:o0 + lq, :]], axis=0)
                vmat = jnp.concatenate([vw_ref[bi, h, 0, p0:p0 + WINDOW, :], vw_ref[bi, h, 1, p0:p0 + WINDOW, :],
                                        vw_ref[bi, h, 0, o0:o0 + lq, :], vw_ref[bi, h, 1, o0:o0 + lq, :]], axis=0)
                vmats.append(jnp.concatenate([vmat, head_ones], axis=1))
                qs = jnp.concatenate([q_blocks[2 * h][q0:q0 + lq], q_blocks[2 * h + 1][q0:q0 + lq]], axis=0)
                sc_h = _dot_nt(qs, kmat)
                if not has_past and j * lq < WINDOW:
                    col = lax.broadcasted_iota(jnp.int32, sc_h.shape, 1)
                    n_bad = WINDOW - j * lq
                    bad = jnp.logical_or(col < n_bad, jnp.logical_and(col >= WINDOW, col < WINDOW + n_bad))
                    sc_h = jnp.where(jnp.logical_and(first, bad), NEG_INF, sc_h)
                scores.append(sc_h)
        n_u = len(group)
        sc = jnp.concatenate(scores, axis=0)
        if g0 == 0:
            ga = _dot(xb, w_in_ref[:, OFF_GA:OFF_GR])
            gr = _dot(xb, w_in_ref[:, OFF_GR:IN_DIM])
        tile_rows = lambda a: jnp.concatenate([a] * n_u, axis=0) if n_u > 1 else a
        is_a2g, lane_log, sinks_g = tile_rows(is_a2), tile_rows(lane_lo), tile_rows(sink_rows)
        sinks_ag, sinks_bg = tile_rows(sink_a_rows), tile_rows(sink_b_rows)
        c0 = sc[:, 0:LANES]
        c1 = sc[:, LANES:2 * LANES]
        c2 = sc[:, 2 * LANES:wk]
        if w2 == LANES:
            ma = jnp.max(jnp.maximum(c0, jnp.where(is_a2g, c2, NEG_INF)).astype(_BF16), axis=1, keepdims=True)
            mb = jnp.max(jnp.maximum(c1, jnp.where(is_a2g, NEG_INF, c2)).astype(_BF16), axis=1, keepdims=True)
        else:
            ma = jnp.maximum(jnp.max(c0.astype(_BF16), axis=1, keepdims=True),
                             jnp.max(jnp.where(is_a2g, c2, NEG_INF).astype(_BF16), axis=1, keepdims=True))
            mb = jnp.maximum(jnp.max(c1.astype(_BF16), axis=1, keepdims=True),
                             jnp.max(jnp.where(is_a2g, NEG_INF, c2).astype(_BF16), axis=1, keepdims=True))
        ma = jnp.maximum(jnp.broadcast_to(ma.astype(_F32), sinks_ag.shape), sinks_ag)
        mb = jnp.maximum(jnp.broadcast_to(mb.astype(_F32), sinks_bg.shape), sinks_bg)
        probs = jnp.concatenate([jnp.exp(c0 - ma), jnp.exp(c1 - mb),
                                 jnp.exp(c2 - jnp.where(is_a2g, ma[:, 0:w2], mb[:, 0:w2]))], axis=1).astype(_BF16)
        sink_term = jnp.exp(sinks_g - jnp.where(lane_log, ma, mb))
        for u, (bi, j) in enumerate(group):
            q0 = bi * lt + j * lq
            for h in range(N_KV_HEADS):
                r0 = (u * N_KV_HEADS + h) * 2 * lq
                od = _dot(probs[r0:r0 + 2 * lq], vmats[u * N_KV_HEADS + h])
                out = od[:, 0:LANES] / (od[:, LANES:2 * LANES] + sink_term[r0:r0 + 2 * lq])
                for p in range(2):
                    oat_ref[q0:q0 + lq, (2 * h + p) * LANES:(2 * h + p + 1) * LANES] = (
                        out[p * lq:(p + 1) * lq].astype(_BF16))

    if lt >= WINDOW:
        @pl.when(s_idx + 1 < n_s)
        def _carry_kv():
            for bi in range(nb):
                for h in range(N_KV_HEADS):
                    for v in range(2):
                        kw_ref[bi, h, v, 0:WINDOW, :] = kw_ref[bi, h, v, lt:lt + WINDOW, :]
                        vw_ref[bi, h, v, 0:WINDOW, :] = vw_ref[bi, h, v, lt:lt + WINDOW, :]

    xr = _dot(xb, w_in_ref[:, OFF_XR:OFF_YR])
    n_lb = D_RNN // LANES
    n_g = lt // SCAN_S
    conv_w = convw_ref[...]
    conv_b = convb_ref[...]
    for bi in range(nb):
        r0 = bi * lt
        tail = xr[r0 + lt - (CONV_W - 1):r0 + lt]
        cn_ref[bi] = tail
        for c in range(n_lb):
            ls = slice(c * LANES, (c + 1) * LANES)
            xp_ref[bi, c, CONV_HDR:CONV_HDR + lt, :] = xr[r0:r0 + lt, ls]
            frames = [xp_ref[bi, c, pl.ds(CONV_HDR + m, n_g, stride=SCAN_S), :]
                      for m in range(-(CONV_W - 1), SCAN_S)]
            for k in range(SCAN_S):
                acc = frames[k] * conv_w[0:1, ls]
                for t in range(1, CONV_W):
                    acc = acc + frames[k + t] * conv_w[t:t + 1, ls]
                xc_ref[c, pl.ds(r0 + k, n_g, stride=SCAN_S), :] = acc + conv_b[:, ls]
            xp_ref[bi, c, CONV_HDR - (CONV_W - 1):CONV_HDR, :] = tail[:, ls]

    yr = _dot(xb, w_in_ref[:, OFF_YR:OFF_GA])
    xc = jnp.concatenate([xc_ref[c] for c in range(n_lb)], axis=1)
    xcb = xc.astype(_BF16)
    lam = lam_ref[...]
    softplus_neg = jnp.maximum(-lam, 0.0) + jnp.log(1.0 + jnp.exp(-jnp.abs(lam)))
    for j in range(D_RNN // GATE_COLS):
        cs = slice(j * GATE_COLS, (j + 1) * GATE_COLS)
        g = _dot(xcb[:, GATE_STARTS[j]:GATE_STARTS[j] + GATE_K], wg_ref[j]) + bg_ref[j]
        i_gate = _half_logistic(g[:, GATE_COLS:2 * GATE_COLS])
        m_sp = (-0.5 * LRU_C) * softplus_neg[:, cs]
        log_a = jnp.tanh(g[:, 0:GATE_COLS]) * m_sp + m_sp
        a = jnp.exp(log_a)
        z = -jnp.tanh(log_a) * (1.0 + a * a)
        bv = jnp.where(z > 0.0, z * lax.rsqrt(z), 0.0) * i_gate * xc[:, cs]
        for cc in range(GATE_COLS // LANES):
            c = j * (GATE_COLS // LANES) + cc
            a_ref[c] = a[:, cc * LANES:(cc + 1) * LANES]
            b_ref[c] = bv[:, cc * LANES:(cc + 1) * LANES]

    for bi in range(nb):
        r0 = bi * lt
        a_tot, h_tot = [], []
        for c in range(n_lb):
            at = a_ref[c, pl.ds(r0, n_g, stride=SCAN_S), :]
            ht = b_ref[c, pl.ds(r0, n_g, stride=SCAN_S), :]
            for k in range(1, SCAN_S):
                ak = a_ref[c, pl.ds(r0 + k, n_g, stride=SCAN_S), :]
                ht = ak * ht + b_ref[c, pl.ds(r0 + k, n_g, stride=SCAN_S), :]
                at = ak * at
            a_tot.append(at)
            h_tot.append(ht)
        carry = [hc_ref[bi][:, c * LANES:(c + 1) * LANES] for c in range(n_lb)]
        for g in range(n_g):
            for c in range(n_lb):
                hin_ref[c, g:g + 1, :] = carry[c]
                carry[c] = a_tot[c][g:g + 1, :] * carry[c] + h_tot[c][g:g + 1, :]
        h_last = jnp.concatenate(carry, axis=1)
        hc_ref[bi] = h_last
        hn_ref[bi] = h_last
        for c in range(n_lb):
            hcur = hin_ref[c]
            for k in range(SCAN_S):
                hcur = a_ref[c, pl.ds(r0 + k, n_g, stride=SCAN_S), :] * hcur + b_ref[c, pl.ds(r0 + k, n_g, stride=SCAN_S), :]
                b_ref[c, pl.ds(r0 + k, n_g, stride=SCAN_S), :] = hcur

    gelu_t = jnp.tanh(yr * (GELU_C0 + GELU_C1 * (yr * yr)))
    half_h = jnp.concatenate([b_ref[c] for c in range(n_lb)], axis=1) * (0.5 * yr)
    rnn = half_h * gelu_t + half_h

    merged2 = ((jnp.tanh(ga) + 1.0) * _dot(oat_ref[...], wao_ref[...])
               + (jnp.tanh(gr) + 1.0) * _dot(rnn.astype(_BF16), wro_ref[...]))
    mix = _dot(merged2.astype(_BF16), wout_ref[...])
    x1 = _layer_norm(DN_ALPHA * x + mix, g1_ref[...], b1_ref[...])
    for c in range(SLAB):
        x1s_ref[pl.ds(c, rows, stride=SLAB), :] = x1[:, c * LANES:(c + 1) * LANES]

    logits = _dot_nt(rwt_ref[...], x1.astype(_BF16)) + rb_ref[...]
    e_iota = lax.broadcasted_iota(jnp.int32, logits.shape, 0)
    work = logits
    top_vals = []
    top_idx = []
    top_sel = []
    for _ in range(TOP_K):
        m = jnp.max(work, axis=0, keepdims=True)
        idx = jnp.min(jnp.where(work == m, e_iota, N_EXPERTS), axis=0, keepdims=True)
        sel = e_iota == idx
        top_vals.append(m)
        top_idx.append(idx)
        top_sel.append(sel)
        work = jnp.where(sel, -jnp.inf, work)
    exps = [jnp.exp(v - top_vals[0]) for v in top_vals]
    denom = exps[0]
    for e in exps[1:]:
        denom = denom + e

    onehot = jnp.zeros_like(logits)
    for sel in top_sel:
        onehot = onehot + jnp.where(sel, 1.0, 0.0)
    r_i = lax.broadcasted_iota(jnp.int32, (rows, rows), 0)
    c_i = lax.broadcasted_iota(jnp.int32, (rows, rows), 1)
    earlier = jnp.where(r_i < c_i, 1.0, 0.0).astype(_BF16)
    before = _dot(onehot.astype(_BF16), earlier) + run_ref[...]
    run_ref[...] = run_ref[...] + jnp.sum(onehot, axis=1, keepdims=True)
    cnt_ref[...] = run_ref[...]

    sub_m = lax.broadcasted_iota(jnp.int32, (2 * TOP_K, rows), 0)
    meta_i = jnp.zeros((2 * TOP_K, rows), jnp.int32)
    meta_g = jnp.zeros((2 * TOP_K, rows), _F32)
    for k in range(TOP_K):
        rank_k = jnp.sum(jnp.where(top_sel[k], before, 0.0), axis=0, keepdims=True).astype(jnp.int32)
        meta_i = jnp.where(sub_m == k, top_idx[k], meta_i)
        meta_i = jnp.where(sub_m == TOP_K + k, rank_k, meta_i)
        meta_g = jnp.where(sub_m == k, exps[k] / denom, meta_g)
    meta_ref[...] = meta_i
    gate_ref[...] = meta_g


def _const_spec(shape):
    nd = len(shape)
    return pl.BlockSpec(shape, lambda b, s, _nd=nd: (0,) * _nd, pipeline_mode=pl.Buffered(1))


def _mixer(x, rope, k0, v0, c0, h0, sinks, prm, *, has_past, nb, lt, lq):
    bsz, seq, _ = x.shape
    assert bsz % nb == 0 and seq % lt == 0 and lt % lq == 0 and lq % 32 == 0
    assert lt >= WINDOW or seq == lt
    n_s = seq // lt
    rows = nb * lt
    n_tok = bsz * seq
    assert lt % SCAN_S == 0
    kernel = functools.partial(_mixer_kernel, has_past, nb, lt, lq)
    r_all = N_KV_HEADS * 2 * lq
    sink_rows = jnp.broadcast_to(sinks.reshape(N_KV_HEADS * 2, 1, 2, 1), (N_KV_HEADS * 2, lq, 2, HEAD_DIM))
    sink_rows = sink_rows.reshape(r_all, LANES)
    sink_ab = jnp.broadcast_to(sinks.reshape(N_KV_HEADS * 2, 1, 2, 1), (N_KV_HEADS * 2, lq, 2, LANES))
    sink_rows = jnp.stack([sink_rows, sink_ab[:, :, 0].reshape(r_all, LANES), sink_ab[:, :, 1].reshape(r_all, LANES)])
    n_gate = D_RNN // GATE_COLS
    batch_spec = lambda shape: pl.BlockSpec((nb,) + shape, lambda b, s: (b,) + (0,) * len(shape))
    in_specs = [
        pl.BlockSpec((nb, lt, D_MODEL), lambda b, s: (b, s, 0)),
        pl.BlockSpec((3, lt, LANES), lambda b, s: (0, s, 0)),
        _const_spec((D_MODEL, IN_DIM)),
        batch_spec((WINDOW, KV_DIM)), batch_spec((WINDOW, KV_DIM)),
        batch_spec((CONV_W - 1, D_RNN)), batch_spec((1, D_RNN)),
        _const_spec((3, r_all, LANES)),
        _const_spec((CONV_W, D_RNN)), _const_spec((1, D_RNN)),
        _const_spec((n_gate, GATE_K, 2 * GATE_COLS)), _const_spec((n_gate, 1, 2 * GATE_COLS)), _const_spec((1, D_RNN)),
        _const_spec((Q_DIM, D_MODEL)), _const_spec((D_RNN, D_MODEL)), _const_spec((D_MODEL, D_MODEL)),
        _const_spec((1, D_MODEL)), _const_spec((1, D_MODEL)),
        _const_spec((N_EXPERTS, D_MODEL)), _const_spec((N_EXPERTS, 1)),
    ]
    tok_map = lambda b, s: (b * n_s + s, 0)
    tok_map_t = lambda b, s: (0, b * n_s + s)
    out_shape = (
        jax.ShapeDtypeStruct((n_tok * SLAB, LANES), _F32),
        jax.ShapeDtypeStruct((2 * TOP_K, n_tok), jnp.int32),
        jax.ShapeDtypeStruct((2 * TOP_K, n_tok), _F32),
        jax.ShapeDtypeStruct((N_EXPERTS, 1), _F32),
        jax.ShapeDtypeStruct((bsz, WINDOW, KV_DIM), _F32),
        jax.ShapeDtypeStruct((bsz, WINDOW, KV_DIM), _F32),
        jax.ShapeDtypeStruct((bsz, CONV_W - 1, D_RNN), _F32),
        jax.ShapeDtypeStruct((bsz, 1, D_RNN), _F32),
    )
    out_specs = (
        pl.BlockSpec((rows * SLAB, LANES), tok_map),
        pl.BlockSpec((2 * TOP_K, rows), tok_map_t),
        pl.BlockSpec((2 * TOP_K, rows), tok_map_t),
        pl.BlockSpec((N_EXPERTS, 1), lambda b, s: (0, 0)),
        batch_spec((WINDOW, KV_DIM)), batch_spec((WINDOW, KV_DIM)),
        batch_spec((CONV_W - 1, D_RNN)), batch_spec((1, D_RNN)),
    )
    scratch = [
        pltpu.VMEM((nb, N_KV_HEADS, 2, WINDOW + lt, LANES), _BF16),
        pltpu.VMEM((nb, N_KV_HEADS, 2, WINDOW + lt, LANES), _BF16),
        pltpu.VMEM((nb, D_RNN // LANES, CONV_HDR + lt, LANES), _F32),
        pltpu.VMEM((D_RNN // LANES, rows, LANES), _F32),
        pltpu.VMEM((D_RNN // LANES, rows, LANES), _F32),
        pltpu.VMEM((D_RNN // LANES, rows, LANES), _F32),
        pltpu.VMEM((D_RNN // LANES, lt // SCAN_S, LANES), _F32),
        pltpu.VMEM((nb, 1, D_RNN), _F32),
        pltpu.VMEM((rows, Q_DIM), _BF16),
        pltpu.VMEM((N_EXPERTS, 1), _F32),
    ]
    args = [x, rope, prm["w_in"], k0, v0, c0, h0, sink_rows, prm["conv_w"], prm["conv_b"], prm["wg"], prm["bg"],
            prm["lam"], prm["wao"], prm["wro"], prm["wout"], prm["g1"], prm["b1"], prm["rwt"], prm["rb"]]
    return pl.pallas_call(
        kernel,
        grid=(bsz // nb, n_s),
        in_specs=in_specs,
        out_specs=out_specs,
        out_shape=out_shape,
        scratch_shapes=scratch,
        compiler_params=pltpu.CompilerParams(
            dimension_semantics=("arbitrary", "arbitrary"), vmem_limit_bytes=VMEM_LIMIT_BYTES),
        name="mixer_past" if has_past else "mixer_prompt",
    )(*args)


def _slab_rows(ref, first_tok, n):
    return jnp.concatenate([ref[pl.ds(first_tok * SLAB + c, n, stride=SLAB), :] for c in range(SLAB)], axis=1)


def _dispatch_kernel(tt, tm, n_first, n_tiles, zt_ref, nu_ref, pos_ref, xa_ref, xb_ref, xs_hbm, zero_ref, sem):
    i = pl.program_id(0)

    @pl.when(i == 0)
    def _zero_padding():
        zero_ref[...] = jnp.zeros_like(zero_ref)

        def clear(tile):
            dst = xs_hbm.at[pl.ds(pl.multiple_of(tile * (tm * SLAB), SLAB), tm * SLAB)]
            cp = pltpu.make_async_copy(zero_ref, dst, sem)
            cp.start()
            cp.wait()

        for e in range(N_EXPERTS):
            pl.when(zt_ref[e] >= 0)(functools.partial(clear, zt_ref[e]))

        def clear_unused(tile, carry):
            clear(tile)
            return carry

        lax.fori_loop(nu_ref[0], n_tiles, clear_unused, 0)

    def scatter(src_ref):
        def issue(r, carry):
            src = src_ref.at[pl.ds(pl.multiple_of(r * SLAB, SLAB), SLAB)]
            for k in range(TOP_K):
                dst = xs_hbm.at[pl.ds(pl.multiple_of(pos_ref[0, 0, r * TOP_K + k] * SLAB, SLAB), SLAB)]
                pltpu.make_async_copy(src, dst, sem).start(priority=k % 2)
            return carry

        lax.fori_loop(0, tt, issue, 0)

    pl.when(i < n_first)(lambda: scatter(xa_ref))
    pl.when(i >= n_first)(lambda: scatter(xb_ref))
    n_rows = tt * TOP_K * SLAB
    pltpu.make_async_copy(xs_hbm.at[pl.ds(0, n_rows)], xs_hbm.at[pl.ds(0, n_rows)], sem).wait()


def _dispatch(x1s_a, x1s_b, pos, zero_tiles, n_used, *, tt, tm, n_tiles):
    n_a = x1s_a.shape[0] // SLAB
    n_b = x1s_b.shape[0] // SLAB
    assert n_a % tt == 0 and n_b % tt == 0 and n_tiles * tm >= tt * TOP_K
    n_first = n_a // tt
    n_steps = (n_a + n_b) // tt
    return pl.pallas_call(
        functools.partial(_dispatch_kernel, tt, tm, n_first, n_tiles),
        grid_spec=pltpu.PrefetchScalarGridSpec(
            num_scalar_prefetch=2,
            grid=(n_steps,),
            in_specs=[
                pl.BlockSpec((1, 1, tt * TOP_K), lambda i, zt, nu: (i, 0, 0), memory_space=pltpu.SMEM),
                pl.BlockSpec((tt * SLAB, LANES), lambda i, zt, nu: (jnp.minimum(i, n_first - 1), 0)),
                pl.BlockSpec((tt * SLAB, LANES), lambda i, zt, nu: (jnp.maximum(i - n_first, 0), 0)),
            ],
            out_specs=pl.BlockSpec(memory_space=pl.ANY),
            scratch_shapes=[pltpu.VMEM((tm * SLAB, LANES), _F32), pltpu.SemaphoreType.DMA],
        ),
        out_shape=jax.ShapeDtypeStruct((n_tiles * tm * SLAB, LANES), _F32),
        compiler_params=pltpu.CompilerParams(dimension_semantics=("arbitrary",)),
        name="moe_dispatch",
    )(zero_tiles, n_used, pos.reshape(n_steps, 1, tt * TOP_K), x1s_a, x1s_b)


def _expert_kernel(tm, te_ref, nu_ref, tv_ref, xs_ref, wgu_ref, bgu_ref, wd_ref, bd_ref, ys_ref, wgu_b, wd_b):
    i = pl.program_id(0)
    half = tm // 2

    def mlp_rows(n):
        x = _slab_rows(xs_ref, 0, n).astype(_BF16)
        hgu = _dot(x, wgu_b[...]) + bgu_ref[0]
        glu = jnp.minimum(hgu[:, 0:D_FF], SWIGLU_LIMIT)
        lin = jnp.clip(hgu[:, D_FF:2 * D_FF], -SWIGLU_LIMIT, SWIGLU_LIMIT)
        hh = glu * _half_logistic((0.5 * SWIGLU_ALPHA) * glu) * (lin + 1.0)
        y = _dot(hh.astype(_BF16), wd_b[...]) + bd_ref[0]
        for c in range(SLAB):
            ys_ref[pl.ds(c, n, stride=SLAB), :] = y[:, c * LANES:(c + 1) * LANES]

    @pl.when(i < nu_ref[0])
    def _tile():
        prev_e = te_ref[jnp.maximum(i - 1, 0)]

        @pl.when(jnp.logical_or(i == 0, te_ref[i] != prev_e))
        def _new_expert():
            wgu_b[...] = wgu_ref[0].astype(_BF16)
            wd_b[...] = wd_ref[0].astype(_BF16)

        few = tv_ref[i] <= half

        @pl.when(jnp.logical_not(few))
        def _all_rows():
            mlp_rows(tm)

        @pl.when(few)
        def _first_half():
            mlp_rows(half)
            ys_ref[pl.ds(half * SLAB, half * SLAB), :] = jnp.zeros((half * SLAB, LANES), _F32)

    @pl.when(i >= nu_ref[0])
    def _unused_tile():
        ys_ref[...] = jnp.zeros_like(ys_ref)


def _experts(xs, tile_expert, n_used, tile_valid, prm, *, tm):
    n_tiles = xs.shape[0] // (tm * SLAB)
    row_map = lambda i, te, nu, tv: (jnp.minimum(i, nu[0] - 1), 0)
    out_map = lambda i, te, nu, tv: (i, 0)
    exp_map = lambda i, te, nu, tv: (te[i], 0, 0)
    return pl.pallas_call(
        functools.partial(_expert_kernel, tm),
        grid_spec=pltpu.PrefetchScalarGridSpec(
            num_scalar_prefetch=3,
            grid=(n_tiles,),
            in_specs=[
                pl.BlockSpec((tm * SLAB, LANES), row_map),
                pl.BlockSpec((1, D_MODEL, 2 * D_FF), exp_map),
                pl.BlockSpec((1, 1, 2 * D_FF), exp_map),
                pl.BlockSpec((1, D_FF, D_MODEL), exp_map),
                pl.BlockSpec((1, 1, D_MODEL), exp_map),
            ],
            out_specs=pl.BlockSpec((tm * SLAB, LANES), out_map),
            scratch_shapes=[pltpu.VMEM((D_MODEL, 2 * D_FF), _BF16), pltpu.VMEM((D_FF, D_MODEL), _BF16)],
        ),
        out_shape=jax.ShapeDtypeStruct(xs.shape, _F32),
        compiler_params=pltpu.CompilerParams(
            dimension_semantics=("arbitrary",), vmem_limit_bytes=VMEM_LIMIT_BYTES),
        name="moe_experts",
    )(tile_expert, n_used, tile_valid, xs, prm["wgu"], prm["bgu"], prm["wd"], prm["bd"])


def _combine_kernel(tt, n_first, n_steps, pos_ref, pos_next_ref, gate_ref, xa_ref, xb_ref, ys_hbm, g2_ref, b2_ref,
                    ya_ref, yb_ref, ybuf, sems):
    i = pl.program_id(0)
    n_rows = tt * TOP_K * SLAB

    def gather(p_ref, slot):
        def issue(r, carry):
            for k in range(TOP_K):
                src = ys_hbm.at[pl.ds(pl.multiple_of(p_ref[0, 0, r * TOP_K + k] * SLAB, SLAB), SLAB)]
                dst = ybuf.at[slot, pl.ds(pl.multiple_of((k * tt + r) * SLAB, SLAB), SLAB)]
                pltpu.make_async_copy(src, dst, sems.at[slot]).start(priority=k % 2)
            return carry

        lax.fori_loop(0, tt, issue, 0)

    pl.when(i == 0)(lambda: gather(pos_ref, 0))

    def tile(slot):
        pl.when(i + 1 < n_steps)(lambda: gather(pos_next_ref, 1 - slot))
        pltpu.make_async_copy(ys_hbm.at[pl.ds(0, n_rows)], ybuf.at[slot], sems.at[slot]).wait()
        rows_ref = ybuf.at[slot]
        gates = gate_ref[...]
        acc = gates[:, 0:1] * _slab_rows(rows_ref, 0, tt)
        for k in range(1, TOP_K):
            acc = acc + gates[:, k:k + 1] * _slab_rows(rows_ref, k * tt, tt)

        def finish(x_ref, y_ref):
            x1 = _slab_rows(x_ref, 0, tt)
            y_ref[...] = _layer_norm(DN_ALPHA * x1 + acc, g2_ref[...], b2_ref[...])

        pl.when(i < n_first)(lambda: finish(xa_ref, ya_ref))
        pl.when(i >= n_first)(lambda: finish(xb_ref, yb_ref))

    for slot in range(2):
        pl.when(i % 2 == slot)(functools.partial(tile, slot))


def _combine(x1s_a, x1s_b, ys, pos, gates, prm, *, tt):
    n_a = x1s_a.shape[0] // SLAB
    n_b = x1s_b.shape[0] // SLAB
    assert n_a % tt == 0 and n_b % tt == 0
    n_first = n_a // tt
    n_steps = (n_a + n_b) // tt
    a_map = lambda i: (jnp.minimum(i, n_first - 1), 0)
    b_map = lambda i: (jnp.maximum(i - n_first, 0), 0)
    pos3 = pos.reshape(n_steps, 1, tt * TOP_K)
    return pl.pallas_call(
        functools.partial(_combine_kernel, tt, n_first, n_steps),
        grid=(n_steps,),
        in_specs=[
            pl.BlockSpec((1, 1, tt * TOP_K), lambda i: (i, 0, 0), memory_space=pltpu.SMEM),
            pl.BlockSpec((1, 1, tt * TOP_K), lambda i: (jnp.minimum(i + 1, n_steps - 1), 0, 0),
                         memory_space=pltpu.SMEM),
            pl.BlockSpec((tt, TOP_K), lambda i: (i, 0)),
            pl.BlockSpec((tt * SLAB, LANES), a_map),
            pl.BlockSpec((tt * SLAB, LANES), b_map),
            pl.BlockSpec(memory_space=pl.ANY),
            pl.BlockSpec((1, D_MODEL), lambda i: (0, 0)),
            pl.BlockSpec((1, D_MODEL), lambda i: (0, 0)),
        ],
        out_specs=(pl.BlockSpec((tt, D_MODEL), a_map), pl.BlockSpec((tt, D_MODEL), b_map)),
        out_shape=(jax.ShapeDtypeStruct((n_a, D_MODEL), _F32), jax.ShapeDtypeStruct((n_b, D_MODEL), _F32)),
        scratch_shapes=[pltpu.VMEM((2, TOP_K * tt * SLAB, LANES), _F32), pltpu.SemaphoreType.DMA((2,))],
        compiler_params=pltpu.CompilerParams(
            dimension_semantics=("arbitrary",), vmem_limit_bytes=VMEM_LIMIT_BYTES),
        name="moe_combine",
    )(pos3, pos3, gates, x1s_a, x1s_b, ys, prm["g2"], prm["b2"])


def _route(meta_a, meta_b, gate_a, gate_b, counts_a, counts_b, tm, n_tiles):
    meta = jnp.concatenate([meta_a, meta_b], axis=1).T
    idx = meta[:, 0:TOP_K]
    rank = meta[:, TOP_K:2 * TOP_K]
    gates = jnp.concatenate([gate_a[0:TOP_K], gate_b[0:TOP_K]], axis=1).T
    cp = counts_a.reshape(-1).astype(jnp.int32)
    cs = counts_b.reshape(-1).astype(jnp.int32)
    is_b = (jnp.arange(meta.shape[0]) >= meta_a.shape[1])[:, None]
    is_e = idx[:, :, None] == jnp.arange(N_EXPERTS, dtype=idx.dtype)
    lookup = lambda table: jnp.sum(jnp.where(is_e, table, 0), axis=-1)
    rank = rank + jnp.where(is_b, lookup(cp), 0)
    tiles_per = (cp + cs + tm - 1) // tm
    tile_end = jnp.cumsum(tiles_per)
    tile_start = tile_end - tiles_per
    pos = lookup(tile_start) * tm + rank
    n_used = tile_end[-1:]
    tile_ids = jnp.arange(n_tiles, dtype=jnp.int32)
    tile_expert = jnp.minimum(jnp.sum(tile_ids[:, None] >= tile_end[None, :], axis=1), N_EXPERTS - 1)
    last_e = tile_expert[jnp.maximum(n_used[0] - 1, 0)]
    tile_expert = jnp.where(tile_ids < n_used[0], tile_expert, last_e).astype(jnp.int32)
    zero_tiles = jnp.where(tiles_per > 0, tile_end - 1, -1).astype(jnp.int32)
    tile_valid = jnp.clip((cp + cs)[tile_expert] - (tile_ids - tile_start[tile_expert]) * tm, 0, tm).astype(jnp.int32)
    return pos.astype(jnp.int32), gates, tile_expert, n_used.astype(jnp.int32), zero_tiles, tile_valid


def _rope_tables(pos):
    half = ROT_DIM // 2
    inv_freq = ROPE_THETA ** (-jnp.arange(half, dtype=_F32) / half)
    ang = pos.astype(_F32)[:, None] * inv_freq[None, :]
    cos, sin = jnp.cos(ang), jnp.sin(ang)
    n = pos.shape[0]
    ones = jnp.ones((n, HEAD_DIM - ROT_DIM), _F32)
    zeros = jnp.zeros((n, HEAD_DIM - ROT_DIM), _F32)
    zh = jnp.zeros((n, half), _F32)
    cos_t = jnp.concatenate([cos, cos, ones], axis=1)
    sin_up = jnp.concatenate([-sin, zh, zeros], axis=1)
    sin_dn = jnp.concatenate([zh, sin, zeros], axis=1)
    tab = jnp.stack([cos_t, sin_up, sin_dn])
    return jnp.concatenate([tab, tab], axis=2)


def _block_diag(w):
    hd, d, _ = w.shape
    eye = jnp.eye(hd, dtype=w.dtype)
    return (eye[:, None, :, None] * w[:, :, None, :]).reshape(hd * d, hd * d)


def _gate_blocks(wa, wx):
    blocks = []
    head_of = np.arange(D_RNN) // RNN_HEAD_DIM
    for j, k0 in enumerate(GATE_STARTS):
        cols = np.arange(j * GATE_COLS, (j + 1) * GATE_COLS)
        reach = np.flatnonzero(np.isin(head_of, head_of[cols]))
        assert reach.min() >= k0 and reach.max() < k0 + GATE_K
        cs = slice(j * GATE_COLS, (j + 1) * GATE_COLS)
        blocks.append(jnp.concatenate([wa[k0:k0 + GATE_K, cs], wx[k0:k0 + GATE_K, cs]], axis=1))
    return jnp.stack(blocks)


def kernel(x_prompt, x_sample, cache_k, cache_v, state_conv, state_h, w_in, attn_sinks, w_attn_out, conv_w, conv_b, gate_a_w, gate_a_b, gate_x_w, gate_x_b, lru_lambda, w_rnn_out, w_out, ln1_g, ln1_b, router_w, router_b, w_gate_up, b_gate_up, w_down, b_down, ln2_g, ln2_b):
    assert w_in.shape[0] == DEPTH == 1
    l = 0
    bsz, seq, _ = x_prompt.shape
    dbsz, dseq, _ = x_sample.shape
    row = lambda a: a.reshape(1, -1)
    prm = {
        "w_in": (w_in[l] * jnp.where(jnp.arange(IN_DIM) >= OFF_GA, 0.5, 1.0)).astype(_BF16),
        "conv_w": conv_w[l], "conv_b": row(conv_b[l]),
        "wg": (0.5 * _gate_blocks(_block_diag(gate_a_w[l]), _block_diag(gate_x_w[l]))).astype(_BF16),
        "bg": 0.5 * jnp.concatenate([gate_a_b[l].reshape(-1, 1, GATE_COLS),
                                     gate_x_b[l].reshape(-1, 1, GATE_COLS)], axis=2),
        "lam": row(lru_lambda[l]),
        "wao": w_attn_out[l].astype(_BF16), "wro": w_rnn_out[l].astype(_BF16), "wout": (0.5 * w_out[l]).astype(_BF16),
        "g1": row(ln1_g[l]), "b1": row(ln1_b[l]),
        "rwt": router_w[l].T.astype(_BF16), "rb": router_b[l].reshape(-1, 1),
        "wgu": w_gate_up[l], "bgu": b_gate_up[l][:, None, :],
        "wd": w_down[l], "bd": b_down[l][:, None, :],
        "g2": row(ln2_g[l]), "b2": row(ln2_b[l]),
    }
    sinks = attn_sinks[l]

    rope_p = _rope_tables(jnp.arange(seq, dtype=jnp.int32))
    rope_s = _rope_tables(PAST_LEN + jnp.arange(dseq, dtype=jnp.int32))

    n_prompt = bsz * seq
    n_tok = n_prompt + dbsz * dseq
    zeros_kv = jnp.zeros((bsz, WINDOW, KV_DIM), _F32)
    x1_p, meta_p, gate_p, cnt_p, pk, pv, pc, ph = _mixer(
        x_prompt, rope_p, zeros_kv, zeros_kv, jnp.zeros((bsz, CONV_W - 1, D_RNN), _F32),
        jnp.zeros((bsz, 1, D_RNN), _F32), sinks, prm, has_past=False, nb=1, lt=min(MIXER_ROWS, seq), lq=CHUNK)
    x1_s, meta_s, gate_s, cnt_s, sk, sv, sc, sh = _mixer(
        x_sample, rope_s, cache_k[l].reshape(dbsz, WINDOW, KV_DIM), cache_v[l].reshape(dbsz, WINDOW, KV_DIM),
        state_conv[l], state_h[l][:, None, :], sinks, prm, has_past=True, nb=min(PAST_ROWS // dseq, dbsz),
        lt=dseq, lq=dseq)

    tm = EXPERT_ROWS
    n_tiles = (n_tok * TOP_K + N_EXPERTS * (tm - 1)) // tm + 1
    pos, gates, tile_expert, n_used, zero_tiles, tile_valid = _route(
        meta_p, meta_s, gate_p, gate_s, cnt_p, cnt_s, tm, n_tiles)
    xs = _dispatch(x1_p, x1_s, pos, zero_tiles, n_used, tt=TOKEN_ROWS, tm=tm, n_tiles=n_tiles)
    ysort = _experts(xs, tile_expert, n_used, tile_valid, prm, tm=tm)
    yp, ys = _combine(x1_p, x1_s, ysort, pos, gates, prm, tt=TOKEN_ROWS)
    yp = yp.reshape(bsz, seq, D_MODEL)
    ys = ys.reshape(dbsz, dseq, D_MODEL)

    kv5 = lambda a: a.reshape(1, a.shape[0], WINDOW, N_KV_HEADS, HEAD_DIM)
    return (yp, ys, kv5(pk), kv5(pv), pc[None], ph.reshape(1, bsz, D_RNN),
            kv5(sk), kv5(sv), sc[None], sh.reshape(1, dbsz, D_RNN))
```

```python
import functools

import jax
import jax.numpy as jnp
import numpy as np
from jax import lax
from jax.experimental import pallas as pl
from jax.experimental.pallas import tpu as pltpu

D_MODEL = 1024
PAST_LEN = 1024
CHUNK = 64
N_HEADS = 16
N_KV_HEADS = 4
HEAD_DIM = 64
GROUP = N_HEADS // N_KV_HEADS
ROT_DIM = HEAD_DIM // 4
ROPE_THETA = 500000.0
WINDOW = 128
ATTN_SCALE = HEAD_DIM ** -0.5
NEG_INF = -1e30
D_RNN = 1280
RNN_HEADS = 16
RNN_HEAD_DIM = D_RNN // RNN_HEADS
CONV_W = 4
LRU_C = 8.0
N_EXPERTS = 32
TOP_K = 4
D_FF = 1024
SWIGLU_ALPHA = 1.702
SWIGLU_LIMIT = 7.0
LN_EPS = 1e-5
DEPTH = 1
DN_ALPHA = (2.0 * DEPTH) ** 0.25

Q_DIM = N_HEADS * HEAD_DIM
KV_DIM = N_KV_HEADS * HEAD_DIM
OFF_K = Q_DIM
OFF_V = OFF_K + KV_DIM
OFF_XR = OFF_V + KV_DIM
OFF_YR = OFF_XR + D_RNN
OFF_GA = OFF_YR + D_RNN
OFF_GR = OFF_GA + D_MODEL
IN_DIM = OFF_GR + D_MODEL

LANES = 128
SLAB = D_MODEL // LANES
MIXER_ROWS = 256
PAST_ROWS = 256
EXPERT_ROWS = 512
TOKEN_ROWS = 512
GELU_C0 = float(np.sqrt(2.0 / np.pi))
GELU_C1 = 0.044715 * GELU_C0
ATT_GROUP = 4
SCAN_S = 4
GATE_COLS = 256
GATE_K = 512
GATE_STARTS = tuple(min(max((j * GATE_COLS // RNN_HEAD_DIM) * RNN_HEAD_DIM // LANES * LANES, 0), D_RNN - GATE_K)
                    for j in range(D_RNN // GATE_COLS))
CONV_HDR = 8
VMEM_LIMIT_BYTES = 56 * 1024 * 1024

_BF16 = jnp.bfloat16
_F32 = jnp.float32


def _dot(a, b):
    return jnp.dot(a, b, preferred_element_type=_F32)


def _dot_nt(a, b):
    return lax.dot_general(a, b, (((1,), (1,)), ((), ())), preferred_element_type=_F32)


def _half_logistic(h):
    return 0.5 * jnp.tanh(h) + 0.5


def _layer_norm(x, g, b):
    mu = jnp.mean(x, axis=-1, keepdims=True)
    xc = x - mu
    var = jnp.mean(xc * xc, axis=-1, keepdims=True)
    return xc * lax.rsqrt(var + LN_EPS) * g + b


def _rope(x, cos_t, sin_up, sin_dn):
    return x * cos_t + pltpu.roll(x, LANES - ROT_DIM // 2, 1) * sin_up + pltpu.roll(x, ROT_DIM // 2, 1) * sin_dn


def _lo_hi(slab, h):
    blk = slab[:, (h // 2) * LANES:(h // 2 + 1) * LANES]
    lane = lax.broadcasted_iota(jnp.int32, blk.shape, 1)
    if h % 2 == 0:
        lo = jnp.where(lane < HEAD_DIM, blk, 0.0)
        hi = pltpu.roll(lo, HEAD_DIM, 1)
    else:
        hi = jnp.where(lane >= HEAD_DIM, blk, 0.0)
        lo = pltpu.roll(hi, HEAD_DIM, 1)
    return lo.astype(_BF16), hi.astype(_BF16)


def _mixer_kernel(has_past, nb, lt, lq,
                  x_ref, rope_ref, w_in_ref, k0_ref, v0_ref, c0_ref, h0_ref, sinkrows_ref,
                  convw_ref, convb_ref, wg_ref, bg_ref, lam_ref, wao_ref, wro_ref, wout_ref,
                  g1_ref, b1_ref, rwt_ref, rb_ref,
                  x1s_ref, meta_ref, gate_ref, cnt_ref, kn_ref, vn_ref, cn_ref, hn_ref,
                  kw_ref, vw_ref, xp_ref, xc_ref, a_ref, b_ref, hin_ref, hc_ref, oat_ref, run_ref):
    s_idx = pl.program_id(1)
    n_s = pl.num_programs(1)
    rows = nb * lt
    n_chunks = lt // lq

    @pl.when(jnp.logical_and(pl.program_id(0) == 0, s_idx == 0))
    def _init_counts():
        run_ref[...] = jnp.zeros_like(run_ref)

    @pl.when(s_idx == 0)
    def _init():
        for bi in range(nb):
            for h in range(N_KV_HEADS):
                klo, khi = _lo_hi(k0_ref[bi], h)
                vlo, vhi = _lo_hi(v0_ref[bi], h)
                kw_ref[bi, h, 0, 0:WINDOW, :] = klo
                kw_ref[bi, h, 1, 0:WINDOW, :] = khi
                vw_ref[bi, h, 0, 0:WINDOW, :] = vlo
                vw_ref[bi, h, 1, 0:WINDOW, :] = vhi
            for c in range(D_RNN // LANES):
                xp_ref[bi, c, CONV_HDR - (CONV_W - 1):CONV_HDR, :] = c0_ref[bi, :, c * LANES:(c + 1) * LANES]
            hc_ref[bi] = h0_ref[bi]

    x = x_ref[...].reshape(rows, D_MODEL)
    xb = x.astype(_BF16)

    cos_t = jnp.concatenate([rope_ref[0]] * nb, axis=0) if nb > 1 else rope_ref[0]
    sin_up = jnp.concatenate([rope_ref[1]] * nb, axis=0) if nb > 1 else rope_ref[1]
    sin_dn = jnp.concatenate([rope_ref[2]] * nb, axis=0) if nb > 1 else rope_ref[2]

    zq = _dot(xb, w_in_ref[:, 0:Q_DIM])
    p_m = lax.broadcasted_iota(jnp.int32, (LANES, LANES), 0)
    p_l = lax.broadcasted_iota(jnp.int32, (LANES, LANES), 1)
    p_d = p_l % HEAD_DIM
    partner = jnp.where(p_d < ROT_DIM // 2, p_l + ROT_DIM // 2, jnp.where(p_d < ROT_DIM, p_l - ROT_DIM // 2, -1))
    perm = jnp.where(p_m == partner, 1.0, 0.0).astype(_BF16)
    cos_q = cos_t * ATTN_SCALE
    sin_q = (sin_up + sin_dn) * ATTN_SCALE
    q_blocks = []
    for c in range(Q_DIM // LANES):
        zb = zq[:, c * LANES:(c + 1) * LANES]
        q_blocks.append((zb * cos_q + _dot(zb.astype(_BF16), perm) * sin_q).astype(_BF16))
    zk = _dot(xb, w_in_ref[:, OFF_K:OFF_V])
    k_rot = jnp.concatenate(
        [_rope(zk[:, c * LANES:(c + 1) * LANES], cos_t, sin_up, sin_dn) for c in range(KV_DIM // LANES)], axis=1)
    v_new = _dot(xb, w_in_ref[:, OFF_V:OFF_XR])

    for bi in range(nb):
        r0 = bi * lt
        for h in range(N_KV_HEADS):
            klo, khi = _lo_hi(k_rot[r0:r0 + lt], h)
            vlo, vhi = _lo_hi(v_new[r0:r0 + lt], h)
            kw_ref[bi, h, 0, WINDOW:WINDOW + lt, :] = klo
            kw_ref[bi, h, 1, WINDOW:WINDOW + lt, :] = khi
            vw_ref[bi, h, 0, WINDOW:WINDOW + lt, :] = vlo
            vw_ref[bi, h, 1, WINDOW:WINDOW + lt, :] = vhi

    for bi in range(nb):
        r0 = bi * lt
        if lt >= WINDOW:
            kn_ref[bi] = k_rot[r0 + lt - WINDOW:r0 + lt]
            vn_ref[bi] = v_new[r0 + lt - WINDOW:r0 + lt]
        else:
            kn_ref[bi, 0:WINDOW - lt, :] = k0_ref[bi, lt:WINDOW, :]
            kn_ref[bi, WINDOW - lt:WINDOW, :] = k_rot[r0:r0 + lt]
            vn_ref[bi, 0:WINDOW - lt, :] = v0_ref[bi, lt:WINDOW, :]
            vn_ref[bi, WINDOW - lt:WINDOW, :] = v_new[r0:r0 + lt]

    w2 = 2 * lq
    wk = 2 * WINDOW + w2
    r_all = N_KV_HEADS * 2 * lq
    is_a2 = lax.broadcasted_iota(jnp.int32, (r_all, w2), 1) < lq
    lane_lo = lax.broadcasted_iota(jnp.int32, (r_all, LANES), 1) < HEAD_DIM
    key_row = lax.broadcasted_iota(jnp.int32, (wk, LANES), 0)
    key_is_a = jnp.logical_or(key_row < WINDOW, jnp.logical_and(key_row >= 2 * WINDOW, key_row < 2 * WINDOW + lq))
    key_lane_lo = lax.broadcasted_iota(jnp.int32, (wk, LANES), 1) < HEAD_DIM
    head_ones = jnp.where(key_is_a == key_lane_lo, 1.0, 0.0).astype(_BF16)
    sink_rows, sink_a_rows, sink_b_rows = sinkrows_ref[0], sinkrows_ref[1], sinkrows_ref[2]
    first = s_idx == 0
    units = [(bi, j) for bi in range(nb) for j in range(n_chunks)]
    for g0 in range(0, len(units), ATT_GROUP):
        group = units[g0:g0 + ATT_GROUP]
        scores = []
        vmats = []
        for bi, j in group:
            p0 = j * lq
            o0 = WINDOW + j * lq
            q0 = bi * lt + j * lq
            for h in range(N_KV_HEADS):
                kmat = jnp.concatenate([kw_ref[bi, h, 0, p0:p0 + WINDOW, :], kw_ref[bi, h, 1, p0:p0 + WINDOW, :],
                                        kw_ref[bi, h, 0, o0:o0 + lq, :], kw_ref[bi, h, 1, o0:o0 + lq, :]], axis=0)
                vmat = jnp.concatenate([vw_ref[bi, h, 0, p0:p0 + WINDOW, :], vw_ref[bi, h, 1, p0:p0 + WINDOW, :],
                                        vw_ref[bi, h, 0, o0:o0 + lq, :], vw_ref[bi, h, 1, o0:o0 + lq, :]], axis=0)
                vmats.append(jnp.concatenate([vmat, head_ones], axis=1))
                qs = jnp.concatenate([q_blocks[2 * h][q0:q0 + lq], q_blocks[2 * h + 1][q0:q0 + lq]], axis=0)
                sc_h = _dot_nt(qs, kmat)
                if not has_past and j * lq < WINDOW:
                    col = lax.broadcasted_iota(jnp.int32, sc_h.shape, 1)
                    n_bad = WINDOW - j * lq
                    bad = jnp.logical_or(col < n_bad, jnp.logical_and(col >= WINDOW, col < WINDOW + n_bad))
                    sc_h = jnp.where(jnp.logical_and(first, bad), NEG_INF, sc_h)
                scores.append(sc_h)
        n_u = len(group)
        sc = jnp.concatenate(scores, axis=0)
        if g0 == 0:
            ga = _dot(xb, w_in_ref[:, OFF_GA:OFF_GR])
            gr = _dot(xb, w_in_ref[:, OFF_GR:IN_DIM])
        tile_rows = lambda a: jnp.concatenate([a] * n_u, axis=0) if n_u > 1 else a
        is_a2g, lane_log, sinks_g = tile_rows(is_a2), tile_rows(lane_lo), tile_rows(sink_rows)
        sinks_ag, sinks_bg = tile_rows(sink_a_rows), tile_rows(sink_b_rows)
        c0 = sc[:, 0:LANES]
        c1 = sc[:, LANES:2 * LANES]
        c2 = sc[:, 2 * LANES:wk]
        if w2 == LANES:
            ma = jnp.max(jnp.maximum(c0, jnp.where(is_a2g, c2, NEG_INF)).astype(_BF16), axis=1, keepdims=True)
            mb = jnp.max(jnp.maximum(c1, jnp.where(is_a2g, NEG_INF, c2)).astype(_BF16), axis=1, keepdims=True)
        else:
            ma = jnp.maximum(jnp.max(c0.astype(_BF16), axis=1, keepdims=True),
                             jnp.max(jnp.where(is_a2g, c2, NEG_INF).astype(_BF16), axis=1, keepdims=True))
            mb = jnp.maximum(jnp.max(c1.astype(_BF16), axis=1, keepdims=True),
                             jnp.max(jnp.where(is_a2g, NEG_INF, c2).astype(_BF16), axis=1, keepdims=True))
        ma = jnp.maximum(jnp.broadcast_to(ma.astype(_F32), sinks_ag.shape), sinks_ag)
        mb = jnp.maximum(jnp.broadcast_to(mb.astype(_F32), sinks_bg.shape), sinks_bg)
        probs = jnp.concatenate([jnp.exp(c0 - ma), jnp.exp(c1 - mb),
                                 jnp.exp(c2 - jnp.where(is_a2g, ma[:, 0:w2], mb[:, 0:w2]))], axis=1).astype(_BF16)
        sink_term = jnp.exp(sinks_g - jnp.where(lane_log, ma, mb))
        for u, (bi, j) in enumerate(group):
            q0 = bi * lt + j * lq
            for h in range(N_KV_HEADS):
                r0 = (u * N_KV_HEADS + h) * 2 * lq
                od = _dot(probs[r0:r0 + 2 * lq], vmats[u * N_KV_HEADS + h])
                out = od[:, 0:LANES] / (od[:, LANES:2 * LANES] + sink_term[r0:r0 + 2 * lq])
                for p in range(2):
                    oat_ref[q0:q0 + lq, (2 * h + p) * LANES:(2 * h + p + 1) * LANES] = (
                        out[p * lq:(p + 1) * lq].astype(_BF16))

    if lt >= WINDOW:
        @pl.when(s_idx + 1 < n_s)
        def _carry_kv():
            for bi in range(nb):
                for h in range(N_KV_HEADS):
                    for v in range(2):
                        kw_ref[bi, h, v, 0:WINDOW, :] = kw_ref[bi, h, v, lt:lt + WINDOW, :]
                        vw_ref[bi, h, v, 0:WINDOW, :] = vw_ref[bi, h, v, lt:lt + WINDOW, :]

    xr = _dot(xb, w_in_ref[:, OFF_XR:OFF_YR])
    n_lb = D_RNN // LANES
    n_g = lt // SCAN_S
    conv_w = convw_ref[...]
    conv_b = convb_ref[...]
    for bi in range(nb):
        r0 = bi * lt
        tail = xr[r0 + lt - (CONV_W - 1):r0 + lt]
        cn_ref[bi] = tail
        for c in range(n_lb):
            ls = slice(c * LANES, (c + 1) * LANES)
            xp_ref[bi, c, CONV_HDR:CONV_HDR + lt, :] = xr[r0:r0 + lt, ls]
            frames = [xp_ref[bi, c, pl.ds(CONV_HDR + m, n_g, stride=SCAN_S), :]
                      for m in range(-(CONV_W - 1), SCAN_S)]
            for k in range(SCAN_S):
                acc = frames[k] * conv_w[0:1, ls]
                for t in range(1, CONV_W):
                    acc = acc + frames[k + t] * conv_w[t:t + 1, ls]
                xc_ref[c, pl.ds(r0 + k, n_g, stride=SCAN_S), :] = acc + conv_b[:, ls]
            xp_ref[bi, c, CONV_HDR - (CONV_W - 1):CONV_HDR, :] = tail[:, ls]

    yr = _dot(xb, w_in_ref[:, OFF_YR:OFF_GA])
    xc = jnp.concatenate([xc_ref[c] for c in range(n_lb)], axis=1)
    xcb = xc.astype(_BF16)
    lam = lam_ref[...]
    softplus_neg = jnp.maximum(-lam, 0.0) + jnp.log(1.0 + jnp.exp(-jnp.abs(lam)))
    for j in range(D_RNN // GATE_COLS):
        cs = slice(j * GATE_COLS, (j + 1) * GATE_COLS)
        g = _dot(xcb[:, GATE_STARTS[j]:GATE_STARTS[j] + GATE_K], wg_ref[j]) + bg_ref[j]
        i_gate = _half_logistic(g[:, GATE_COLS:2 * GATE_COLS])
        m_sp = (-0.5 * LRU_C) * softplus_neg[:, cs]
        log_a = jnp.tanh(g[:, 0:GATE_COLS]) * m_sp + m_sp
        a = jnp.exp(log_a)
        z = -jnp.tanh(log_a) * (1.0 + a * a)
        bv = jnp.where(z > 0.0, z * lax.rsqrt(z), 0.0) * i_gate * xc[:, cs]
        for cc in range(GATE_COLS // LANES):
            c = j * (GATE_COLS // LANES) + cc
            a_ref[c] = a[:, cc * LANES:(cc + 1) * LANES]
            b_ref[c] = bv[:, cc * LANES:(cc + 1) * LANES]

    for bi in range(nb):
        r0 = bi * lt
        a_tot, h_tot = [], []
        for c in range(n_lb):
            at = a_ref[c, pl.ds(r0, n_g, stride=SCAN_S), :]
            ht = b_ref[c, pl.ds(r0, n_g, stride=SCAN_S), :]
            for k in range(1, SCAN_S):
                ak = a_ref[c, pl.ds(r0 + k, n_g, stride=SCAN_S), :]
                ht = ak * ht + b_ref[c, pl.ds(r0 + k, n_g, stride=SCAN_S), :]
                at = ak * at
            a_tot.append(at)
            h_tot.append(ht)
        carry = [hc_ref[bi][:, c * LANES:(c + 1) * LANES] for c in range(n_lb)]
        for g in range(n_g):
            for c in range(n_lb):
                hin_ref[c, g:g + 1, :] = carry[c]
                carry[c] = a_tot[c][g:g + 1, :] * carry[c] + h_tot[c][g:g + 1, :]
        h_last = jnp.concatenate(carry, axis=1)
        hc_ref[bi] = h_last
        hn_ref[bi] = h_last
        for c in range(n_lb):
            hcur = hin_ref[c]
            for k in range(SCAN_S):
                hcur = a_ref[c, pl.ds(r0 + k, n_g, stride=SCAN_S), :] * hcur + b_ref[c, pl.ds(r0 + k, n_g, stride=SCAN_S), :]
                b_ref[c, pl.ds(r0 + k, n_g, stride=SCAN_S), :] = hcur

    gelu_t = jnp.tanh(yr * (GELU_C0 + GELU_C1 * (yr * yr)))
    half_h = jnp.concatenate([b_ref[c] for c in range(n_lb)], axis=1) * (0.5 * yr)
    rnn = half_h * gelu_t + half_h

    merged2 = ((jnp.tanh(ga) + 1.0) * _dot(oat_ref[...], wao_ref[...])
               + (jnp.tanh(gr) + 1.0) * _dot(rnn.astype(_BF16), wro_ref[...]))
    mix = _dot(merged2.astype(_BF16), wout_ref[...])
    x1 = _layer_norm(DN_ALPHA * x + mix, g1_ref[...], b1_ref[...])
    for c in range(SLAB):
        x1s_ref[pl.ds(c, rows, stride=SLAB), :] = x1[:, c * LANES:(c + 1) * LANES]

    logits = _dot_nt(rwt_ref[...], x1.astype(_BF16)) + rb_ref[...]
    e_iota = lax.broadcasted_iota(jnp.int32, logits.shape, 0)
    work = logits
    top_vals = []
    top_idx = []
    top_sel = []
    for _ in range(TOP_K):
        m = jnp.max(work, axis=0, keepdims=True)
        idx = jnp.min(jnp.where(work == m, e_iota, N_EXPERTS), axis=0, keepdims=True)
        sel = e_iota == idx
        top_vals.append(m)
        top_idx.append(idx)
        top_sel.append(sel)
        work = jnp.where(sel, -jnp.inf, work)
    exps = [jnp.exp(v - top_vals[0]) for v in top_vals]
    denom = exps[0]
    for e in exps[1:]:
        denom = denom + e

    onehot = jnp.zeros_like(logits)
    for sel in top_sel:
        onehot = onehot + jnp.where(sel, 1.0, 0.0)
    r_i = lax.broadcasted_iota(jnp.int32, (rows, rows), 0)
    c_i = lax.broadcasted_iota(jnp.int32, (rows, rows), 1)
    earlier = jnp.where(r_i < c_i, 1.0, 0.0).astype(_BF16)
    before = _dot(onehot.astype(_BF16), earlier) + run_ref[...]
    run_ref[...] = run_ref[...] + jnp.sum(onehot, axis=1, keepdims=True)
    cnt_ref[...] = run_ref[...]

    sub_m = lax.broadcasted_iota(jnp.int32, (2 * TOP_K, rows), 0)
    meta_i = jnp.zeros((2 * TOP_K, rows), jnp.int32)
    meta_g = jnp.zeros((2 * TOP_K, rows), _F32)
    for k in range(TOP_K):
        rank_k = jnp.sum(jnp.where(top_sel[k], before, 0.0), axis=0, keepdims=True).astype(jnp.int32)
        meta_i = jnp.where(sub_m == k, top_idx[k], meta_i)
        meta_i = jnp.where(sub_m == TOP_K + k, rank_k, meta_i)
        meta_g = jnp.where(sub_m == k, exps[k] / denom, meta_g)
    meta_ref[...] = meta_i
    gate_ref[...] = meta_g


def _const_spec(shape):
    nd = len(shape)
    return pl.BlockSpec(shape, lambda b, s, _nd=nd: (0,) * _nd, pipeline_mode=pl.Buffered(1))


def _mixer(x, rope, k0, v0, c0, h0, sinks, prm, *, has_past, nb, lt, lq):
    bsz, seq, _ = x.shape
    assert bsz % nb == 0 and seq % lt == 0 and lt % lq == 0 and lq % 32 == 0
    assert lt >= WINDOW or seq == lt
    n_s = seq // lt
    rows = nb * lt
    n_tok = bsz * seq
    assert lt % SCAN_S == 0
    kernel = functools.partial(_mixer_kernel, has_past, nb, lt, lq)
    r_all = N_KV_HEADS * 2 * lq
    sink_rows = jnp.broadcast_to(sinks.reshape(N_KV_HEADS * 2, 1, 2, 1), (N_KV_HEADS * 2, lq, 2, HEAD_DIM))
    sink_rows = sink_rows.reshape(r_all, LANES)
    sink_ab = jnp.broadcast_to(sinks.reshape(N_KV_HEADS * 2, 1, 2, 1), (N_KV_HEADS * 2, lq, 2, LANES))
    sink_rows = jnp.stack([sink_rows, sink_ab[:, :, 0].reshape(r_all, LANES), sink_ab[:, :, 1].reshape(r_all, LANES)])
    n_gate = D_RNN // GATE_COLS
    batch_spec = lambda shape: pl.BlockSpec((nb,) + shape, lambda b, s: (b,) + (0,) * len(shape))
    in_specs = [
        pl.BlockSpec((nb, lt, D_MODEL), lambda b, s: (b, s, 0)),
        pl.BlockSpec((3, lt, LANES), lambda b, s: (0, s, 0)),
        _const_spec((D_MODEL, IN_DIM)),
        batch_spec((WINDOW, KV_DIM)), batch_spec((WINDOW, KV_DIM)),
        batch_spec((CONV_W - 1, D_RNN)), batch_spec((1, D_RNN)),
        _const_spec((3, r_all, LANES)),
        _const_spec((CONV_W, D_RNN)), _const_spec((1, D_RNN)),
        _const_spec((n_gate, GATE_K, 2 * GATE_COLS)), _const_spec((n_gate, 1, 2 * GATE_COLS)), _const_spec((1, D_RNN)),
        _const_spec((Q_DIM, D_MODEL)), _const_spec((D_RNN, D_MODEL)), _const_spec((D_MODEL, D_MODEL)),
        _const_spec((1, D_MODEL)), _const_spec((1, D_MODEL)),
        _const_spec((N_EXPERTS, D_MODEL)), _const_spec((N_EXPERTS, 1)),
    ]
    tok_map = lambda b, s: (b * n_s + s, 0)
    tok_map_t = lambda b, s: (0, b * n_s + s)
    out_shape = (
        jax.ShapeDtypeStruct((n_tok * SLAB, LANES), _F32),
        jax.ShapeDtypeStruct((2 * TOP_K, n_tok), jnp.int32),
        jax.ShapeDtypeStruct((2 * TOP_K, n_tok), _F32),
        jax.ShapeDtypeStruct((N_EXPERTS, 1), _F32),
        jax.ShapeDtypeStruct((bsz, WINDOW, KV_DIM), _F32),
        jax.ShapeDtypeStruct((bsz, WINDOW, KV_DIM), _F32),
        jax.ShapeDtypeStruct((bsz, CONV_W - 1, D_RNN), _F32),
        jax.ShapeDtypeStruct((bsz, 1, D_RNN), _F32),
    )
    out_specs = (
        pl.BlockSpec((rows * SLAB, LANES), tok_map),
        pl.BlockSpec((2 * TOP_K, rows), tok_map_t),
        pl.BlockSpec((2 * TOP_K, rows), tok_map_t),
        pl.BlockSpec((N_EXPERTS, 1), lambda b, s: (0, 0)),
        batch_spec((WINDOW, KV_DIM)), batch_spec((WINDOW, KV_DIM)),
        batch_spec((CONV_W - 1, D_RNN)), batch_spec((1, D_RNN)),
    )
    scratch = [
        pltpu.VMEM((nb, N_KV_HEADS, 2, WINDOW + lt, LANES), _BF16),
        pltpu.VMEM((nb, N_KV_HEADS, 2, WINDOW + lt, LANES), _BF16),
        pltpu.VMEM((nb, D_RNN // LANES, CONV_HDR + lt, LANES), _F32),
        pltpu.VMEM((D_RNN // LANES, rows, LANES), _F32),
        pltpu.VMEM((D_RNN // LANES, rows, LANES), _F32),
        pltpu.VMEM((D_RNN // LANES, rows, LANES), _F32),
        pltpu.VMEM((D_RNN // LANES, lt // SCAN_S, LANES), _F32),
        pltpu.VMEM((nb, 1, D_RNN), _F32),
        pltpu.VMEM((rows, Q_DIM), _BF16),
        pltpu.VMEM((N_EXPERTS, 1), _F32),
    ]
    args = [x, rope, prm["w_in"], k0, v0, c0, h0, sink_rows, prm["conv_w"], prm["conv_b"], prm["wg"], prm["bg"],
            prm["lam"], prm["wao"], prm["wro"], prm["wout"], prm["g1"], prm["b1"], prm["rwt"], prm["rb"]]
    return pl.pallas_call(
        kernel,
        grid=(bsz // nb, n_s),
        in_specs=in_specs,
        out_specs=out_specs,
        out_shape=out_shape,
        scratch_shapes=scratch,
        compiler_params=pltpu.CompilerParams(
            dimension_semantics=("arbitrary", "arbitrary"), vmem_limit_bytes=VMEM_LIMIT_BYTES),
        name="mixer_past" if has_past else "mixer_prompt",
    )(*args)


def _slab_rows(ref, first_tok, n):
    return jnp.concatenate([ref[pl.ds(first_tok * SLAB + c, n, stride=SLAB), :] for c in range(SLAB)], axis=1)


def _dispatch_kernel(tt, tm, n_first, n_tiles, zt_ref, nu_ref, pos_ref, xa_ref, xb_ref, xs_hbm, zero_ref, sem):
    i = pl.program_id(0)

    @pl.when(i == 0)
    def _zero_padding():
        zero_ref[...] = jnp.zeros_like(zero_ref)

        def clear(tile):
            dst = xs_hbm.at[pl.ds(pl.multiple_of(tile * (tm * SLAB), SLAB), tm * SLAB)]
            cp = pltpu.make_async_copy(zero_ref, dst, sem)
            cp.start()
            cp.wait()

        for e in range(N_EXPERTS):
            pl.when(zt_ref[e] >= 0)(functools.partial(clear, zt_ref[e]))

        def clear_unused(tile, carry):
            clear(tile)
            return carry

        lax.fori_loop(nu_ref[0], n_tiles, clear_unused, 0)

    def scatter(src_ref):
        def issue(r, carry):
            src = src_ref.at[pl.ds(pl.multiple_of(r * SLAB, SLAB), SLAB)]
            for k in range(TOP_K):
                dst = xs_hbm.at[pl.ds(pl.multiple_of(pos_ref[0, 0, r * TOP_K + k] * SLAB, SLAB), SLAB)]
                pltpu.make_async_copy(src, dst, sem).start(priority=k % 2)
            return carry

        lax.fori_loop(0, tt, issue, 0)

    pl.when(i < n_first)(lambda: scatter(xa_ref))
    pl.when(i >= n_first)(lambda: scatter(xb_ref))
    n_rows = tt * TOP_K * SLAB
    pltpu.make_async_copy(xs_hbm.at[pl.ds(0, n_rows)], xs_hbm.at[pl.ds(0, n_rows)], sem).wait()


def _dispatch(x1s_a, x1s_b, pos, zero_tiles, n_used, *, tt, tm, n_tiles):
    n_a = x1s_a.shape[0] // SLAB
    n_b = x1s_b.shape[0] // SLAB
    assert n_a % tt == 0 and n_b % tt == 0 and n_tiles * tm >= tt * TOP_K
    n_first = n_a // tt
    n_steps = (n_a + n_b) // tt
    return pl.pallas_call(
        functools.partial(_dispatch_kernel, tt, tm, n_first, n_tiles),
        grid_spec=pltpu.PrefetchScalarGridSpec(
            num_scalar_prefetch=2,
            grid=(n_steps,),
            in_specs=[
                pl.BlockSpec((1, 1, tt * TOP_K), lambda i, zt, nu: (i, 0, 0), memory_space=pltpu.SMEM),
                pl.BlockSpec((tt * SLAB, LANES), lambda i, zt, nu: (jnp.minimum(i, n_first - 1), 0)),
                pl.BlockSpec((tt * SLAB, LANES), lambda i, zt, nu: (jnp.maximum(i - n_first, 0), 0)),
            ],
            out_specs=pl.BlockSpec(memory_space=pl.ANY),
            scratch_shapes=[pltpu.VMEM((tm * SLAB, LANES), _F32), pltpu.SemaphoreType.DMA],
        ),
        out_shape=jax.ShapeDtypeStruct((n_tiles * tm * SLAB, LANES), _F32),
        compiler_params=pltpu.CompilerParams(dimension_semantics=("arbitrary",)),
        name="moe_dispatch",
    )(zero_tiles, n_used, pos.reshape(n_steps, 1, tt * TOP_K), x1s_a, x1s_b)


def _expert_kernel(tm, te_ref, nu_ref, xs_ref, wgu_ref, bgu_ref, wd_ref, bd_ref, ys_ref, wgu_b, wd_b):
    i = pl.program_id(0)

    @pl.when(i < nu_ref[0])
    def _tile():
        prev_e = te_ref[jnp.maximum(i - 1, 0)]

        @pl.when(jnp.logical_or(i == 0, te_ref[i] != prev_e))
        def _new_expert():
            wgu_b[...] = wgu_ref[0].astype(_BF16)
            wd_b[...] = wd_ref[0].astype(_BF16)

        x = _slab_rows(xs_ref, 0, tm).astype(_BF16)
        hgu = _dot(x, wgu_b[...]) + bgu_ref[0]
        glu = jnp.minimum(hgu[:, 0:D_FF], SWIGLU_LIMIT)
        lin = jnp.clip(hgu[:, D_FF:2 * D_FF], -SWIGLU_LIMIT, SWIGLU_LIMIT)
        hh = glu * _half_logistic((0.5 * SWIGLU_ALPHA) * glu) * (lin + 1.0)
        y = _dot(hh.astype(_BF16), wd_b[...]) + bd_ref[0]
        for c in range(SLAB):
            ys_ref[pl.ds(c, tm, stride=SLAB), :] = y[:, c * LANES:(c + 1) * LANES]

    @pl.when(i >= nu_ref[0])
    def _unused_tile():
        ys_ref[...] = jnp.zeros_like(ys_ref)


def _experts(xs, tile_expert, n_used, prm, *, tm):
    n_tiles = xs.shape[0] // (tm * SLAB)
    row_map = lambda i, te, nu: (jnp.minimum(i, nu[0] - 1), 0)
    out_map = lambda i, te, nu: (i, 0)
    exp_map = lambda i, te, nu: (te[i], 0, 0)
    return pl.pallas_call(
        functools.partial(_expert_kernel, tm),
        grid_spec=pltpu.PrefetchScalarGridSpec(
            num_scalar_prefetch=2,
            grid=(n_tiles,),
            in_specs=[
                pl.BlockSpec((tm * SLAB, LANES), row_map),
                pl.BlockSpec((1, D_MODEL, 2 * D_FF), exp_map),
                pl.BlockSpec((1, 1, 2 * D_FF), exp_map),
                pl.BlockSpec((1, D_FF, D_MODEL), exp_map),
                pl.BlockSpec((1, 1, D_MODEL), exp_map),
            ],
            out_specs=pl.BlockSpec((tm * SLAB, LANES), out_map),
            scratch_shapes=[pltpu.VMEM((D_MODEL, 2 * D_FF), _BF16), pltpu.VMEM((D_FF, D_MODEL), _BF16)],
        ),
        out_shape=jax.ShapeDtypeStruct(xs.shape, _F32),
        compiler_params=pltpu.CompilerParams(
            dimension_semantics=("arbitrary",), vmem_limit_bytes=VMEM_LIMIT_BYTES),
        name="moe_experts",
    )(tile_expert, n_used, xs, prm["wgu"], prm["bgu"], prm["wd"], prm["bd"])


def _combine_kernel(tt, n_first, n_steps, pos_ref, pos_next_ref, gate_ref, xa_ref, xb_ref, ys_hbm, g2_ref, b2_ref,
                    ya_ref, yb_ref, ybuf, sems):
    i = pl.program_id(0)
    n_rows = tt * TOP_K * SLAB

    def gather(p_ref, slot):
        def issue(r, carry):
            for k in range(TOP_K):
                src = ys_hbm.at[pl.ds(pl.multiple_of(p_ref[0, 0, r * TOP_K + k] * SLAB, SLAB), SLAB)]
                dst = ybuf.at[slot, pl.ds(pl.multiple_of((k * tt + r) * SLAB, SLAB), SLAB)]
                pltpu.make_async_copy(src, dst, sems.at[slot]).start(priority=k % 2)
            return carry

        lax.fori_loop(0, tt, issue, 0)

    pl.when(i == 0)(lambda: gather(pos_ref, 0))

    def tile(slot):
        pl.when(i + 1 < n_steps)(lambda: gather(pos_next_ref, 1 - slot))
        pltpu.make_async_copy(ys_hbm.at[pl.ds(0, n_rows)], ybuf.at[slot], sems.at[slot]).wait()
        rows_ref = ybuf.at[slot]
        gates = gate_ref[...]
        acc = gates[:, 0:1] * _slab_rows(rows_ref, 0, tt)
        for k in range(1, TOP_K):
            acc = acc + gates[:, k:k + 1] * _slab_rows(rows_ref, k * tt, tt)

        def finish(x_ref, y_ref):
            x1 = _slab_rows(x_ref, 0, tt)
            y_ref[...] = _layer_norm(DN_ALPHA * x1 + acc, g2_ref[...], b2_ref[...])

        pl.when(i < n_first)(lambda: finish(xa_ref, ya_ref))
        pl.when(i >= n_first)(lambda: finish(xb_ref, yb_ref))

    for slot in range(2):
        pl.when(i % 2 == slot)(functools.partial(tile, slot))


def _combine(x1s_a, x1s_b, ys, pos, gates, prm, *, tt):
    n_a = x1s_a.shape[0] // SLAB
    n_b = x1s_b.shape[0] // SLAB
    assert n_a % tt == 0 and n_b % tt == 0
    n_first = n_a // tt
    n_steps = (n_a + n_b) // tt
    a_map = lambda i: (jnp.minimum(i, n_first - 1), 0)
    b_map = lambda i: (jnp.maximum(i - n_first, 0), 0)
    pos3 = pos.reshape(n_steps, 1, tt * TOP_K)
    return pl.pallas_call(
        functools.partial(_combine_kernel, tt, n_first, n_steps),
        grid=(n_steps,),
        in_specs=[
            pl.BlockSpec((1, 1, tt * TOP_K), lambda i: (i, 0, 0), memory_space=pltpu.SMEM),
            pl.BlockSpec((1, 1, tt * TOP_K), lambda i: (jnp.minimum(i + 1, n_steps - 1), 0, 0),
                         memory_space=pltpu.SMEM),
            pl.BlockSpec((tt, TOP_K), lambda i: (i, 0)),
            pl.BlockSpec((tt * SLAB, LANES), a_map),
            pl.BlockSpec((tt * SLAB, LANES), b_map),
            pl.BlockSpec(memory_space=pl.ANY),
            pl.BlockSpec((1, D_MODEL), lambda i: (0, 0)),
            pl.BlockSpec((1, D_MODEL), lambda i: (0, 0)),
        ],
        out_specs=(pl.BlockSpec((tt, D_MODEL), a_map), pl.BlockSpec((tt, D_MODEL), b_map)),
        out_shape=(jax.ShapeDtypeStruct((n_a, D_MODEL), _F32), jax.ShapeDtypeStruct((n_b, D_MODEL), _F32)),
        scratch_shapes=[pltpu.VMEM((2, TOP_K * tt * SLAB, LANES), _F32), pltpu.SemaphoreType.DMA((2,))],
        compiler_params=pltpu.CompilerParams(
            dimension_semantics=("arbitrary",), vmem_limit_bytes=VMEM_LIMIT_BYTES),
        name="moe_combine",
    )(pos3, pos3, gates, x1s_a, x1s_b, ys, prm["g2"], prm["b2"])


def _route(meta_a, meta_b, gate_a, gate_b, counts_a, counts_b, tm, n_tiles):
    meta = jnp.concatenate([meta_a, meta_b], axis=1).T
    idx = meta[:, 0:TOP_K]
    rank = meta[:, TOP_K:2 * TOP_K]
    gates = jnp.concatenate([gate_a[0:TOP_K], gate_b[0:TOP_K]], axis=1).T
    cp = counts_a.reshape(-1).astype(jnp.int32)
    cs = counts_b.reshape(-1).astype(jnp.int32)
    is_b = (jnp.arange(meta.shape[0]) >= meta_a.shape[1])[:, None]
    is_e = idx[:, :, None] == jnp.arange(N_EXPERTS, dtype=idx.dtype)
    lookup = lambda table: jnp.sum(jnp.where(is_e, table, 0), axis=-1)
    rank = rank + jnp.where(is_b, lookup(cp), 0)
    tiles_per = (cp + cs + tm - 1) // tm
    tile_end = jnp.cumsum(tiles_per)
    tile_start = tile_end - tiles_per
    pos = lookup(tile_start) * tm + rank
    n_used = tile_end[-1:]
    tile_ids = jnp.arange(n_tiles, dtype=jnp.int32)
    tile_expert = jnp.minimum(jnp.sum(tile_ids[:, None] >= tile_end[None, :], axis=1), N_EXPERTS - 1)
    last_e = tile_expert[jnp.maximum(n_used[0] - 1, 0)]
    tile_expert = jnp.where(tile_ids < n_used[0], tile_expert, last_e).astype(jnp.int32)
    zero_tiles = jnp.where(tiles_per > 0, tile_end - 1, -1).astype(jnp.int32)
    return pos.astype(jnp.int32), gates, tile_expert, n_used.astype(jnp.int32), zero_tiles


def _rope_tables(pos):
    half = ROT_DIM // 2
    inv_freq = ROPE_THETA ** (-jnp.arange(half, dtype=_F32) / half)
    ang = pos.astype(_F32)[:, None] * inv_freq[None, :]
    cos, sin = jnp.cos(ang), jnp.sin(ang)
    n = pos.shape[0]
    ones = jnp.ones((n, HEAD_DIM - ROT_DIM), _F32)
    zeros = jnp.zeros((n, HEAD_DIM - ROT_DIM), _F32)
    zh = jnp.zeros((n, half), _F32)
    cos_t = jnp.concatenate([cos, cos, ones], axis=1)
    sin_up = jnp.concatenate([-sin, zh, zeros], axis=1)
    sin_dn = jnp.concatenate([zh, sin, zeros], axis=1)
    tab = jnp.stack([cos_t, sin_up, sin_dn])
    return jnp.concatenate([tab, tab], axis=2)


def _block_diag(w):
    hd, d, _ = w.shape
    eye = jnp.eye(hd, dtype=w.dtype)
    return (eye[:, None, :, None] * w[:, :, None, :]).reshape(hd * d, hd * d)


def _gate_blocks(wa, wx):
    blocks = []
    head_of = np.arange(D_RNN) // RNN_HEAD_DIM
    for j, k0 in enumerate(GATE_STARTS):
        cols = np.arange(j * GATE_COLS, (j + 1) * GATE_COLS)
        reach = np.flatnonzero(np.isin(head_of, head_of[cols]))
        assert reach.min() >= k0 and reach.max() < k0 + GATE_K
        cs = slice(j * GATE_COLS, (j + 1) * GATE_COLS)
        blocks.append(jnp.concatenate([wa[k0:k0 + GATE_K, cs], wx[k0:k0 + GATE_K, cs]], axis=1))
    return jnp.stack(blocks)


def kernel(x_prompt, x_sample, cache_k, cache_v, state_conv, state_h, w_in, attn_sinks, w_attn_out, conv_w, conv_b, gate_a_w, gate_a_b, gate_x_w, gate_x_b, lru_lambda, w_rnn_out, w_out, ln1_g, ln1_b, router_w, router_b, w_gate_up, b_gate_up, w_down, b_down, ln2_g, ln2_b):
    assert w_in.shape[0] == DEPTH == 1
    l = 0
    bsz, seq, _ = x_prompt.shape
    dbsz, dseq, _ = x_sample.shape
    row = lambda a: a.reshape(1, -1)
    prm = {
        "w_in": (w_in[l] * jnp.where(jnp.arange(IN_DIM) >= OFF_GA, 0.5, 1.0)).astype(_BF16),
        "conv_w": conv_w[l], "conv_b": row(conv_b[l]),
        "wg": (0.5 * _gate_blocks(_block_diag(gate_a_w[l]), _block_diag(gate_x_w[l]))).astype(_BF16),
        "bg": 0.5 * jnp.concatenate([gate_a_b[l].reshape(-1, 1, GATE_COLS),
                                     gate_x_b[l].reshape(-1, 1, GATE_COLS)], axis=2),
        "lam": row(lru_lambda[l]),
        "wao": w_attn_out[l].astype(_BF16), "wro": w_rnn_out[l].astype(_BF16), "wout": (0.5 * w_out[l]).astype(_BF16),
        "g1": row(ln1_g[l]), "b1": row(ln1_b[l]),
        "rwt": router_w[l].T.astype(_BF16), "rb": router_b[l].reshape(-1, 1),
        "wgu": w_gate_up[l], "bgu": b_gate_up[l][:, None, :],
        "wd": w_down[l], "bd": b_down[l][:, None, :],
        "g2": row(ln2_g[l]), "b2": row(ln2_b[l]),
    }
    sinks = attn_sinks[l]

    rope_p = _rope_tables(jnp.arange(seq, dtype=jnp.int32))
    rope_s = _rope_tables(PAST_LEN + jnp.arange(dseq, dtype=jnp.int32))

    n_prompt = bsz * seq
    n_tok = n_prompt + dbsz * dseq
    zeros_kv = jnp.zeros((bsz, WINDOW, KV_DIM), _F32)
    x1_p, meta_p, gate_p, cnt_p, pk, pv, pc, ph = _mixer(
        x_prompt, rope_p, zeros_kv, zeros_kv, jnp.zeros((bsz, CONV_W - 1, D_RNN), _F32),
        jnp.zeros((bsz, 1, D_RNN), _F32), sinks, prm, has_past=False, nb=1, lt=min(MIXER_ROWS, seq), lq=CHUNK)
    x1_s, meta_s, gate_s, cnt_s, sk, sv, sc, sh = _mixer(
        x_sample, rope_s, cache_k[l].reshape(dbsz, WINDOW, KV_DIM), cache_v[l].reshape(dbsz, WINDOW, KV_DIM),
        state_conv[l], state_h[l][:, None, :], sinks, prm, has_past=True, nb=min(PAST_ROWS // dseq, dbsz),
        lt=dseq, lq=dseq)

    tm = EXPERT_ROWS
    n_tiles = (n_tok * TOP_K + N_EXPERTS * (tm - 1)) // tm + 1
    pos, gates, tile_expert, n_used, zero_tiles = _route(meta_p, meta_s, gate_p, gate_s, cnt_p, cnt_s, tm, n_tiles)
    xs = _dispatch(x1_p, x1_s, pos, zero_tiles, n_used, tt=TOKEN_ROWS, tm=tm, n_tiles=n_tiles)
    ysort = _experts(xs, tile_expert, n_used, prm, tm=tm)
    yp, ys = _combine(x1_p, x1_s, ysort, pos, gates, prm, tt=TOKEN_ROWS)
    yp = yp.reshape(bsz, seq, D_MODEL)
    ys = ys.reshape(dbsz, dseq, D_MODEL)

    kv5 = lambda a: a.reshape(1, a.shape[0], WINDOW, N_KV_HEADS, HEAD_DIM)
    return (yp, ys, kv5(pk), kv5(pv), pc[None], ph.reshape(1, bsz, D_RNN),
            kv5(sk), kv5(sv), sc[None], sh.reshape(1, dbsz, D_RNN))
```

```python
import functools

import jax
import jax.numpy as jnp
import numpy as np
from jax import lax
from jax.experimental import pallas as pl
from jax.experimental.pallas import tpu as pltpu

D_MODEL = 1024
PAST_LEN = 1024
CHUNK = 64
N_HEADS = 16
N_KV_HEADS = 4
HEAD_DIM = 64
GROUP = N_HEADS // N_KV_HEADS
ROT_DIM = HEAD_DIM // 4
ROPE_THETA = 500000.0
WINDOW = 128
ATTN_SCALE = HEAD_DIM ** -0.5
NEG_INF = -1e30
D_RNN = 1280
RNN_HEADS = 16
RNN_HEAD_DIM = D_RNN // RNN_HEADS
CONV_W = 4
LRU_C = 8.0
N_EXPERTS = 32
TOP_K = 4
D_FF = 1024
SWIGLU_ALPHA = 1.702
SWIGLU_LIMIT = 7.0
LN_EPS = 1e-5
DEPTH = 1
DN_ALPHA = (2.0 * DEPTH) ** 0.25

Q_DIM = N_HEADS * HEAD_DIM
KV_DIM = N_KV_HEADS * HEAD_DIM
OFF_K = Q_DIM
OFF_V = OFF_K + KV_DIM
OFF_XR = OFF_V + KV_DIM
OFF_YR = OFF_XR + D_RNN
OFF_GA = OFF_YR + D_RNN
OFF_GR = OFF_GA + D_MODEL
IN_DIM = OFF_GR + D_MODEL

LANES = 128
SLAB = D_MODEL // LANES
MIXER_ROWS = 256
PAST_ROWS = 256
EXPERT_ROWS = 512
TOKEN_ROWS = 512
GELU_C0 = float(np.sqrt(2.0 / np.pi))
GELU_C1 = 0.044715 * GELU_C0
ATT_GROUP = 4
SCAN_S = 4
GATE_COLS = 256
GATE_K = 512
GATE_STARTS = tuple(min(max((j * GATE_COLS // RNN_HEAD_DIM) * RNN_HEAD_DIM // LANES * LANES, 0), D_RNN - GATE_K)
                    for j in range(D_RNN // GATE_COLS))
CONV_HDR = 8
VMEM_LIMIT_BYTES = 56 * 1024 * 1024

_BF16 = jnp.bfloat16
_F32 = jnp.float32


def _dot(a, b):
    return jnp.dot(a, b, preferred_element_type=_F32)


def _dot_nt(a, b):
    return lax.dot_general(a, b, (((1,), (1,)), ((), ())), preferred_element_type=_F32)


def _half_logistic(h):
    return 0.5 * jnp.tanh(h) + 0.5


def _layer_norm(x, g, b):
    mu = jnp.mean(x, axis=-1, keepdims=True)
    xc = x - mu
    var = jnp.mean(xc * xc, axis=-1, keepdims=True)
    return xc * lax.rsqrt(var + LN_EPS) * g + b


def _rope(x, cos_t, sin_up, sin_dn):
    return x * cos_t + pltpu.roll(x, LANES - ROT_DIM // 2, 1) * sin_up + pltpu.roll(x, ROT_DIM // 2, 1) * sin_dn


def _lo_hi(slab, h):
    blk = slab[:, (h // 2) * LANES:(h // 2 + 1) * LANES]
    lane = lax.broadcasted_iota(jnp.int32, blk.shape, 1)
    if h % 2 == 0:
        lo = jnp.where(lane < HEAD_DIM, blk, 0.0)
        hi = pltpu.roll(lo, HEAD_DIM, 1)
    else:
        hi = jnp.where(lane >= HEAD_DIM, blk, 0.0)
        lo = pltpu.roll(hi, HEAD_DIM, 1)
    return lo.astype(_BF16), hi.astype(_BF16)


def _mixer_kernel(has_past, nb, lt, lq,
                  x_ref, rope_ref, w_in_ref, k0_ref, v0_ref, c0_ref, h0_ref, sinkrows_ref,
                  convw_ref, convb_ref, wg_ref, bg_ref, lam_ref, wao_ref, wro_ref, wout_ref,
                  g1_ref, b1_ref, rwt_ref, rb_ref,
                  x1s_ref, meta_ref, gate_ref, cnt_ref, kn_ref, vn_ref, cn_ref, hn_ref,
                  kw_ref, vw_ref, xp_ref, xc_ref, a_ref, b_ref, hin_ref, hc_ref, oat_ref, run_ref):
    s_idx = pl.program_id(1)
    n_s = pl.num_programs(1)
    rows = nb * lt
    n_chunks = lt // lq

    @pl.when(jnp.logical_and(pl.program_id(0) == 0, s_idx == 0))
    def _init_counts():
        run_ref[...] = jnp.zeros_like(run_ref)

    @pl.when(s_idx == 0)
    def _init():
        for bi in range(nb):
            for h in range(N_KV_HEADS):
                klo, khi = _lo_hi(k0_ref[bi], h)
                vlo, vhi = _lo_hi(v0_ref[bi], h)
                kw_ref[bi, h, 0, 0:WINDOW, :] = klo
                kw_ref[bi, h, 1, 0:WINDOW, :] = khi
                vw_ref[bi, h, 0, 0:WINDOW, :] = vlo
                vw_ref[bi, h, 1, 0:WINDOW, :] = vhi
            for c in range(D_RNN // LANES):
                xp_ref[bi, c, CONV_HDR - (CONV_W - 1):CONV_HDR, :] = c0_ref[bi, :, c * LANES:(c + 1) * LANES]
            hc_ref[bi] = h0_ref[bi]

    x = x_ref[...].reshape(rows, D_MODEL)
    xb = x.astype(_BF16)

    cos_t = jnp.concatenate([rope_ref[0]] * nb, axis=0) if nb > 1 else rope_ref[0]
    sin_up = jnp.concatenate([rope_ref[1]] * nb, axis=0) if nb > 1 else rope_ref[1]
    sin_dn = jnp.concatenate([rope_ref[2]] * nb, axis=0) if nb > 1 else rope_ref[2]

    zq = _dot(xb, w_in_ref[:, 0:Q_DIM])
    q_blocks = []
    for c in range(Q_DIM // LANES):
        qr = _rope(zq[:, c * LANES:(c + 1) * LANES], cos_t, sin_up, sin_dn)
        q_blocks.append((qr * ATTN_SCALE).astype(_BF16))
    zk = _dot(xb, w_in_ref[:, OFF_K:OFF_V])
    k_rot = jnp.concatenate(
        [_rope(zk[:, c * LANES:(c + 1) * LANES], cos_t, sin_up, sin_dn) for c in range(KV_DIM // LANES)], axis=1)
    v_new = _dot(xb, w_in_ref[:, OFF_V:OFF_XR])

    for bi in range(nb):
        r0 = bi * lt
        for h in range(N_KV_HEADS):
            klo, khi = _lo_hi(k_rot[r0:r0 + lt], h)
            vlo, vhi = _lo_hi(v_new[r0:r0 + lt], h)
            kw_ref[bi, h, 0, WINDOW:WINDOW + lt, :] = klo
            kw_ref[bi, h, 1, WINDOW:WINDOW + lt, :] = khi
            vw_ref[bi, h, 0, WINDOW:WINDOW + lt, :] = vlo
            vw_ref[bi, h, 1, WINDOW:WINDOW + lt, :] = vhi

    for bi in range(nb):
        r0 = bi * lt
        if lt >= WINDOW:
            kn_ref[bi] = k_rot[r0 + lt - WINDOW:r0 + lt]
            vn_ref[bi] = v_new[r0 + lt - WINDOW:r0 + lt]
        else:
            kn_ref[bi, 0:WINDOW - lt, :] = k0_ref[bi, lt:WINDOW, :]
            kn_ref[bi, WINDOW - lt:WINDOW, :] = k_rot[r0:r0 + lt]
            vn_ref[bi, 0:WINDOW - lt, :] = v0_ref[bi, lt:WINDOW, :]
            vn_ref[bi, WINDOW - lt:WINDOW, :] = v_new[r0:r0 + lt]

    w2 = 2 * lq
    wk = 2 * WINDOW + w2
    r_all = N_KV_HEADS * 2 * lq
    is_a2 = lax.broadcasted_iota(jnp.int32, (r_all, w2), 1) < lq
    lane_lo = lax.broadcasted_iota(jnp.int32, (r_all, LANES), 1) < HEAD_DIM
    key_row = lax.broadcasted_iota(jnp.int32, (wk, LANES), 0)
    key_is_a = jnp.logical_or(key_row < WINDOW, jnp.logical_and(key_row >= 2 * WINDOW, key_row < 2 * WINDOW + lq))
    key_lane_lo = lax.broadcasted_iota(jnp.int32, (wk, LANES), 1) < HEAD_DIM
    head_ones = jnp.where(key_is_a == key_lane_lo, 1.0, 0.0).astype(_BF16)
    sink_rows, sink_a_rows, sink_b_rows = sinkrows_ref[0], sinkrows_ref[1], sinkrows_ref[2]
    first = s_idx == 0
    units = [(bi, j) for bi in range(nb) for j in range(n_chunks)]
    for g0 in range(0, len(units), ATT_GROUP):
        group = units[g0:g0 + ATT_GROUP]
        scores = []
        vmats = []
        for bi, j in group:
            p0 = j * lq
            o0 = WINDOW + j * lq
            q0 = bi * lt + j * lq
            for h in range(N_KV_HEADS):
                kmat = jnp.concatenate([kw_ref[bi, h, 0, p0:p0 + WINDOW, :], kw_ref[bi, h, 1, p0:p0 + WINDOW, :],
                                        kw_ref[bi, h, 0, o0:o0 + lq, :], kw_ref[bi, h, 1, o0:o0 + lq, :]], axis=0)
                vmat = jnp.concatenate([vw_ref[bi, h, 0, p0:p0 + WINDOW, :], vw_ref[bi, h, 1, p0:p0 + WINDOW, :],
                                        vw_ref[bi, h, 0, o0:o0 + lq, :], vw_ref[bi, h, 1, o0:o0 + lq, :]], axis=0)
                vmats.append(jnp.concatenate([vmat, head_ones], axis=1))
                qs = jnp.concatenate([q_blocks[2 * h][q0:q0 + lq], q_blocks[2 * h + 1][q0:q0 + lq]], axis=0)
                sc_h = _dot_nt(qs, kmat)
                if not has_past and j * lq < WINDOW:
                    col = lax.broadcasted_iota(jnp.int32, sc_h.shape, 1)
                    n_bad = WINDOW - j * lq
                    bad = jnp.logical_or(col < n_bad, jnp.logical_and(col >= WINDOW, col < WINDOW + n_bad))
                    sc_h = jnp.where(jnp.logical_and(first, bad), NEG_INF, sc_h)
                scores.append(sc_h)
        n_u = len(group)
        sc = jnp.concatenate(scores, axis=0)
        if g0 == 0:
            ga = _dot(xb, w_in_ref[:, OFF_GA:OFF_GR])
            gr = _dot(xb, w_in_ref[:, OFF_GR:IN_DIM])
        tile_rows = lambda a: jnp.concatenate([a] * n_u, axis=0) if n_u > 1 else a
        is_a2g, lane_log, sinks_g = tile_rows(is_a2), tile_rows(lane_lo), tile_rows(sink_rows)
        sinks_ag, sinks_bg = tile_rows(sink_a_rows), tile_rows(sink_b_rows)
        c0 = sc[:, 0:LANES]
        c1 = sc[:, LANES:2 * LANES]
        c2 = sc[:, 2 * LANES:wk]
        if w2 == LANES:
            ma = jnp.max(jnp.maximum(c0, jnp.where(is_a2g, c2, NEG_INF)).astype(_BF16), axis=1, keepdims=True)
            mb = jnp.max(jnp.maximum(c1, jnp.where(is_a2g, NEG_INF, c2)).astype(_BF16), axis=1, keepdims=True)
        else:
            ma = jnp.maximum(jnp.max(c0.astype(_BF16), axis=1, keepdims=True),
                             jnp.max(jnp.where(is_a2g, c2, NEG_INF).astype(_BF16), axis=1, keepdims=True))
            mb = jnp.maximum(jnp.max(c1.astype(_BF16), axis=1, keepdims=True),
                             jnp.max(jnp.where(is_a2g, NEG_INF, c2).astype(_BF16), axis=1, keepdims=True))
        ma = jnp.maximum(jnp.broadcast_to(ma.astype(_F32), sinks_ag.shape), sinks_ag)
        mb = jnp.maximum(jnp.broadcast_to(mb.astype(_F32), sinks_bg.shape), sinks_bg)
        probs = jnp.concatenate([jnp.exp(c0 - ma), jnp.exp(c1 - mb),
                                 jnp.exp(c2 - jnp.where(is_a2g, ma[:, 0:w2], mb[:, 0:w2]))], axis=1).astype(_BF16)
        sink_term = jnp.exp(sinks_g - jnp.where(lane_log, ma, mb))
        for u, (bi, j) in enumerate(group):
            q0 = bi * lt + j * lq
            for h in range(N_KV_HEADS):
                r0 = (u * N_KV_HEADS + h) * 2 * lq
                od = _dot(probs[r0:r0 + 2 * lq], vmats[u * N_KV_HEADS + h])
                out = od[:, 0:LANES] / (od[:, LANES:2 * LANES] + sink_term[r0:r0 + 2 * lq])
                for p in range(2):
                    oat_ref[q0:q0 + lq, (2 * h + p) * LANES:(2 * h + p + 1) * LANES] = (
                        out[p * lq:(p + 1) * lq].astype(_BF16))

    if lt >= WINDOW:
        @pl.when(s_idx + 1 < n_s)
        def _carry_kv():
            for bi in range(nb):
                for h in range(N_KV_HEADS):
                    for v in range(2):
                        kw_ref[bi, h, v, 0:WINDOW, :] = kw_ref[bi, h, v, lt:lt + WINDOW, :]
                        vw_ref[bi, h, v, 0:WINDOW, :] = vw_ref[bi, h, v, lt:lt + WINDOW, :]

    xr = _dot(xb, w_in_ref[:, OFF_XR:OFF_YR])
    n_lb = D_RNN // LANES
    n_g = lt // SCAN_S
    conv_w = convw_ref[...]
    conv_b = convb_ref[...]
    for bi in range(nb):
        r0 = bi * lt
        tail = xr[r0 + lt - (CONV_W - 1):r0 + lt]
        cn_ref[bi] = tail
        for c in range(n_lb):
            ls = slice(c * LANES, (c + 1) * LANES)
            xp_ref[bi, c, CONV_HDR:CONV_HDR + lt, :] = xr[r0:r0 + lt, ls]
            frames = [xp_ref[bi, c, pl.ds(CONV_HDR + m, n_g, stride=SCAN_S), :]
                      for m in range(-(CONV_W - 1), SCAN_S)]
            for k in range(SCAN_S):
                acc = frames[k] * conv_w[0:1, ls]
                for t in range(1, CONV_W):
                    acc = acc + frames[k + t] * conv_w[t:t + 1, ls]
                xc_ref[c, pl.ds(r0 + k, n_g, stride=SCAN_S), :] = acc + conv_b[:, ls]
            xp_ref[bi, c, CONV_HDR - (CONV_W - 1):CONV_HDR, :] = tail[:, ls]

    yr = _dot(xb, w_in_ref[:, OFF_YR:OFF_GA])
    xc = jnp.concatenate([xc_ref[c] for c in range(n_lb)], axis=1)
    xcb = xc.astype(_BF16)
    lam = lam_ref[...]
    softplus_neg = jnp.maximum(-lam, 0.0) + jnp.log(1.0 + jnp.exp(-jnp.abs(lam)))
    for j in range(D_RNN // GATE_COLS):
        cs = slice(j * GATE_COLS, (j + 1) * GATE_COLS)
        g = _dot(xcb[:, GATE_STARTS[j]:GATE_STARTS[j] + GATE_K], wg_ref[j]) + bg_ref[j]
        i_gate = _half_logistic(g[:, GATE_COLS:2 * GATE_COLS])
        m_sp = (-0.5 * LRU_C) * softplus_neg[:, cs]
        log_a = jnp.tanh(g[:, 0:GATE_COLS]) * m_sp + m_sp
        a = jnp.exp(log_a)
        z = -jnp.tanh(log_a) * (1.0 + a * a)
        bv = jnp.where(z > 0.0, z * lax.rsqrt(z), 0.0) * i_gate * xc[:, cs]
        for cc in range(GATE_COLS // LANES):
            c = j * (GATE_COLS // LANES) + cc
            a_ref[c] = a[:, cc * LANES:(cc + 1) * LANES]
            b_ref[c] = bv[:, cc * LANES:(cc + 1) * LANES]

    for bi in range(nb):
        r0 = bi * lt
        a_tot, h_tot = [], []
        for c in range(n_lb):
            at = a_ref[c, pl.ds(r0, n_g, stride=SCAN_S), :]
            ht = b_ref[c, pl.ds(r0, n_g, stride=SCAN_S), :]
            for k in range(1, SCAN_S):
                ak = a_ref[c, pl.ds(r0 + k, n_g, stride=SCAN_S), :]
                ht = ak * ht + b_ref[c, pl.ds(r0 + k, n_g, stride=SCAN_S), :]
                at = ak * at
            a_tot.append(at)
            h_tot.append(ht)
        carry = [hc_ref[bi][:, c * LANES:(c + 1) * LANES] for c in range(n_lb)]
        for g in range(n_g):
            for c in range(n_lb):
                hin_ref[c, g:g + 1, :] = carry[c]
                carry[c] = a_tot[c][g:g + 1, :] * carry[c] + h_tot[c][g:g + 1, :]
        h_last = jnp.concatenate(carry, axis=1)
        hc_ref[bi] = h_last
        hn_ref[bi] = h_last
        for c in range(n_lb):
            hcur = hin_ref[c]
            for k in range(SCAN_S):
                hcur = a_ref[c, pl.ds(r0 + k, n_g, stride=SCAN_S), :] * hcur + b_ref[c, pl.ds(r0 + k, n_g, stride=SCAN_S), :]
                b_ref[c, pl.ds(r0 + k, n_g, stride=SCAN_S), :] = hcur

    gelu_t = jnp.tanh(yr * (GELU_C0 + GELU_C1 * (yr * yr)))
    half_h = jnp.concatenate([b_ref[c] for c in range(n_lb)], axis=1) * (0.5 * yr)
    rnn = half_h * gelu_t + half_h

    merged2 = ((jnp.tanh(ga) + 1.0) * _dot(oat_ref[...], wao_ref[...])
               + (jnp.tanh(gr) + 1.0) * _dot(rnn.astype(_BF16), wro_ref[...]))
    mix = _dot(merged2.astype(_BF16), wout_ref[...])
    x1 = _layer_norm(DN_ALPHA * x + mix, g1_ref[...], b1_ref[...])
    for c in range(SLAB):
        x1s_ref[pl.ds(c, rows, stride=SLAB), :] = x1[:, c * LANES:(c + 1) * LANES]

    logits = _dot_nt(rwt_ref[...], x1.astype(_BF16)) + rb_ref[...]
    e_iota = lax.broadcasted_iota(jnp.int32, logits.shape, 0)
    work = logits
    top_vals = []
    top_idx = []
    top_sel = []
    for _ in range(TOP_K):
        m = jnp.max(work, axis=0, keepdims=True)
        idx = jnp.min(jnp.where(work == m, e_iota, N_EXPERTS), axis=0, keepdims=True)
        sel = e_iota == idx
        top_vals.append(m)
        top_idx.append(idx)
        top_sel.append(sel)
        work = jnp.where(sel, -jnp.inf, work)
    exps = [jnp.exp(v - top_vals[0]) for v in top_vals]
    denom = exps[0]
    for e in exps[1:]:
        denom = denom + e

    onehot = jnp.zeros_like(logits)
    for sel in top_sel:
        onehot = onehot + jnp.where(sel, 1.0, 0.0)
    r_i = lax.broadcasted_iota(jnp.int32, (rows, rows), 0)
    c_i = lax.broadcasted_iota(jnp.int32, (rows, rows), 1)
    earlier = jnp.where(r_i < c_i, 1.0, 0.0).astype(_BF16)
    before = _dot(onehot.astype(_BF16), earlier) + run_ref[...]
    run_ref[...] = run_ref[...] + jnp.sum(onehot, axis=1, keepdims=True)
    cnt_ref[...] = run_ref[...]

    sub_m = lax.broadcasted_iota(jnp.int32, (2 * TOP_K, rows), 0)
    meta_i = jnp.zeros((2 * TOP_K, rows), jnp.int32)
    meta_g = jnp.zeros((2 * TOP_K, rows), _F32)
    for k in range(TOP_K):
        rank_k = jnp.sum(jnp.where(top_sel[k], before, 0.0), axis=0, keepdims=True).astype(jnp.int32)
        meta_i = jnp.where(sub_m == k, top_idx[k], meta_i)
        meta_i = jnp.where(sub_m == TOP_K + k, rank_k, meta_i)
        meta_g = jnp.where(sub_m == k, exps[k] / denom, meta_g)
    meta_ref[...] = meta_i
    gate_ref[...] = meta_g


def _const_spec(shape):
    nd = len(shape)
    return pl.BlockSpec(shape, lambda b, s, _nd=nd: (0,) * _nd, pipeline_mode=pl.Buffered(1))


def _mixer(x, rope, k0, v0, c0, h0, sinks, prm, *, has_past, nb, lt, lq):
    bsz, seq, _ = x.shape
    assert bsz % nb == 0 and seq % lt == 0 and lt % lq == 0 and lq % 32 == 0
    assert lt >= WINDOW or seq == lt
    n_s = seq // lt
    rows = nb * lt
    n_tok = bsz * seq
    assert lt % SCAN_S == 0
    kernel = functools.partial(_mixer_kernel, has_past, nb, lt, lq)
    r_all = N_KV_HEADS * 2 * lq
    sink_rows = jnp.broadcast_to(sinks.reshape(N_KV_HEADS * 2, 1, 2, 1), (N_KV_HEADS * 2, lq, 2, HEAD_DIM))
    sink_rows = sink_rows.reshape(r_all, LANES)
    sink_ab = jnp.broadcast_to(sinks.reshape(N_KV_HEADS * 2, 1, 2, 1), (N_KV_HEADS * 2, lq, 2, LANES))
    sink_rows = jnp.stack([sink_rows, sink_ab[:, :, 0].reshape(r_all, LANES), sink_ab[:, :, 1].reshape(r_all, LANES)])
    n_gate = D_RNN // GATE_COLS
    batch_spec = lambda shape: pl.BlockSpec((nb,) + shape, lambda b, s: (b,) + (0,) * len(shape))
    in_specs = [
        pl.BlockSpec((nb, lt, D_MODEL), lambda b, s: (b, s, 0)),
        pl.BlockSpec((3, lt, LANES), lambda b, s: (0, s, 0)),
        _const_spec((D_MODEL, IN_DIM)),
        batch_spec((WINDOW, KV_DIM)), batch_spec((WINDOW, KV_DIM)),
        batch_spec((CONV_W - 1, D_RNN)), batch_spec((1, D_RNN)),
        _const_spec((3, r_all, LANES)),
        _const_spec((CONV_W, D_RNN)), _const_spec((1, D_RNN)),
        _const_spec((n_gate, GATE_K, 2 * GATE_COLS)), _const_spec((n_gate, 1, 2 * GATE_COLS)), _const_spec((1, D_RNN)),
        _const_spec((Q_DIM, D_MODEL)), _const_spec((D_RNN, D_MODEL)), _const_spec((D_MODEL, D_MODEL)),
        _const_spec((1, D_MODEL)), _const_spec((1, D_MODEL)),
        _const_spec((N_EXPERTS, D_MODEL)), _const_spec((N_EXPERTS, 1)),
    ]
    tok_map = lambda b, s: (b * n_s + s, 0)
    tok_map_t = lambda b, s: (0, b * n_s + s)
    out_shape = (
        jax.ShapeDtypeStruct((n_tok * SLAB, LANES), _F32),
        jax.ShapeDtypeStruct((2 * TOP_K, n_tok), jnp.int32),
        jax.ShapeDtypeStruct((2 * TOP_K, n_tok), _F32),
        jax.ShapeDtypeStruct((N_EXPERTS, 1), _F32),
        jax.ShapeDtypeStruct((bsz, WINDOW, KV_DIM), _F32),
        jax.ShapeDtypeStruct((bsz, WINDOW, KV_DIM), _F32),
        jax.ShapeDtypeStruct((bsz, CONV_W - 1, D_RNN), _F32),
        jax.ShapeDtypeStruct((bsz, 1, D_RNN), _F32),
    )
    out_specs = (
        pl.BlockSpec((rows * SLAB, LANES), tok_map),
        pl.BlockSpec((2 * TOP_K, rows), tok_map_t),
        pl.BlockSpec((2 * TOP_K, rows), tok_map_t),
        pl.BlockSpec((N_EXPERTS, 1), lambda b, s: (0, 0)),
        batch_spec((WINDOW, KV_DIM)), batch_spec((WINDOW, KV_DIM)),
        batch_spec((CONV_W - 1, D_RNN)), batch_spec((1, D_RNN)),
    )
    scratch = [
        pltpu.VMEM((nb, N_KV_HEADS, 2, WINDOW + lt, LANES), _BF16),
        pltpu.VMEM((nb, N_KV_HEADS, 2, WINDOW + lt, LANES), _BF16),
        pltpu.VMEM((nb, D_RNN // LANES, CONV_HDR + lt, LANES), _F32),
        pltpu.VMEM((D_RNN // LANES, rows, LANES), _F32),
        pltpu.VMEM((D_RNN // LANES, rows, LANES), _F32),
        pltpu.VMEM((D_RNN // LANES, rows, LANES), _F32),
        pltpu.VMEM((D_RNN // LANES, lt // SCAN_S, LANES), _F32),
        pltpu.VMEM((nb, 1, D_RNN), _F32),
        pltpu.VMEM((rows, Q_DIM), _BF16),
        pltpu.VMEM((N_EXPERTS, 1), _F32),
    ]
    args = [x, rope, prm["w_in"], k0, v0, c0, h0, sink_rows, prm["conv_w"], prm["conv_b"], prm["wg"], prm["bg"],
            prm["lam"], prm["wao"], prm["wro"], prm["wout"], prm["g1"], prm["b1"], prm["rwt"], prm["rb"]]
    return pl.pallas_call(
        kernel,
        grid=(bsz // nb, n_s),
        in_specs=in_specs,
        out_specs=out_specs,
        out_shape=out_shape,
        scratch_shapes=scratch,
        compiler_params=pltpu.CompilerParams(
            dimension_semantics=("arbitrary", "arbitrary"), vmem_limit_bytes=VMEM_LIMIT_BYTES),
        name="mixer_past" if has_past else "mixer_prompt",
    )(*args)


def _slab_rows(ref, first_tok, n):
    return jnp.concatenate([ref[pl.ds(first_tok * SLAB + c, n, stride=SLAB), :] for c in range(SLAB)], axis=1)


def _pos_tiles(pos, n_steps, tt):
    return pos.reshape(TOP_K, n_steps, tt).transpose(1, 0, 2).reshape(n_steps, 1, TOP_K * tt)


def _dispatch_kernel(tt, tm, n_first, n_tiles, zt_ref, nu_ref, pos_ref, xa_ref, xb_ref, xs_hbm, zero_ref, sem):
    i = pl.program_id(0)

    @pl.when(i == 0)
    def _zero_padding():
        zero_ref[...] = jnp.zeros_like(zero_ref)

        def clear(tile):
            dst = xs_hbm.at[pl.ds(pl.multiple_of(tile * (tm * SLAB), SLAB), tm * SLAB)]
            cp = pltpu.make_async_copy(zero_ref, dst, sem)
            cp.start()
            cp.wait()

        for e in range(N_EXPERTS):
            pl.when(zt_ref[e] >= 0)(functools.partial(clear, zt_ref[e]))

        def clear_unused(tile, carry):
            clear(tile)
            return carry

        lax.fori_loop(nu_ref[0], n_tiles, clear_unused, 0)

    def scatter(src_ref):
        def issue(r, carry):
            src = src_ref.at[pl.ds(pl.multiple_of(r * SLAB, SLAB), SLAB)]
            for k in range(TOP_K):
                dst = xs_hbm.at[pl.ds(pl.multiple_of(pos_ref[0, 0, k * tt + r] * SLAB, SLAB), SLAB)]
                pltpu.make_async_copy(src, dst, sem).start(priority=k % 2)
            return carry

        lax.fori_loop(0, tt, issue, 0)

    pl.when(i < n_first)(lambda: scatter(xa_ref))
    pl.when(i >= n_first)(lambda: scatter(xb_ref))
    n_rows = tt * TOP_K * SLAB
    pltpu.make_async_copy(xs_hbm.at[pl.ds(0, n_rows)], xs_hbm.at[pl.ds(0, n_rows)], sem).wait()


def _dispatch(x1s_a, x1s_b, pos, zero_tiles, n_used, *, tt, tm, n_tiles):
    n_a = x1s_a.shape[0] // SLAB
    n_b = x1s_b.shape[0] // SLAB
    assert n_a % tt == 0 and n_b % tt == 0 and n_tiles * tm >= tt * TOP_K
    n_first = n_a // tt
    n_steps = (n_a + n_b) // tt
    return pl.pallas_call(
        functools.partial(_dispatch_kernel, tt, tm, n_first, n_tiles),
        grid_spec=pltpu.PrefetchScalarGridSpec(
            num_scalar_prefetch=2,
            grid=(n_steps,),
            in_specs=[
                pl.BlockSpec((1, 1, tt * TOP_K), lambda i, zt, nu: (i, 0, 0), memory_space=pltpu.SMEM),
                pl.BlockSpec((tt * SLAB, LANES), lambda i, zt, nu: (jnp.minimum(i, n_first - 1), 0)),
                pl.BlockSpec((tt * SLAB, LANES), lambda i, zt, nu: (jnp.maximum(i - n_first, 0), 0)),
            ],
            out_specs=pl.BlockSpec(memory_space=pl.ANY),
            scratch_shapes=[pltpu.VMEM((tm * SLAB, LANES), _F32), pltpu.SemaphoreType.DMA],
        ),
        out_shape=jax.ShapeDtypeStruct((n_tiles * tm * SLAB, LANES), _F32),
        compiler_params=pltpu.CompilerParams(dimension_semantics=("arbitrary",)),
        name="moe_dispatch",
    )(zero_tiles, n_used, _pos_tiles(pos, n_steps, tt), x1s_a, x1s_b)


def _expert_kernel(tm, te_ref, nu_ref, xs_ref, wgu_ref, bgu_ref, wd_ref, bd_ref, ys_ref, wgu_b, wd_b):
    i = pl.program_id(0)

    @pl.when(i < nu_ref[0])
    def _tile():
        prev_e = te_ref[jnp.maximum(i - 1, 0)]

        @pl.when(jnp.logical_or(i == 0, te_ref[i] != prev_e))
        def _new_expert():
            wgu_b[...] = wgu_ref[0].astype(_BF16)
            wd_b[...] = wd_ref[0].astype(_BF16)

        x = _slab_rows(xs_ref, 0, tm).astype(_BF16)
        hgu = _dot(x, wgu_b[...]) + bgu_ref[0]
        glu = jnp.minimum(hgu[:, 0:D_FF], SWIGLU_LIMIT)
        lin = jnp.clip(hgu[:, D_FF:2 * D_FF], -SWIGLU_LIMIT, SWIGLU_LIMIT)
        hh = glu * _half_logistic((0.5 * SWIGLU_ALPHA) * glu) * (lin + 1.0)
        y = _dot(hh.astype(_BF16), wd_b[...]) + bd_ref[0]
        for c in range(SLAB):
            ys_ref[pl.ds(c, tm, stride=SLAB), :] = y[:, c * LANES:(c + 1) * LANES]

    @pl.when(i >= nu_ref[0])
    def _unused_tile():
        ys_ref[...] = jnp.zeros_like(ys_ref)


def _experts(xs, tile_expert, n_used, prm, *, tm):
    n_tiles = xs.shape[0] // (tm * SLAB)
    row_map = lambda i, te, nu: (jnp.minimum(i, nu[0] - 1), 0)
    out_map = lambda i, te, nu: (i, 0)
    exp_map = lambda i, te, nu: (te[i], 0, 0)
    return pl.pallas_call(
        functools.partial(_expert_kernel, tm),
        grid_spec=pltpu.PrefetchScalarGridSpec(
            num_scalar_prefetch=2,
            grid=(n_tiles,),
            in_specs=[
                pl.BlockSpec((tm * SLAB, LANES), row_map),
                pl.BlockSpec((1, D_MODEL, 2 * D_FF), exp_map),
                pl.BlockSpec((1, 1, 2 * D_FF), exp_map),
                pl.BlockSpec((1, D_FF, D_MODEL), exp_map),
                pl.BlockSpec((1, 1, D_MODEL), exp_map),
            ],
            out_specs=pl.BlockSpec((tm * SLAB, LANES), out_map),
            scratch_shapes=[pltpu.VMEM((D_MODEL, 2 * D_FF), _BF16), pltpu.VMEM((D_FF, D_MODEL), _BF16)],
        ),
        out_shape=jax.ShapeDtypeStruct(xs.shape, _F32),
        compiler_params=pltpu.CompilerParams(
            dimension_semantics=("arbitrary",), vmem_limit_bytes=VMEM_LIMIT_BYTES),
        name="moe_experts",
    )(tile_expert, n_used, xs, prm["wgu"], prm["bgu"], prm["wd"], prm["bd"])


def _combine_kernel(tt, n_first, n_steps, pos_ref, pos_next_ref, gate_ref, xa_ref, xb_ref, ys_hbm, g2_ref, b2_ref,
                    ya_ref, yb_ref, ybuf, sems):
    i = pl.program_id(0)
    n_rows = tt * TOP_K * SLAB

    def gather(p_ref, slot):
        def issue(r, carry):
            for k in range(TOP_K):
                src = ys_hbm.at[pl.ds(pl.multiple_of(p_ref[0, 0, k * tt + r] * SLAB, SLAB), SLAB)]
                dst = ybuf.at[slot, pl.ds(pl.multiple_of((k * tt + r) * SLAB, SLAB), SLAB)]
                pltpu.make_async_copy(src, dst, sems.at[slot]).start(priority=k % 2)
            return carry

        lax.fori_loop(0, tt, issue, 0)

    pl.when(i == 0)(lambda: gather(pos_ref, 0))

    def tile(slot):
        pl.when(i + 1 < n_steps)(lambda: gather(pos_next_ref, 1 - slot))
        pltpu.make_async_copy(ys_hbm.at[pl.ds(0, n_rows)], ybuf.at[slot], sems.at[slot]).wait()
        rows_ref = ybuf.at[slot]
        gates = gate_ref[...]
        acc = gates[:, 0:1] * _slab_rows(rows_ref, 0, tt)
        for k in range(1, TOP_K):
            acc = acc + gates[:, k:k + 1] * _slab_rows(rows_ref, k * tt, tt)

        def finish(x_ref, y_ref):
            x1 = _slab_rows(x_ref, 0, tt)
            y_ref[...] = _layer_norm(DN_ALPHA * x1 + acc, g2_ref[...], b2_ref[...])

        pl.when(i < n_first)(lambda: finish(xa_ref, ya_ref))
        pl.when(i >= n_first)(lambda: finish(xb_ref, yb_ref))

    for slot in range(2):
        pl.when(i % 2 == slot)(functools.partial(tile, slot))


def _combine(x1s_a, x1s_b, ys, pos, gates, prm, *, tt):
    n_a = x1s_a.shape[0] // SLAB
    n_b = x1s_b.shape[0] // SLAB
    assert n_a % tt == 0 and n_b % tt == 0
    n_first = n_a // tt
    n_steps = (n_a + n_b) // tt
    a_map = lambda i: (jnp.minimum(i, n_first - 1), 0)
    b_map = lambda i: (jnp.maximum(i - n_first, 0), 0)
    pos3 = _pos_tiles(pos, n_steps, tt)
    return pl.pallas_call(
        functools.partial(_combine_kernel, tt, n_first, n_steps),
        grid=(n_steps,),
        in_specs=[
            pl.BlockSpec((1, 1, tt * TOP_K), lambda i: (i, 0, 0), memory_space=pltpu.SMEM),
            pl.BlockSpec((1, 1, tt * TOP_K), lambda i: (jnp.minimum(i + 1, n_steps - 1), 0, 0),
                         memory_space=pltpu.SMEM),
            pl.BlockSpec((tt, TOP_K), lambda i: (i, 0)),
            pl.BlockSpec((tt * SLAB, LANES), a_map),
            pl.BlockSpec((tt * SLAB, LANES), b_map),
            pl.BlockSpec(memory_space=pl.ANY),
            pl.BlockSpec((1, D_MODEL), lambda i: (0, 0)),
            pl.BlockSpec((1, D_MODEL), lambda i: (0, 0)),
        ],
        out_specs=(pl.BlockSpec((tt, D_MODEL), a_map), pl.BlockSpec((tt, D_MODEL), b_map)),
        out_shape=(jax.ShapeDtypeStruct((n_a, D_MODEL), _F32), jax.ShapeDtypeStruct((n_b, D_MODEL), _F32)),
        scratch_shapes=[pltpu.VMEM((2, TOP_K * tt * SLAB, LANES), _F32), pltpu.SemaphoreType.DMA((2,))],
        compiler_params=pltpu.CompilerParams(
            dimension_semantics=("arbitrary",), vmem_limit_bytes=VMEM_LIMIT_BYTES),
        name="moe_combine",
    )(pos3, pos3, gates, x1s_a, x1s_b, ys, prm["g2"], prm["b2"])


def _route(meta_a, meta_b, gate_a, gate_b, counts_a, counts_b, tm, n_tiles):
    meta = jnp.concatenate([meta_a, meta_b], axis=1)
    idx = meta[0:TOP_K]
    rank = meta[TOP_K:2 * TOP_K]
    gates = jnp.concatenate([gate_a[0:TOP_K], gate_b[0:TOP_K]], axis=1).T
    cp = counts_a.reshape(-1).astype(jnp.int32)
    cs = counts_b.reshape(-1).astype(jnp.int32)
    is_b = (jnp.arange(meta.shape[1]) >= meta_a.shape[1])[None, :]
    is_e = idx[None] == jnp.arange(N_EXPERTS, dtype=idx.dtype)[:, None, None]
    lookup = lambda table: jnp.sum(jnp.where(is_e, table[:, None, None], 0), axis=0)
    rank = rank + jnp.where(is_b, lookup(cp), 0)
    tiles_per = (cp + cs + tm - 1) // tm
    tile_end = jnp.cumsum(tiles_per)
    tile_start = tile_end - tiles_per
    pos = lookup(tile_start) * tm + rank
    n_used = tile_end[-1:]
    tile_ids = jnp.arange(n_tiles, dtype=jnp.int32)
    tile_expert = jnp.minimum(jnp.sum(tile_ids[:, None] >= tile_end[None, :], axis=1), N_EXPERTS - 1)
    last_e = tile_expert[jnp.maximum(n_used[0] - 1, 0)]
    tile_expert = jnp.where(tile_ids < n_used[0], tile_expert, last_e).astype(jnp.int32)
    zero_tiles = jnp.where(tiles_per > 0, tile_end - 1, -1).astype(jnp.int32)
    return pos.astype(jnp.int32), gates, tile_expert, n_used.astype(jnp.int32), zero_tiles


def _rope_tables(pos):
    half = ROT_DIM // 2
    inv_freq = ROPE_THETA ** (-jnp.arange(half, dtype=_F32) / half)
    ang = pos.astype(_F32)[:, None] * inv_freq[None, :]
    cos, sin = jnp.cos(ang), jnp.sin(ang)
    n = pos.shape[0]
    ones = jnp.ones((n, HEAD_DIM - ROT_DIM), _F32)
    zeros = jnp.zeros((n, HEAD_DIM - ROT_DIM), _F32)
    zh = jnp.zeros((n, half), _F32)
    cos_t = jnp.concatenate([cos, cos, ones], axis=1)
    sin_up = jnp.concatenate([-sin, zh, zeros], axis=1)
    sin_dn = jnp.concatenate([zh, sin, zeros], axis=1)
    tab = jnp.stack([cos_t, sin_up, sin_dn])
    return jnp.concatenate([tab, tab], axis=2)


def _block_diag(w):
    hd, d, _ = w.shape
    eye = jnp.eye(hd, dtype=w.dtype)
    return (eye[:, None, :, None] * w[:, :, None, :]).reshape(hd * d, hd * d)


def _gate_blocks(wa, wx):
    blocks = []
    head_of = np.arange(D_RNN) // RNN_HEAD_DIM
    for j, k0 in enumerate(GATE_STARTS):
        cols = np.arange(j * GATE_COLS, (j + 1) * GATE_COLS)
        reach = np.flatnonzero(np.isin(head_of, head_of[cols]))
        assert reach.min() >= k0 and reach.max() < k0 + GATE_K
        cs = slice(j * GATE_COLS, (j + 1) * GATE_COLS)
        blocks.append(jnp.concatenate([wa[k0:k0 + GATE_K, cs], wx[k0:k0 + GATE_K, cs]], axis=1))
    return jnp.stack(blocks)


def kernel(x_prompt, x_sample, cache_k, cache_v, state_conv, state_h, w_in, attn_sinks, w_attn_out, conv_w, conv_b, gate_a_w, gate_a_b, gate_x_w, gate_x_b, lru_lambda, w_rnn_out, w_out, ln1_g, ln1_b, router_w, router_b, w_gate_up, b_gate_up, w_down, b_down, ln2_g, ln2_b):
    assert w_in.shape[0] == DEPTH == 1
    l = 0
    bsz, seq, _ = x_prompt.shape
    dbsz, dseq, _ = x_sample.shape
    row = lambda a: a.reshape(1, -1)
    prm = {
        "w_in": (w_in[l] * jnp.where(jnp.arange(IN_DIM) >= OFF_GA, 0.5, 1.0)).astype(_BF16),
        "conv_w": conv_w[l], "conv_b": row(conv_b[l]),
        "wg": (0.5 * _gate_blocks(_block_diag(gate_a_w[l]), _block_diag(gate_x_w[l]))).astype(_BF16),
        "bg": 0.5 * jnp.concatenate([gate_a_b[l].reshape(-1, 1, GATE_COLS),
                                     gate_x_b[l].reshape(-1, 1, GATE_COLS)], axis=2),
        "lam": row(lru_lambda[l]),
        "wao": w_attn_out[l].astype(_BF16), "wro": w_rnn_out[l].astype(_BF16), "wout": (0.5 * w_out[l]).astype(_BF16),
        "g1": row(ln1_g[l]), "b1": row(ln1_b[l]),
        "rwt": router_w[l].T.astype(_BF16), "rb": router_b[l].reshape(-1, 1),
        "wgu": w_gate_up[l], "bgu": b_gate_up[l][:, None, :],
        "wd": w_down[l], "bd": b_down[l][:, None, :],
        "g2": row(ln2_g[l]), "b2": row(ln2_b[l]),
    }
    sinks = attn_sinks[l]

    rope_p = _rope_tables(jnp.arange(seq, dtype=jnp.int32))
    rope_s = _rope_tables(PAST_LEN + jnp.arange(dseq, dtype=jnp.int32))

    n_prompt = bsz * seq
    n_tok = n_prompt + dbsz * dseq
    zeros_kv = jnp.zeros((bsz, WINDOW, KV_DIM), _F32)
    x1_p, meta_p, gate_p, cnt_p, pk, pv, pc, ph = _mixer(
        x_prompt, rope_p, zeros_kv, zeros_kv, jnp.zeros((bsz, CONV_W - 1, D_RNN), _F32),
        jnp.zeros((bsz, 1, D_RNN), _F32), sinks, prm, has_past=False, nb=1, lt=min(MIXER_ROWS, seq), lq=CHUNK)
    x1_s, meta_s, gate_s, cnt_s, sk, sv, sc, sh = _mixer(
        x_sample, rope_s, cache_k[l].reshape(dbsz, WINDOW, KV_DIM), cache_v[l].reshape(dbsz, WINDOW, KV_DIM),
        state_conv[l], state_h[l][:, None, :], sinks, prm, has_past=True, nb=min(PAST_ROWS // dseq, dbsz),
        lt=dseq, lq=dseq)

    tm = EXPERT_ROWS
    n_tiles = (n_tok * TOP_K + N_EXPERTS * (tm - 1)) // tm + 1
    pos, gates, tile_expert, n_used, zero_tiles = _route(meta_p, meta_s, gate_p, gate_s, cnt_p, cnt_s, tm, n_tiles)
    xs = _dispatch(x1_p, x1_s, pos, zero_tiles, n_used, tt=TOKEN_ROWS, tm=tm, n_tiles=n_tiles)
    ysort = _experts(xs, tile_expert, n_used, prm, tm=tm)
    yp, ys = _combine(x1_p, x1_s, ysort, pos, gates, prm, tt=TOKEN_ROWS)
    yp = yp.reshape(bsz, seq, D_MODEL)
    ys = ys.reshape(dbsz, dseq, D_MODEL)

    kv5 = lambda a: a.reshape(1, a.shape[0], WINDOW, N_KV_HEADS, HEAD_DIM)
    return (yp, ys, kv5(pk), kv5(pv), pc[None], ph.reshape(1, bsz, D_RNN),
            kv5(sk), kv5(sv), sc[None], sh.reshape(1, dbsz, D_RNN))
```
